```python
import math
import jax
import jax.numpy as jnp
from jax import lax
import numpy as np

D_MODEL = 1024
BATCH = 1
SEQ = 16384
DEPTH = 2
DEC_BATCH = 16
DEC_SEQ = 32
PAST_LEN = 2048

CHUNK = 64
Q_BLOCK = 128
EPS = 1e-6
CONV_W = 4

GROUP_WIDTH = D_MODEL // 4
MIX_WIDTH = 4 * GROUP_WIDTH
A_HEADS = 4
A_QK_DIM = GROUP_WIDTH // (2 * A_HEADS)
A_V_DIM = GROUP_WIDTH // A_HEADS
NUM_BUCKETS = 32
REL_MAX_DIST = 256
B_HEADS = 4
B_HEAD_DIM = GROUP_WIDTH // B_HEADS
C_WIDTH = GROUP_WIDTH
C_BLOCKS = 4
C_BLOCK_DIM = C_WIDTH // C_BLOCKS
C_POWER = 8.0
D_HEADS = 4
D_HEAD_DIM = GROUP_WIDTH // D_HEADS
D_WIDTH = GROUP_WIDTH
PEER_HEADS = 8
PEER_N_KEYS = 128
PEER_EXPERTS = PEER_N_KEYS * PEER_N_KEYS
PEER_TOPK = 16
PEER_QUERY_DIM = 256
PEER_HALF = PEER_QUERY_DIM // 2
PEER_BLOCK = 256

IN_SPLITS = (
    2 * A_HEADS * A_QK_DIM, 2 * A_HEADS * A_QK_DIM, A_HEADS * A_V_DIM,
    B_HEADS * B_HEAD_DIM, B_HEADS * B_HEAD_DIM, B_HEADS * B_HEAD_DIM, B_HEADS,
    C_WIDTH, C_WIDTH,
    3 * D_WIDTH, D_WIDTH, D_HEADS, D_HEADS,
)
IN_WIDTH = sum(IN_SPLITS)

kernel_name = "hybrid_stream_encoder_step"

F32 = jnp.float32


def _rmsnorm(x, g):
    xf = x.astype(F32)
    y = xf * lax.rsqrt(jnp.mean(xf * xf, axis=-1, keepdims=True) + EPS)
    return (y * g.astype(F32)).astype(x.dtype)


def _split_cols(z):
    out, off = [], 0
    for n in IN_SPLITS:
        out.append(z[..., off:off + n])
        off += n
    return out


def _t5_bucket(rel):
    half = NUM_BUCKETS // 2
    max_exact = half // 2
    ret = jnp.where(rel > 0, half, 0)
    n = jnp.abs(rel)
    nf = jnp.maximum(n, 1).astype(F32)
    large = max_exact + (jnp.log(nf / max_exact) / math.log(REL_MAX_DIST / max_exact)
                         * (half - max_exact)).astype(jnp.int32)
    large = jnp.minimum(large, half - 1)
    return ret + jnp.where(n < max_exact, n, large)


def _sweep_queries(fn, qpos, *qs):
    sq = qpos.shape[0]
    if sq <= Q_BLOCK:
        return fn(qpos, *qs)
    nb = sq // Q_BLOCK
    pos_b = qpos.reshape(nb, Q_BLOCK)
    qs_b = tuple(jnp.moveaxis(a.reshape(a.shape[0], nb, Q_BLOCK, *a.shape[2:]), 1, 0) for a in qs)
    out = lax.map(lambda args: fn(args[0], *args[1]), (pos_b, qs_b))
    out = jnp.moveaxis(out, 0, 1)
    return out.reshape(out.shape[0], sq, *out.shape[3:])


def _causal_conv(x, prev, w):
    s = x.shape[1]
    xp = jnp.concatenate([prev.astype(x.dtype), x], axis=1)
    y = xp[:, 0:s] * w[0]
    for j in range(1, CONV_W):
        y = y + xp[:, j:j + s] * w[j]
    return y, xp[:, s:]


def _diff_attention(q, k, v, qpos, kpos, rel_bias, lam):
    scale = A_QK_DIM ** -0.5
    k = k.reshape(k.shape[0], k.shape[1], A_HEADS, 2, A_QK_DIM)
    k1, k2 = k[..., 0, :], k[..., 1, :]
    table = rel_bias.astype(F32)
    kchunk = kpos // CHUNK

    def blk(qp, q1b, q2b):
        bias = jnp.take(table, _t5_bucket(kpos[None, :] - qp[:, None]), axis=0)
        bias = jnp.transpose(bias, (2, 0, 1))[None]
        mask = kchunk[None, :] <= (qp // CHUNK)[:, None]

        def probs(qq, kk):
            s = jnp.einsum('bqhd,bkhd->bhqk', qq, kk).astype(F32) * scale + bias
            return jax.nn.softmax(jnp.where(mask, s, -jnp.inf), axis=-1)

        pr = probs(q1b, k1) - lam * probs(q2b, k2)
        return jnp.einsum('bhqk,bkhd->bqhd', pr.astype(v.dtype), v)

    return _sweep_queries(blk, qpos, q[..., 0, :], q[..., 1, :])


def _forget_attention(q, k, v, cum, qpos, kpos, plen):
    scale = B_HEAD_DIM ** -0.5
    c_k = jnp.transpose(cum, (0, 2, 1))[:, :, None, :]
    c_q = cum[:, plen:]

    def blk(qp, qb, cqb):
        s = jnp.einsum('bqhd,bkhd->bhqk', qb, k).astype(F32) * scale
        s = s + jnp.transpose(cqb, (0, 2, 1))[..., None] - c_k
        mask = kpos[None, :] <= qp[:, None]
        pr = jax.nn.softmax(jnp.where(mask, s, -jnp.inf), axis=-1)
        return jnp.einsum('bhqk,bkhd->bqhd', pr.astype(v.dtype), v)

    return _sweep_queries(blk, qpos, q, c_q)


def _linear_scan(a, b, h0):
    b = b.at[:, 0].add(a[:, 0] * h0)

    def comb(left, right):
        al, bl = left
        ar, br = right
        return al * ar, ar * bl + br

    _, h = lax.associative_scan(comb, (a, b), axis=1)
    return h


def _rg_lru(x, gate, h0, conv0, w_conv, b_conv, wa, ba, wx, bx, lam_param):
    bsz, s, _ = x.shape
    xc, conv_new = _causal_conv(x, conv0, w_conv)
    xc = xc + b_conv
    xh = xc.reshape(bsz, s, C_BLOCKS, C_BLOCK_DIM)
    r = jax.nn.sigmoid(jnp.einsum('bsni,nij->bsnj', xh, wa).reshape(bsz, s, C_WIDTH).astype(F32) + ba.astype(F32))
    i = jax.nn.sigmoid(jnp.einsum('bsni,nij->bsnj', xh, wx).reshape(bsz, s, C_WIDTH).astype(F32) + bx.astype(F32))
    log_a = -C_POWER * r * jax.nn.softplus(-lam_param.astype(F32))
    a = jnp.exp(log_a)
    b = jnp.sqrt(-jnp.expm1(2.0 * log_a)) * i * xc.astype(F32)
    h = _linear_scan(a, b, h0.astype(F32))
    y = h * jax.nn.gelu(gate.astype(F32))
    return y.astype(x.dtype), h[:, -1], conv_new


def _gated_delta_chunked(q, k, v, beta, g, s0):
    bsz, s, nh, dh = q.shape
    c = min(CHUNK, s)
    n = s // c

    def blocks(t):
        t = t.reshape(bsz, n, c, nh, *t.shape[3:])
        return jnp.moveaxis(t, 3, 1)

    q, k, v, beta, g = blocks(q), blocks(k), blocks(v), blocks(beta), blocks(g)
    gcum = jnp.cumsum(g, axis=-1)
    idx = jnp.arange(c)
    incl = idx[:, None] >= idx[None, :]
    strict = idx[:, None] > idx[None, :]
    decay = jnp.exp(jnp.where(incl, gcum[..., :, None] - gcum[..., None, :], -jnp.inf))
    kb = k * beta[..., None]
    a_mat = jnp.eye(c, dtype=F32) + jnp.where(strict, jnp.einsum('...id,...jd->...ij', kb, k) * decay, 0.0)
    v_w = lax.linalg.triangular_solve(a_mat, v * beta[..., None], left_side=True, lower=True)
    k_w = lax.linalg.triangular_solve(a_mat, kb * jnp.exp(gcum)[..., None], left_side=True, lower=True)
    attn = jnp.einsum('...id,...jd->...ij', q, k) * decay
    q_g = q * jnp.exp(gcum)[..., None]
    k_d = k * jnp.exp(gcum[..., -1:] - gcum)[..., None]
    g_last = jnp.exp(gcum[..., -1])
    xs = tuple(jnp.moveaxis(t, 2, 0) for t in (v_w, k_w, attn, q_g, k_d, g_last))

    def step(state, inp):
        v_w_c, k_w_c, attn_c, q_g_c, k_d_c, g_c = inp
        v_new = v_w_c - jnp.einsum('bhcd,bhde->bhce', k_w_c, state)
        o = jnp.einsum('bhcd,bhde->bhce', q_g_c, state) + jnp.einsum('bhij,bhje->bhie', attn_c, v_new)
        state = state * g_c[..., None, None] + jnp.einsum('bhcd,bhce->bhde', k_d_c, v_new)
        return state, o

    s_final, o = lax.scan(step, s0, xs)
    o = jnp.moveaxis(o, 0, 2)
    o = jnp.moveaxis(o, 1, 3).reshape(bsz, s, nh, dh)
    return o, s_final


def _gated_deltanet(qkv, z, beta_logit, alpha_logit, s0, conv0, w_conv, a_log, dt_bias, norm_g):
    bsz, s, _ = qkv.shape
    y, conv_new = _causal_conv(qkv, conv0, w_conv)
    y = jax.nn.silu(y.astype(F32))
    q, k, v = jnp.split(y, 3, axis=-1)
    q = q.reshape(bsz, s, D_HEADS, D_HEAD_DIM)
    k = k.reshape(bsz, s, D_HEADS, D_HEAD_DIM)
    v = v.reshape(bsz, s, D_HEADS, D_HEAD_DIM)
    q = q * lax.rsqrt(jnp.sum(q * q, axis=-1, keepdims=True) + EPS) * (D_HEAD_DIM ** -0.5)
    k = k * lax.rsqrt(jnp.sum(k * k, axis=-1, keepdims=True) + EPS)
    beta = jax.nn.sigmoid(beta_logit.astype(F32))
    g = -jnp.exp(a_log.astype(F32)) * jax.nn.softplus(alpha_logit.astype(F32) + dt_bias.astype(F32))
    o, s_new = _gated_delta_chunked(q, k, v, beta, g, s0.astype(F32))
    o = _rmsnorm(o, norm_g) * jax.nn.silu(z.reshape(bsz, s, D_HEADS, D_HEAD_DIM).astype(F32))
    return o.reshape(bsz, s, D_WIDTH).astype(qkv.dtype), s_new, conv_new


def _peer(h, wq, subkeys, u, v):
    bsz, s, d = h.shape
    t = bsz * s
    hf = h.reshape(t, d)
    q = (hf @ wq).astype(F32).reshape(t, PEER_HEADS, 2, PEER_HALF)
    sk = subkeys.astype(F32)
    s1 = jnp.einsum('thd,nd->thn', q[:, :, 0], sk[0])
    s2 = jnp.einsum('thd,nd->thn', q[:, :, 1], sk[1])
    v1, i1 = lax.top_k(s1, PEER_TOPK)
    v2, i2 = lax.top_k(s2, PEER_TOPK)
    cand = (v1[..., :, None] + v2[..., None, :]).reshape(t, PEER_HEADS, PEER_TOPK * PEER_TOPK)
    cidx = (i1[..., :, None] * PEER_N_KEYS + i2[..., None, :]).reshape(t, PEER_HEADS, PEER_TOPK * PEER_TOPK)
    top, pos = lax.top_k(cand, PEER_TOPK)
    eidx = jnp.take_along_axis(cidx, pos, axis=-1).reshape(t, PEER_HEADS * PEER_TOPK)
    gate = jax.nn.softmax(top, axis=-1).reshape(t, PEER_HEADS * PEER_TOPK)

    def blk(args):
        hb, eb, gb = args
        act = jax.nn.gelu(jnp.einsum('td,tkd->tk', hb, jnp.take(u, eb, axis=0)).astype(F32))
        return jnp.einsum('tk,tkd->td', (gb * act).astype(v.dtype), jnp.take(v, eb, axis=0))

    if t <= PEER_BLOCK:
        out = blk((hf, eidx, gate))
    else:
        nb = -(-t // PEER_BLOCK)
        pad = nb * PEER_BLOCK - t
        hp = jnp.pad(hf, ((0, pad), (0, 0))).reshape(nb, PEER_BLOCK, d)
        ep = jnp.pad(eidx, ((0, pad), (0, 0))).reshape(nb, PEER_BLOCK, -1)
        gp = jnp.pad(gate, ((0, pad), (0, 0))).reshape(nb, PEER_BLOCK, -1)
        out = lax.map(blk, (hp, ep, gp)).reshape(nb * PEER_BLOCK, d)[:t]
    return out.reshape(bsz, s, d).astype(h.dtype)


def _layer(x, past, p, lam_init):
    pa_k, pa_v, pb_k, pb_v, pb_logf, c_h0, c_conv0, d_s0, d_conv0 = past
    bsz, s, _ = x.shape
    plen = pa_k.shape[1]
    qpos = plen + jnp.arange(s, dtype=jnp.int32)
    kpos = jnp.arange(plen + s, dtype=jnp.int32)

    h = _rmsnorm(x, p['norm1_g'])
    (a_q, a_k, a_v, b_q, b_k, b_v, b_f, c_x, c_g, d_qkv, d_z, d_b, d_a) = _split_cols(h @ p['w_in'])

    a_q = a_q.reshape(bsz, s, A_HEADS, 2, A_QK_DIM)
    a_k = a_k.reshape(bsz, s, A_HEADS, 2 * A_QK_DIM)
    a_v = a_v.reshape(bsz, s, A_HEADS, A_V_DIM)
    lp = p['a_lambda'].astype(F32)
    lam = jnp.exp(jnp.sum(lp[0] * lp[1])) - jnp.exp(jnp.sum(lp[2] * lp[3])) + lam_init
    o_a = _diff_attention(a_q, jnp.concatenate([pa_k, a_k], axis=1), jnp.concatenate([pa_v, a_v], axis=1),
                          qpos, kpos, p['rel_bias'], lam)
    o_a = (_rmsnorm(o_a, p['a_norm_g']) * (1.0 - lam_init)).reshape(bsz, s, GROUP_WIDTH)

    b_q = b_q.reshape(bsz, s, B_HEADS, B_HEAD_DIM)
    b_k = b_k.reshape(bsz, s, B_HEADS, B_HEAD_DIM)
    b_v = b_v.reshape(bsz, s, B_HEADS, B_HEAD_DIM)
    b_logf = jax.nn.log_sigmoid(b_f.astype(F32) + p['b_forget_bias'].astype(F32))
    cum = jnp.cumsum(jnp.concatenate([pb_logf.astype(F32), b_logf], axis=1), axis=1)
    o_b = _forget_attention(b_q, jnp.concatenate([pb_k, b_k], axis=1), jnp.concatenate([pb_v, b_v], axis=1),
                            cum, qpos, kpos, plen).reshape(bsz, s, GROUP_WIDTH)

    o_c, c_h, c_conv = _rg_lru(c_x, c_g, c_h0, c_conv0, p['c_conv_w'], p['c_conv_b'],
                               p['c_gate_a_w'], p['c_gate_a_b'], p['c_gate_x_w'], p['c_gate_x_b'], p['c_lambda'])

    o_d, d_s, d_conv = _gated_deltanet(d_qkv, d_z, d_b, d_a, d_s0, d_conv0, p['d_conv_w'],
                                       p['d_a_log'], p['d_dt_bias'], p['d_norm_g'])

    mix = jnp.concatenate([o_a.astype(x.dtype), o_b.astype(x.dtype), o_c.astype(x.dtype), o_d.astype(x.dtype)], axis=-1)
    x = x + mix @ p['w_out']
    x = x + _peer(_rmsnorm(x, p['norm2_g']), p['peer_wq'], p['peer_subkeys'], p['peer_u'], p['peer_v'])
    return x, (a_k, a_v, b_k, b_v, b_logf, c_h, c_conv, d_s, d_conv)


def setup_inputs(seed: int = 0) -> dict:
    key = jax.random.key(seed)
    ks = iter(jax.random.split(key, 48))

    def nrm(shape, scale):
        return jax.random.normal(next(ks), shape, F32) * scale

    def unif(shape, lo, hi):
        return jax.random.uniform(next(ks), shape, F32, minval=lo, maxval=hi)

    x_prompt = nrm((BATCH, SEQ, D_MODEL), 1.0)
    x_sample = nrm((DEC_BATCH, DEC_SEQ, D_MODEL), 1.0)
    cache_a_k = nrm((DEPTH, DEC_BATCH, PAST_LEN, A_HEADS, 2 * A_QK_DIM), 1.0)
    cache_a_v = nrm((DEPTH, DEC_BATCH, PAST_LEN, A_HEADS, A_V_DIM), 1.0)
    cache_b_k = nrm((DEPTH, DEC_BATCH, PAST_LEN, B_HEADS, B_HEAD_DIM), 1.0)
    cache_b_v = nrm((DEPTH, DEC_BATCH, PAST_LEN, B_HEADS, B_HEAD_DIM), 1.0)
    cache_b_logf = jax.nn.log_sigmoid(2.0 + nrm((DEPTH, DEC_BATCH, PAST_LEN, B_HEADS), 1.0))
    state_c_h = nrm((DEPTH, DEC_BATCH, C_WIDTH), 0.5)
    state_c_conv = nrm((DEPTH, DEC_BATCH, CONV_W - 1, C_WIDTH), 1.0)
    state_d_s = nrm((DEPTH, DEC_BATCH, D_HEADS, D_HEAD_DIM, D_HEAD_DIM), 0.1)
    state_d_conv = nrm((DEPTH, DEC_BATCH, CONV_W - 1, 3 * D_WIDTH), 1.0)

    norm1_g = 1.0 + nrm((DEPTH, D_MODEL), 0.02)
    norm2_g = 1.0 + nrm((DEPTH, D_MODEL), 0.02)
    final_norm_g = 1.0 + nrm((D_MODEL,), 0.02)
    w_in = nrm((DEPTH, D_MODEL, IN_WIDTH), D_MODEL ** -0.5)
    w_out = nrm((DEPTH, MIX_WIDTH, D_MODEL), MIX_WIDTH ** -0.5)
    rel_bias = nrm((NUM_BUCKETS, A_HEADS), 0.5)
    a_lambda = nrm((DEPTH, 4, A_QK_DIM), 0.1)
    a_norm_g = 1.0 + nrm((DEPTH, A_V_DIM), 0.02)
    b_forget_bias = 2.0 + nrm((DEPTH, B_HEADS), 0.1)
    c_conv_w = nrm((DEPTH, CONV_W, C_WIDTH), CONV_W ** -0.5)
    c_conv_b = nrm((DEPTH, C_WIDTH), 0.02)
    c_gate_a_w = nrm((DEPTH, C_BLOCKS, C_BLOCK_DIM, C_BLOCK_DIM), C_BLOCK_DIM ** -0.5)
    c_gate_a_b = nrm((DEPTH, C_WIDTH), 0.02)
    c_gate_x_w = nrm((DEPTH, C_BLOCKS, C_BLOCK_DIM, C_BLOCK_DIM), C_BLOCK_DIM ** -0.5)
    c_gate_x_b = nrm((DEPTH, C_WIDTH), 0.02)
    a0 = unif((DEPTH, C_WIDTH), 0.9, 0.999)
    sg = a0 ** (1.0 / C_POWER)
    c_lambda = jnp.log(sg) - jnp.log1p(-sg)
    d_conv_w = nrm((DEPTH, CONV_W, 3 * D_WIDTH), CONV_W ** -0.5)
    d_a_log = jnp.log(unif((DEPTH, D_HEADS), 1.0, 16.0))
    dt = jnp.exp(unif((DEPTH, D_HEADS), math.log(1e-3), math.log(1e-1)))
    d_dt_bias = dt + jnp.log(-jnp.expm1(-dt))
    d_norm_g = 1.0 + nrm((DEPTH, D_HEAD_DIM), 0.02)
    peer_wq = nrm((DEPTH, D_MODEL, PEER_HEADS * PEER_QUERY_DIM), D_MODEL ** -0.5)
    peer_subkeys = nrm((DEPTH, 2, PEER_N_KEYS, PEER_HALF), PEER_HALF ** -0.5)
    peer_u = nrm((DEPTH, PEER_EXPERTS, D_MODEL), D_MODEL ** -0.5)
    peer_v = nrm((DEPTH, PEER_EXPERTS, D_MODEL), PEER_HEADS ** -0.5)

    return {
        'x_prompt': x_prompt, 'x_sample': x_sample,
        'cache_a_k': cache_a_k, 'cache_a_v': cache_a_v,
        'cache_b_k': cache_b_k, 'cache_b_v': cache_b_v, 'cache_b_logf': cache_b_logf,
        'state_c_h': state_c_h, 'state_c_conv': state_c_conv,
        'state_d_s': state_d_s, 'state_d_conv': state_d_conv,
        'norm1_g': norm1_g, 'norm2_g': norm2_g, 'final_norm_g': final_norm_g,
        'w_in': w_in, 'w_out': w_out, 'rel_bias': rel_bias,
        'a_lambda': a_lambda, 'a_norm_g': a_norm_g, 'b_forget_bias': b_forget_bias,
        'c_conv_w': c_conv_w, 'c_conv_b': c_conv_b,
        'c_gate_a_w': c_gate_a_w, 'c_gate_a_b': c_gate_a_b,
        'c_gate_x_w': c_gate_x_w, 'c_gate_x_b': c_gate_x_b, 'c_lambda': c_lambda,
        'd_conv_w': d_conv_w, 'd_a_log': d_a_log, 'd_dt_bias': d_dt_bias, 'd_norm_g': d_norm_g,
        'peer_wq': peer_wq, 'peer_subkeys': peer_subkeys, 'peer_u': peer_u, 'peer_v': peer_v,
    }


def reference(x_prompt, x_sample, cache_a_k, cache_a_v, cache_b_k, cache_b_v, cache_b_logf,
              state_c_h, state_c_conv, state_d_s, state_d_conv,
              norm1_g, norm2_g, final_norm_g, w_in, w_out, rel_bias, a_lambda, a_norm_g, b_forget_bias,
              c_conv_w, c_conv_b, c_gate_a_w, c_gate_a_b, c_gate_x_w, c_gate_x_b, c_lambda,
              d_conv_w, d_a_log, d_dt_bias, d_norm_g, peer_wq, peer_subkeys, peer_u, peer_v):
    bsz = x_prompt.shape[0]
    dt = x_prompt.dtype
    xp, xs = x_prompt, x_sample
    prompt_out, sample_out = [], []
    for l in range(DEPTH):
        lp = {
            'norm1_g': norm1_g[l], 'norm2_g': norm2_g[l], 'w_in': w_in[l], 'w_out': w_out[l],
            'rel_bias': rel_bias, 'a_lambda': a_lambda[l], 'a_norm_g': a_norm_g[l],
            'b_forget_bias': b_forget_bias[l],
            'c_conv_w': c_conv_w[l], 'c_conv_b': c_conv_b[l],
            'c_gate_a_w': c_gate_a_w[l], 'c_gate_a_b': c_gate_a_b[l],
            'c_gate_x_w': c_gate_x_w[l], 'c_gate_x_b': c_gate_x_b[l], 'c_lambda': c_lambda[l],
            'd_conv_w': d_conv_w[l], 'd_a_log': d_a_log[l], 'd_dt_bias': d_dt_bias[l], 'd_norm_g': d_norm_g[l],
            'peer_wq': peer_wq[l], 'peer_subkeys': peer_subkeys[l], 'peer_u': peer_u[l], 'peer_v': peer_v[l],
        }
        lam_init = 0.8 - 0.6 * math.exp(-0.3 * l)
        empty = (
            jnp.zeros((bsz, 0, A_HEADS, 2 * A_QK_DIM), dt), jnp.zeros((bsz, 0, A_HEADS, A_V_DIM), dt),
            jnp.zeros((bsz, 0, B_HEADS, B_HEAD_DIM), dt), jnp.zeros((bsz, 0, B_HEADS, B_HEAD_DIM), dt),
            jnp.zeros((bsz, 0, B_HEADS), dt),
            jnp.zeros((bsz, C_WIDTH), dt), jnp.zeros((bsz, CONV_W - 1, C_WIDTH), dt),
            jnp.zeros((bsz, D_HEADS, D_HEAD_DIM, D_HEAD_DIM), dt), jnp.zeros((bsz, CONV_W - 1, 3 * D_WIDTH), dt),
        )
        xp, st_p = _layer(xp, empty, lp, lam_init)
        prompt_out.append(st_p)
        past = (cache_a_k[l], cache_a_v[l], cache_b_k[l], cache_b_v[l], cache_b_logf[l],
                state_c_h[l], state_c_conv[l], state_d_s[l], state_d_conv[l])
        xs, st_s = _layer(xs, past, lp, lam_init)
        sample_out.append(st_s)

    y_prompt = _rmsnorm(xp, final_norm_g)
    y_sample = _rmsnorm(xs, final_norm_g)
    (p_a_k, p_a_v, p_b_k, p_b_v, p_b_logf, p_c_h, p_c_conv, p_d_s, p_d_conv) = [
        jnp.stack(z, axis=0) for z in zip(*prompt_out)]
    (s_a_k, s_a_v, s_b_k, s_b_v, s_b_logf, s_c_h, s_c_conv, s_d_s, s_d_conv) = [
        jnp.stack(z, axis=0) for z in zip(*sample_out)]
    return (y_prompt, y_sample,
            p_a_k, p_a_v, p_b_k, p_b_v, p_b_logf, p_c_h, p_c_conv, p_d_s, p_d_conv,
            s_a_k, s_a_v, s_b_k, s_b_v, s_b_logf, s_c_h, s_c_conv, s_d_s, s_d_conv)
```

```python
import functools
import math

import numpy as np
import jax
import jax.numpy as jnp
from jax import lax
from jax.experimental import pallas as pl
from jax.experimental.pallas import tpu as pltpu

F32 = jnp.float32
BF16 = jnp.bfloat16
HI = lax.Precision.HIGHEST

D_MODEL = 1024
DEPTH = 2
CHUNK = 64
EPS = 1e-6
CONV_W = 4
GROUP_WIDTH = D_MODEL // 4
A_HEADS = 4
A_QK_DIM = GROUP_WIDTH // (2 * A_HEADS)
A_V_DIM = GROUP_WIDTH // A_HEADS
NUM_BUCKETS = 32
REL_MAX_DIST = 256
B_HEADS = 4
B_HEAD_DIM = GROUP_WIDTH // B_HEADS
C_WIDTH = GROUP_WIDTH
C_BLOCKS = 4
C_BLOCK_DIM = C_WIDTH // C_BLOCKS
C_POWER = 8.0
D_HEADS = 4
D_HEAD_DIM = GROUP_WIDTH // D_HEADS
D_WIDTH = GROUP_WIDTH
PEER_HEADS = 8
PEER_N_KEYS = 128
PEER_EXPERTS = PEER_N_KEYS * PEER_N_KEYS
PEER_TOPK = 16
PEER_QUERY_DIM = 256
PEER_HALF = PEER_QUERY_DIM // 2

LANES = 128
NEG_BIG = -1e30
VMEM_LIMIT = 56 * 1024 * 1024


def _cparams(sem):
    return pltpu.CompilerParams(dimension_semantics=sem, vmem_limit_bytes=VMEM_LIMIT)


def _dot(a, b, precision=None):
    return jnp.dot(a, b, preferred_element_type=F32, precision=precision)


def _dot_nt(a, b, precision=None):
    return lax.dot_general(a, b, (((1,), (1,)), ((), ())), preferred_element_type=F32, precision=precision)


def _dot_tn(a, b, precision=None):
    return lax.dot_general(a, b, (((0,), (0,)), ((), ())), preferred_element_type=F32, precision=precision)


_IN_WIDTHS = (256,) * 8 + (768, 256, LANES)
_IN_TOTAL = sum(_IN_WIDTHS)


def _permute_w_in(w):
    pad = jnp.zeros((w.shape[0], LANES - 12), w.dtype)
    return jnp.concatenate([w[:, 0:1536], w[:, 1540:3076], w[:, 1536:1540], w[:, 3076:3084], pad], axis=1).astype(BF16)


def _in_proj_body(x_ref, g_ref, w_ref, *outs):
    x = x_ref[...]
    h = x * lax.rsqrt(jnp.mean(x * x, axis=-1, keepdims=True) + EPS) * g_ref[...]
    hb = h.astype(BF16)
    off = 0
    for o_ref, n in zip(outs, _IN_WIDTHS):
        o_ref[...] = _dot(hb, w_ref[:, off:off + n])
        off += n


def _in_proj(x, g, w_perm, tm):
    t = x.shape[0]
    return pl.pallas_call(
        _in_proj_body,
        grid=(t // tm,),
        in_specs=[pl.BlockSpec((tm, D_MODEL), lambda i: (i, 0)),
                  pl.BlockSpec((1, D_MODEL), lambda i: (0, 0)),
                  pl.BlockSpec((D_MODEL, _IN_TOTAL), lambda i: (0, 0))],
        out_specs=[pl.BlockSpec((tm, n), lambda i: (i, 0)) for n in _IN_WIDTHS],
        out_shape=[jax.ShapeDtypeStruct((t, n), F32) for n in _IN_WIDTHS],
        compiler_params=_cparams(("parallel",)),
        name="in_proj",
    )(x, g.reshape(1, D_MODEL), w_perm)


def _logf_cumsum_body(v_ref, b_ref, logf_ref, cum_ref, *, plen, rows):
    vals = v_ref[0, 0]
    pos = lax.broadcasted_iota(jnp.int32, (rows, LANES), 0) * LANES + lax.broadcasted_iota(jnp.int32, (rows, LANES), 1)
    logf = jnp.where(pos >= plen, jax.nn.log_sigmoid(vals + b_ref[0]), vals)
    logf_ref[0, 0] = logf
    kk = lax.broadcasted_iota(jnp.int32, (LANES, LANES), 0)
    jj = lax.broadcasted_iota(jnp.int32, (LANES, LANES), 1)
    in_row = _dot(logf, (kk <= jj).astype(F32), HI)
    tot = jnp.broadcast_to(in_row[:, LANES - 1:LANES], (rows, LANES))
    ri = lax.broadcasted_iota(jnp.int32, (rows, rows), 0)
    rj = lax.broadcasted_iota(jnp.int32, (rows, rows), 1)
    cum_ref[0, 0] = in_row + _dot((rj < ri).astype(F32), tot, HI)


def _logf_cumsum(vals, bias, plen):
    b, h, rows, _ = vals.shape
    spec = pl.BlockSpec((1, 1, rows, LANES), lambda i, j: (i, j, 0, 0))
    return pl.pallas_call(
        functools.partial(_logf_cumsum_body, plen=plen, rows=rows),
        grid=(b, h),
        in_specs=[spec, pl.BlockSpec((1, 1, 1), lambda i, j: (j, 0, 0))],
        out_specs=[spec, spec],
        out_shape=[jax.ShapeDtypeStruct(vals.shape, F32)] * 2,
        compiler_params=_cparams(("parallel", "parallel")),
        name="logf_cumsum",
    )(vals, bias.reshape(h, 1, 1))


def _attn_schedule(sq, sk, plen, bq, bk, chunk):
    qi, kj, flags = [], [], []
    for i in range(sq // bq):
        q_hi = (plen + i * bq + bq - 1) // chunk
        js = [j for j in range(sk // bk) if (j * bk) // chunk <= q_hi]
        for n, j in enumerate(js):
            qi.append(i)
            kj.append(j)
            flags.append((1 if n == 0 else 0) | (2 if n == len(js) - 1 else 0))
    return np.asarray(qi, np.int32), np.asarray(kj, np.int32), np.asarray(flags, np.int32)


def _t5_bucket(rel):
    half = NUM_BUCKETS // 2
    max_exact = half // 2
    ret = jnp.where(rel > 0, half, 0)
    n = jnp.abs(rel)
    nf = jnp.maximum(n, 1).astype(F32)
    large = max_exact + (jnp.log(nf / max_exact) / math.log(REL_MAX_DIST / max_exact)
                         * (half - max_exact)).astype(jnp.int32)
    large = jnp.minimum(large, half - 1)
    return ret + jnp.where(n < max_exact, n, large)


def _diff_bias_tiles(rel_bias, qi, kj, plen, bq, bk):
    deltas = [int(plen + i * bq - j * bk) for i, j in zip(qi, kj)]
    far = bk + 2 * REL_MAX_DIST
    keys = sorted({min(d, far) for d in deltas})
    tid = np.asarray([keys.index(min(d, far)) for d in deltas], np.int32)
    r = jnp.arange(bq, dtype=jnp.int32)[:, None]
    c = jnp.arange(bk, dtype=jnp.int32)[None, :]
    tiles = []
    for d in keys:
        rel = c - r - d
        bias = jnp.take(rel_bias.astype(F32), _t5_bucket(rel), axis=0)
        vis = (c // CHUNK) <= ((r + d) // CHUNK)
        tiles.append(jnp.where(vis[None], jnp.transpose(bias, (2, 0, 1)), NEG_BIG))
    return jnp.stack(tiles, axis=0), tid


def _online_softmax_step(s, v, m_ref, l_ref, acc_ref):
    m_old = m_ref[...]
    m_new = jnp.maximum(m_old, jnp.max(s, axis=-1, keepdims=True))
    alpha = jnp.exp(m_old - m_new)
    p = jnp.exp(s - m_new)
    l_ref[...] = alpha * l_ref[...] + jnp.sum(p, axis=-1, keepdims=True)
    acc_ref[...] = alpha * acc_ref[...] + _dot(p.astype(BF16), v)
    m_ref[...] = m_new


def _diff_attn_body(qi_ref, kj_ref, tid_ref, fl_ref,
                    q1_ref, q2_ref, k1_ref, k2_ref, v_ref, bias_ref, lam_ref, g_ref,
                    o_ref, m1, l1, a1, m2, l2, a2, *, lam_init):
    s = pl.program_id(2)
    fl = fl_ref[s]

    @pl.when((fl & 1) != 0)
    def _():
        for m, l, a in ((m1, l1, a1), (m2, l2, a2)):
            m[...] = jnp.full(m.shape, -jnp.inf, F32)
            l[...] = jnp.zeros(l.shape, F32)
            a[...] = jnp.zeros(a.shape, F32)

    bias = bias_ref[tid_ref[s], 0]
    v = v_ref[0, 0].astype(BF16)
    scale = A_QK_DIM ** -0.5
    for q_ref, k_ref, m, l, a in ((q1_ref, k1_ref, m1, l1, a1), (q2_ref, k2_ref, m2, l2, a2)):
        q = (q_ref[0, 0] * scale).astype(BF16)
        sc = _dot_nt(q, k_ref[0, 0].astype(BF16)) + bias
        _online_softmax_step(sc, v, m, l, a)

    @pl.when((fl & 2) != 0)
    def _():
        lp = lam_ref[...]
        lam = (jnp.exp(jnp.sum(lp[0:1] * lp[1:2], axis=-1, keepdims=True))
               - jnp.exp(jnp.sum(lp[2:3] * lp[3:4], axis=-1, keepdims=True)) + lam_init)
        o = a1[...] / l1[...] - lam * (a2[...] / l2[...])
        o = o * lax.rsqrt(jnp.mean(o * o, axis=-1, keepdims=True) + EPS) * g_ref[...]
        o_ref[0, 0] = o * (1.0 - lam_init)


def _diff_attn(q1, q2, k1, k2, v, rel_bias, a_lambda, a_norm_g, plen, bq, bk, lam_init):
    b, h, sq, dq = q1.shape
    sk, dv = v.shape[2], v.shape[3]
    qi, kj, fl = _attn_schedule(sq, sk, plen, bq, bk, CHUNK)
    tiles, tid = _diff_bias_tiles(rel_bias, qi, kj, plen, bq, bk)
    nt = tiles.shape[0]
    qspec = pl.BlockSpec((1, 1, bq, dq), lambda b_, h_, s, qi_, kj_, t_, f_: (b_, h_, qi_[s], 0))
    kspec = pl.BlockSpec((1, 1, bk, dq), lambda b_, h_, s, qi_, kj_, t_, f_: (b_, h_, kj_[s], 0))
    grid_spec = pltpu.PrefetchScalarGridSpec(
        num_scalar_prefetch=4,
        grid=(b, h, len(qi)),
        in_specs=[qspec, qspec, kspec, kspec,
                  pl.BlockSpec((1, 1, bk, dv), lambda b_, h_, s, qi_, kj_, t_, f_: (b_, h_, kj_[s], 0)),
                  pl.BlockSpec((nt, 1, bq, bk), lambda b_, h_, s, *_: (0, h_, 0, 0)),
                  pl.BlockSpec((4, A_QK_DIM), lambda b_, h_, s, *_: (0, 0)),
                  pl.BlockSpec((1, dv), lambda b_, h_, s, *_: (0, 0))],
        out_specs=pl.BlockSpec((1, 1, bq, dv), lambda b_, h_, s, qi_, kj_, t_, f_: (b_, h_, qi_[s], 0)),
        scratch_shapes=[pltpu.VMEM((bq, 1), F32), pltpu.VMEM((bq, 1), F32), pltpu.VMEM((bq, dv), F32),
                        pltpu.VMEM((bq, 1), F32), pltpu.VMEM((bq, 1), F32), pltpu.VMEM((bq, dv), F32)],
    )
    return pl.pallas_call(
        functools.partial(_diff_attn_body, lam_init=lam_init),
        grid_spec=grid_spec,
        out_shape=jax.ShapeDtypeStruct((b, h, sq, dv), F32),
        compiler_params=_cparams(("parallel", "parallel", "arbitrary")),
        name="diff_attn",
    )(jnp.asarray(qi), jnp.asarray(kj), jnp.asarray(tid), jnp.asarray(fl),
      q1, q2, k1, k2, v, tiles, a_lambda, a_norm_g.reshape(1, dv))


def _forget_attn_body(qi_ref, kj_ref, fl_ref, q_ref, k_ref, v_ref, ck_ref, cref_ref,
                      o_ref, m, l, acc, *, plen, bq, bk):
    s = pl.program_id(2)
    fl = fl_ref[s]
    q_lo = plen + qi_ref[s] * bq
    k_lo = kj_ref[s] * bk

    @pl.when((fl & 1) != 0)
    def _():
        m[...] = jnp.full(m.shape, -jnp.inf, F32)
        l[...] = jnp.zeros(l.shape, F32)
        acc[...] = jnp.zeros(acc.shape, F32)

    def scores():
        q = (q_ref[0, 0] * (B_HEAD_DIM ** -0.5)).astype(BF16)
        return _dot_nt(q, k_ref[0, 0].astype(BF16)) - (ck_ref[0, 0] - cref_ref[0, 0, 0])

    straddles = k_lo + (bk - 1) > q_lo

    @pl.when(straddles)
    def _():
        kpos = k_lo + lax.broadcasted_iota(jnp.int32, (bq, bk), 1)
        qpos = q_lo + lax.broadcasted_iota(jnp.int32, (bq, bk), 0)
        _online_softmax_step(jnp.where(kpos <= qpos, scores(), NEG_BIG), v_ref[0, 0].astype(BF16), m, l, acc)

    @pl.when(jnp.logical_not(straddles))
    def _():
        _online_softmax_step(scores(), v_ref[0, 0].astype(BF16), m, l, acc)

    @pl.when((fl & 2) != 0)
    def _():
        o_ref[0, 0] = acc[...] / l[...]


def _forget_attn(q, k, v, cum, plen, bq, bk):
    b, h, sq, d = q.shape
    sk = k.shape[2]
    qi, kj, fl = _attn_schedule(sq, sk, plen, bq, bk, 1)
    nq = sq // bq
    cref = cum[:, :, plen:plen + sq:bq].reshape(b, h, nq, 1, 1)
    ck = cum.reshape(b, h, 1, sk)
    grid_spec = pltpu.PrefetchScalarGridSpec(
        num_scalar_prefetch=3,
        grid=(b, h, len(qi)),
        in_specs=[pl.BlockSpec((1, 1, bq, d), lambda b_, h_, s, qi_, kj_, f_: (b_, h_, qi_[s], 0)),
                  pl.BlockSpec((1, 1, bk, d), lambda b_, h_, s, qi_, kj_, f_: (b_, h_, kj_[s], 0)),
                  pl.BlockSpec((1, 1, bk, d), lambda b_, h_, s, qi_, kj_, f_: (b_, h_, kj_[s], 0)),
                  pl.BlockSpec((1, 1, 1, bk), lambda b_, h_, s, qi_, kj_, f_: (b_, h_, 0, kj_[s])),
                  pl.BlockSpec((1, 1, 1, 1, 1), lambda b_, h_, s, qi_, kj_, f_: (b_, h_, qi_[s], 0, 0))],
        out_specs=pl.BlockSpec((1, 1, bq, d), lambda b_, h_, s, qi_, kj_, f_: (b_, h_, qi_[s], 0)),
        scratch_shapes=[pltpu.VMEM((bq, 1), F32), pltpu.VMEM((bq, 1), F32), pltpu.VMEM((bq, d), F32)],
    )
    return pl.pallas_call(
        functools.partial(_forget_attn_body, plen=plen, bq=bq, bk=bk),
        grid_spec=grid_spec,
        out_shape=jax.ShapeDtypeStruct((b, h, sq, d), F32),
        compiler_params=_cparams(("parallel", "parallel", "arbitrary")),
        name="forget_attn",
    )(jnp.asarray(qi), jnp.asarray(kj), jnp.asarray(fl), q, k, v, ck, cref)


_CONV_PAD = 8


def _causal_conv_tile(buf, x, cw_ref, rows):
    buf[_CONV_PAD:_CONV_PAD + rows, :] = x
    lo = _CONV_PAD - (CONV_W - 1)
    y = buf[lo:lo + rows, :] * cw_ref[0:1, :]
    for j in range(1, CONV_W):
        y = y + buf[lo + j:lo + j + rows, :] * cw_ref[j:j + 1, :]
    buf[lo:_CONV_PAD, :] = x[rows - (CONV_W - 1):rows, :]
    return y


def _rg_lru_body(x_ref, gate_ref, h0_ref, conv0_ref, cw_ref, cb_ref, wa_ref, ba_ref, wx_ref, bx_ref, lam_ref,
                 y_ref, hn_ref, convn_ref, buf, hcar, *, ts):
    n = pl.program_id(1)

    @pl.when(n == 0)
    def _():
        buf[_CONV_PAD - (CONV_W - 1):_CONV_PAD, :] = conv0_ref[0]
        hcar[...] = h0_ref[0]

    x = x_ref[0]
    xc = _causal_conv_tile(buf, x, cw_ref, ts) + cb_ref[...]
    r = jax.nn.sigmoid(_dot(xc, wa_ref[...], HI) + ba_ref[...])
    i = jax.nn.sigmoid(_dot(xc, wx_ref[...], HI) + bx_ref[...])
    log_a = -C_POWER * r * jax.nn.softplus(-lam_ref[...])
    a = jnp.exp(log_a)
    th = jnp.tanh(log_a)
    b = jnp.sqrt(-2.0 * th / (1.0 - th)) * i * xc
    row = lax.broadcasted_iota(jnp.int32, a.shape, 0)
    d = 1
    while d < ts:
        keep = row >= d
        a_sh = jnp.where(keep, pltpu.roll(a, d, 0), 1.0)
        b_sh = jnp.where(keep, pltpu.roll(b, d, 0), 0.0)
        b = a * b_sh + b
        a = a * a_sh
        d *= 2
    h = a * hcar[...] + b
    y_ref[0] = h * jax.nn.gelu(gate_ref[0])
    hcar[...] = h[ts - 1:ts, :]

    @pl.when(n == pl.num_programs(1) - 1)
    def _():
        hn_ref[0] = h[ts - 1:ts, :]
        convn_ref[0] = x[ts - (CONV_W - 1):ts, :]


def _block_diag(w):
    n, d, _ = w.shape
    eye = jnp.eye(n, dtype=w.dtype)
    return (eye[:, None, :, None] * w[:, :, None, :]).reshape(n * d, n * d)


def _rg_lru(x, gate, h0, conv0, p, ts):
    b, s, w = x.shape
    row = lambda a: a.reshape(1, w)
    tile = pl.BlockSpec((1, ts, w), lambda i, n: (i, n, 0))
    const = lambda shape: pl.BlockSpec(shape, lambda i, n: (0,) * len(shape))
    return pl.pallas_call(
        functools.partial(_rg_lru_body, ts=ts),
        grid=(b, s // ts),
        in_specs=[tile, tile,
                  pl.BlockSpec((1, 1, w), lambda i, n: (i, 0, 0)),
                  pl.BlockSpec((1, CONV_W - 1, w), lambda i, n: (i, 0, 0)),
                  const((CONV_W, w)), const((1, w)), const((w, w)), const((1, w)), const((w, w)), const((1, w)),
                  const((1, w))],
        out_specs=[tile,
                   pl.BlockSpec((1, 1, w), lambda i, n: (i, 0, 0)),
                   pl.BlockSpec((1, CONV_W - 1, w), lambda i, n: (i, 0, 0))],
        out_shape=[jax.ShapeDtypeStruct((b, s, w), F32), jax.ShapeDtypeStruct((b, 1, w), F32),
                   jax.ShapeDtypeStruct((b, CONV_W - 1, w), F32)],
        scratch_shapes=[pltpu.VMEM((_CONV_PAD + ts, w), F32), pltpu.VMEM((1, w), F32)],
        compiler_params=_cparams(("parallel", "arbitrary")),
        name="rg_lru",
    )(x, gate, h0.reshape(b, 1, w), conv0, p['c_conv_w'], row(p['c_conv_b']),
      _block_diag(p['c_gate_a_w']), row(p['c_gate_a_b']), _block_diag(p['c_gate_x_w']), row(p['c_gate_x_b']),
      row(p['c_lambda']))


def _gdn_body(qkv_ref, z_ref, small_ref, s0_ref, conv0_ref, cw_ref, alog_ref, dtb_ref, ng_ref,
              o_ref, sn_ref, convn_ref, buf, st, *, c):
    n = pl.program_id(1)
    w = D_WIDTH
    hd = D_HEAD_DIM

    @pl.when(n == 0)
    def _():
        buf[_CONV_PAD - (CONV_W - 1):_CONV_PAD, :] = conv0_ref[0]
        st[...] = s0_ref[0]

    x = qkv_ref[0]
    y = jax.nn.silu(_causal_conv_tile(buf, x, cw_ref, c))
    q, k, v = y[:, 0:w], y[:, w:2 * w], y[:, 2 * w:3 * w]

    same_head = (lax.broadcasted_iota(jnp.int32, (w, w), 0) // hd) == (lax.broadcasted_iota(jnp.int32, (w, w), 1) // hd)
    head_sum = same_head.astype(F32)
    q = q * lax.rsqrt(_dot(q * q, head_sum, HI) + EPS) * (hd ** -0.5)
    k = k * lax.rsqrt(_dot(k * k, head_sum, HI) + EPS)

    small = small_ref[0]
    e_row = lax.broadcasted_iota(jnp.int32, (LANES, w), 0)
    e_head = lax.broadcasted_iota(jnp.int32, (LANES, w), 1) // hd
    beta = jax.nn.sigmoid(_dot(small, (e_row == D_HEADS + e_head).astype(F32), HI))
    alpha = _dot(small, (e_row == 2 * D_HEADS + e_head).astype(F32), HI)
    g = -jnp.exp(alog_ref[...]) * jax.nn.softplus(alpha + dtb_ref[...])

    ti = lax.broadcasted_iota(jnp.int32, (c, c), 0)
    tj = lax.broadcasted_iota(jnp.int32, (c, c), 1)
    incl = ti >= tj
    strict = ti > tj
    eye = (ti == tj).astype(F32)
    gcum = _dot(incl.astype(F32), g, HI)
    eg = jnp.exp(gcum)
    vb = v * beta
    kb = k * beta
    kbe = kb * eg
    q_g = q * eg
    g_last = gcum[c - 1:c, :]
    k_d = k * jnp.exp(g_last - gcum)

    lane_head = lax.broadcasted_iota(jnp.int32, (1, w), 1) // hd
    lane = lax.broadcasted_iota(jnp.int32, (c, w), 1)
    v_w = jnp.zeros((c, w), F32)
    k_w = jnp.zeros((c, w), F32)
    attn = []
    heads = []
    for h in range(D_HEADS):
        mh = lane_head == h
        heads.append(mh)
        pick = (lane == h * hd).astype(F32)
        gi = _dot_nt(gcum, pick, HI)
        gj = _dot_nt(pick, gcum, HI)
        decay = jnp.where(incl, jnp.exp(gi - gj), 0.0)
        low = jnp.where(strict, _dot_nt(jnp.where(mh, kb, 0.0), k, HI) * decay, 0.0)
        inv = eye - low
        pw = low
        span = 2
        while span < c:
            pw = _dot(pw, pw, HI)
            inv = inv + _dot(inv, pw, HI)
            span *= 2
        v_w = v_w + jnp.where(mh, _dot(inv, vb, HI), 0.0)
        k_w = k_w + jnp.where(mh, _dot(inv, kbe, HI), 0.0)
        attn.append(_dot_nt(jnp.where(mh, q, 0.0), k, HI) * decay)

    state = st[...]
    v_new = v_w - _dot(k_w, state, HI)
    o = _dot(q_g, state, HI)
    for h in range(D_HEADS):
        o = o + jnp.where(heads[h], _dot(attn[h], v_new, HI), 0.0)
    state = state * jnp.exp(g_last) + jnp.where(same_head, _dot_tn(k_d, v_new, HI), 0.0)
    st[...] = state

    o = o * lax.rsqrt(_dot(o * o, head_sum, HI) * (1.0 / hd) + EPS) * ng_ref[...]
    o_ref[0] = o * jax.nn.silu(z_ref[0])

    @pl.when(n == pl.num_programs(1) - 1)
    def _():
        sn_ref[0] = state
        convn_ref[0] = x[c - (CONV_W - 1):c, :]


def _gdn(qkv, z, small, s0, conv0, p, c):
    b, s, w3 = qkv.shape
    w = D_WIDTH
    s0_bd = jax.vmap(_block_diag)(s0)
    per_lane = lambda a: jnp.repeat(a, D_HEAD_DIM).reshape(1, w)
    tile = lambda n_: pl.BlockSpec((1, c, n_), lambda i, n: (i, n, 0))
    const = lambda shape: pl.BlockSpec(shape, lambda i, n: (0,) * len(shape))
    per_b = lambda shape: pl.BlockSpec((1,) + shape, lambda i, n: (i,) + (0,) * len(shape))
    o, sn, convn = pl.pallas_call(
        functools.partial(_gdn_body, c=c),
        grid=(b, s // c),
        in_specs=[tile(w3), tile(w), tile(LANES), per_b((w, w)), per_b((CONV_W - 1, w3)),
                  const((CONV_W, w3)), const((1, w)), const((1, w)), const((1, w))],
        out_specs=[tile(w), per_b((w, w)), per_b((CONV_W - 1, w3))],
        out_shape=[jax.ShapeDtypeStruct((b, s, w), F32), jax.ShapeDtypeStruct((b, w, w), F32),
                   jax.ShapeDtypeStruct((b, CONV_W - 1, w3), F32)],
        scratch_shapes=[pltpu.VMEM((_CONV_PAD + c, w3), F32), pltpu.VMEM((w, w), F32)],
        compiler_params=_cparams(("parallel", "arbitrary")),
        name="gdn",
    )(qkv, z, small, s0_bd, conv0, p['d_conv_w'], per_lane(p['d_a_log']), per_lane(p['d_dt_bias']),
      jnp.tile(p['d_norm_g'], D_HEADS).reshape(1, w))
    hd = D_HEAD_DIM
    sn = jnp.stack([sn[:, h * hd:(h + 1) * hd, h * hd:(h + 1) * hd] for h in range(D_HEADS)], axis=1)
    return o, sn, convn


_PEER_SLABS = 2 * PEER_HEADS


def _out_proj_body(x_ref, oa_ref, ob_ref, oc_ref, od_ref, wo_ref, g_ref, wq_ref, sk_ref,
                   x2_ref, h2_ref, st_ref):
    x2 = x_ref[...]
    for grp, o_ref in enumerate((oa_ref, ob_ref, oc_ref, od_ref)):
        x2 = x2 + _dot(o_ref[...].astype(BF16), wo_ref[grp * GROUP_WIDTH:(grp + 1) * GROUP_WIDTH, :])
    x2_ref[...] = x2
    hb = (x2 * lax.rsqrt(jnp.mean(x2 * x2, axis=-1, keepdims=True) + EPS) * g_ref[...]).astype(BF16)
    h2_ref[...] = hb
    q = _dot(hb, wq_ref[...])
    for slab in range(_PEER_SLABS):
        st_ref[slab] = _dot_nt(sk_ref[slab % 2], q[:, slab * PEER_HALF:(slab + 1) * PEER_HALF], HI)


def _out_proj(x, o_a, o_b, o_c, o_d, w_out, norm2_g, wq, subkeys, tm):
    t = x.shape[0]
    tile = lambda n: pl.BlockSpec((tm, n), lambda i: (i, 0))
    const = lambda shape: pl.BlockSpec(shape, lambda i: (0,) * len(shape))
    nq = PEER_HEADS * PEER_QUERY_DIM
    return pl.pallas_call(
        _out_proj_body,
        grid=(t // tm,),
        in_specs=[tile(D_MODEL), tile(GROUP_WIDTH), tile(GROUP_WIDTH), tile(GROUP_WIDTH), tile(GROUP_WIDTH),
                  const((D_MODEL, D_MODEL)), const((1, D_MODEL)), const((D_MODEL, nq)),
                  const((2, PEER_N_KEYS, PEER_HALF))],
        out_specs=[tile(D_MODEL), tile(D_MODEL),
                   pl.BlockSpec((_PEER_SLABS, PEER_N_KEYS, tm), lambda i: (0, 0, i))],
        out_shape=[jax.ShapeDtypeStruct((t, D_MODEL), F32), jax.ShapeDtypeStruct((t, D_MODEL), BF16),
                   jax.ShapeDtypeStruct((_PEER_SLABS, PEER_N_KEYS, t), F32)],
        compiler_params=_cparams(("parallel",)),
        name="out_proj",
    )(x, o_a, o_b, o_c, o_d, w_out, norm2_g.reshape(1, D_MODEL), wq, subkeys)


def _top_values(x, count):
    vals = []
    for _ in range(count):
        m = jnp.max(x, axis=0, keepdims=True)
        vals.append(m)
        x = jnp.where(x == m, -jnp.inf, x)
    return vals, x


def _peer_select_body(s_ref, c_ref, th_ref, s2m_ref, e2_ref):
    s1 = s_ref[0]
    s2 = s_ref[1]
    v1, rest1 = _top_values(s1, PEER_TOPK)
    v2, rest2 = _top_values(s2, PEER_TOPK)
    sel1 = rest1 == -jnp.inf
    sel2 = rest2 == -jnp.inf
    v2_all = jnp.concatenate(v2, axis=0)
    cand = jnp.concatenate([v1[i] + v2_all for i in range(PEER_TOPK)], axis=0)
    top, _ = _top_values(cand, PEER_TOPK)
    tau = top[PEER_TOPK - 1]
    z = jnp.ones_like(tau)
    for t in top[1:]:
        z = z + jnp.exp(t - top[0])
    s1m = jnp.where(sel1, s1, -jnp.inf)
    theta = jnp.full(s1.shape, jnp.inf, F32)
    for j in range(PEER_TOPK):
        theta = jnp.minimum(theta, jnp.where(s1m + v2[j] >= tau, v2[j], jnp.inf))
    th_ref[0] = theta
    c_ref[0] = jnp.where(sel1, jnp.exp(s1 - v1[0]), 0.0) / z
    s2m_ref[0] = jnp.where(sel2, s2, -jnp.inf)
    e2_ref[0] = jnp.where(sel2, jnp.exp(s2 - v2[0]), 0.0)


def _peer_select(st, tm):
    t = st.shape[2]
    out_spec = pl.BlockSpec((1, PEER_N_KEYS, tm), lambda i, h: (h, 0, i))
    shape = jax.ShapeDtypeStruct((PEER_HEADS, PEER_N_KEYS, t), F32)
    return pl.pallas_call(
        _peer_select_body,
        grid=(t // tm, PEER_HEADS),
        in_specs=[pl.BlockSpec((2, PEER_N_KEYS, tm), lambda i, h: (h, 0, i))],
        out_specs=[out_spec] * 4,
        out_shape=[shape] * 4,
        compiler_params=_cparams(("parallel", "parallel")),
        name="peer_select",
    )(st)


def _peer_dense_body(ht_ref, u_ref, vt_ref, c_ref, th_ref, s2m_ref, e2_ref, o_ref, act, pbuf, *, na):
    e = pl.program_id(1)
    act[...] = _dot(u_ref[...], ht_ref[...])

    def per_key(a, carry):
        rows = pl.ds(pl.multiple_of(a * PEER_N_KEYS, PEER_N_KEYS), PEER_N_KEYS)
        w = None
        for h in range(PEER_HEADS):
            theta = th_ref[h, pl.ds(a, 1), :]
            term = jnp.where(s2m_ref[h] >= theta, e2_ref[h], 0.0) * c_ref[h, pl.ds(a, 1), :]
            w = term if w is None else w + term
        pbuf[rows, :] = (w * jax.nn.gelu(act[rows, :])).astype(BF16)
        return carry

    lax.fori_loop(0, na, per_key, 0)
    contrib = _dot(vt_ref[...], pbuf[...])

    @pl.when(e == 0)
    def _():
        o_ref[...] = contrib

    @pl.when(e > 0)
    def _():
        o_ref[...] += contrib


def _peer_dense(ht, u, vt, c, theta, s2m, e2, tm, eb):
    t = ht.shape[1]
    na = eb // PEER_N_KEYS
    key_rows = pl.BlockSpec((PEER_HEADS, na, tm), lambda i, e: (0, e, i))
    all_keys = pl.BlockSpec((PEER_HEADS, PEER_N_KEYS, tm), lambda i, e: (0, 0, i))
    return pl.pallas_call(
        functools.partial(_peer_dense_body, na=na),
        grid=(t // tm, PEER_EXPERTS // eb),
        in_specs=[pl.BlockSpec((D_MODEL, tm), lambda i, e: (0, i)),
                  pl.BlockSpec((eb, D_MODEL), lambda i, e: (e, 0)),
                  pl.BlockSpec((D_MODEL, eb), lambda i, e: (0, e)),
                  key_rows, key_rows, all_keys, all_keys],
        out_specs=pl.BlockSpec((D_MODEL, tm), lambda i, e: (0, i)),
        out_shape=jax.ShapeDtypeStruct((D_MODEL, t), F32),
        scratch_shapes=[pltpu.VMEM((eb, tm), F32), pltpu.VMEM((eb, tm), BF16)],
        compiler_params=_cparams(("parallel", "arbitrary")),
        name="peer_dense",
    )(ht, u, vt, c, theta, s2m, e2)


def _residual_body(x_ref, pt_ref, g_ref, o_ref, *, final_norm):
    x = x_ref[...] + pt_ref[...].T
    if final_norm:
        x = x * lax.rsqrt(jnp.mean(x * x, axis=-1, keepdims=True) + EPS) * g_ref[...]
    o_ref[...] = x


def _residual(x, pt, g, tm, final_norm):
    t = x.shape[0]
    return pl.pallas_call(
        functools.partial(_residual_body, final_norm=final_norm),
        grid=(t // tm,),
        in_specs=[pl.BlockSpec((tm, D_MODEL), lambda i: (i, 0)),
                  pl.BlockSpec((D_MODEL, tm), lambda i: (0, i)),
                  pl.BlockSpec((1, D_MODEL), lambda i: (0, 0))],
        out_specs=pl.BlockSpec((tm, D_MODEL), lambda i: (i, 0)),
        out_shape=jax.ShapeDtypeStruct((t, D_MODEL), F32),
        compiler_params=_cparams(("parallel",)),
        name="residual",
    )(x, pt, g.reshape(1, D_MODEL))


def _heads_first(a):
    return jnp.transpose(a, (0, 2, 1, 3))


def _prep_layer_weights(w):
    return {
        'w_in': _permute_w_in(w['w_in']),
        'w_out': w['w_out'].astype(BF16),
        'wq': w['peer_wq'].astype(BF16),
        'u': w['peer_u'].astype(BF16),
        'vt': w['peer_v'].astype(BF16).T,
    }


def _mixers(x2d, bsz, s, past, p, pw, lam_init, cfg):
    pa_k, pa_v, pb_k, pb_v, pb_logf, c_h0, c_conv0, d_s0, d_conv0 = past
    plen = pa_k.shape[1]
    sk = plen + s
    a_q, a_k, a_v, b_q, b_k, b_v, c_x, c_g, d_qkv, d_z, small = _in_proj(x2d, p['norm1_g'], pw['w_in'], cfg['tm_in'])
    bq, bk = cfg['bq'], cfg['bk']

    def seq(a, *tail):
        return a.reshape(bsz, s, *tail)

    q_maps = seq(a_q, A_HEADS, 2, A_QK_DIM)
    k_all = jnp.concatenate([pa_k, seq(a_k, A_HEADS, 2 * A_QK_DIM)], axis=1).reshape(bsz, sk, A_HEADS, 2, A_QK_DIM)
    v_all = jnp.concatenate([pa_v, seq(a_v, A_HEADS, A_V_DIM)], axis=1)
    o_a = _diff_attn(_heads_first(q_maps[..., 0, :]), _heads_first(q_maps[..., 1, :]),
                     _heads_first(k_all[..., 0, :]), _heads_first(k_all[..., 1, :]), _heads_first(v_all),
                     p['rel_bias'], p['a_lambda'], p['a_norm_g'], plen, bq, bk, lam_init)
    o_a = _heads_first(o_a).reshape(bsz * s, GROUP_WIDTH)

    rows = -(-sk // LANES)
    rows = -(-rows // 8) * 8
    f_vals = jnp.concatenate([pb_logf, seq(small[:, 0:B_HEADS], B_HEADS)], axis=1)
    f_vals = jnp.pad(f_vals, ((0, 0), (0, rows * LANES - sk), (0, 0)))
    f_vals = jnp.transpose(f_vals, (0, 2, 1)).reshape(bsz, B_HEADS, rows, LANES)
    logf, cum = _logf_cumsum(f_vals, p['b_forget_bias'], plen)
    cum = cum.reshape(bsz, B_HEADS, rows * LANES)[:, :, :sk]
    b_logf = jnp.transpose(logf.reshape(bsz, B_HEADS, rows * LANES)[:, :, plen:sk], (0, 2, 1))
    kb_all = jnp.concatenate([pb_k, seq(b_k, B_HEADS, B_HEAD_DIM)], axis=1)
    vb_all = jnp.concatenate([pb_v, seq(b_v, B_HEADS, B_HEAD_DIM)], axis=1)
    o_b = _forget_attn(_heads_first(seq(b_q, B_HEADS, B_HEAD_DIM)), _heads_first(kb_all), _heads_first(vb_all),
                       cum, plen, bq, bk)
    o_b = _heads_first(o_b).reshape(bsz * s, GROUP_WIDTH)

    o_c, c_h, c_conv = _rg_lru(seq(c_x, C_WIDTH), seq(c_g, C_WIDTH), c_h0, c_conv0, p, cfg['ts'])

    o_d, d_s, d_conv = _gdn(seq(d_qkv, 3 * D_WIDTH), seq(d_z, D_WIDTH), seq(small, LANES), d_s0, d_conv0, p,
                            min(CHUNK, s))

    state = (seq(a_k, A_HEADS, 2 * A_QK_DIM), seq(a_v, A_HEADS, A_V_DIM), seq(b_k, B_HEADS, B_HEAD_DIM),
             seq(b_v, B_HEADS, B_HEAD_DIM), b_logf, c_h.reshape(bsz, C_WIDTH), c_conv, d_s, d_conv)
    return (o_a, o_b, o_c.reshape(bsz * s, C_WIDTH), o_d.reshape(bsz * s, D_WIDTH)), state


def _layer(x2d, bsz, s, past, p, pw, lam_init, cfg):
    mix, state = _mixers(x2d, bsz, s, past, p, pw, lam_init, cfg)
    x2, h2, st = _out_proj(x2d, *mix, pw['w_out'], p['norm2_g'], pw['wq'], p['peer_subkeys'], cfg['tm_out'])
    c, theta, s2m, e2 = _peer_select(st, cfg['tm_sel'])
    peer_t = _peer_dense(h2.T, pw['u'], pw['vt'], c, theta, s2m, e2, cfg['tm_peer'], cfg['eb'])
    return x2, peer_t, state


_PROMPT_CFG = dict(tm_in=512, bq=512, bk=512, ts=256, tm_out=256, tm_sel=512, tm_peer=256, eb=2048, tm_res=256)


def _sample_cfg(s, sk):
    return dict(tm_in=512, bq=s, bk=sk, ts=s, tm_out=256, tm_sel=512, tm_peer=256, eb=2048, tm_res=256)


def kernel(x_prompt, x_sample, cache_a_k, cache_a_v, cache_b_k, cache_b_v, cache_b_logf, state_c_h, state_c_conv, state_d_s, state_d_conv, norm1_g, norm2_g, final_norm_g, w_in, w_out, rel_bias, a_lambda, a_norm_g, b_forget_bias, c_conv_w, c_conv_b, c_gate_a_w, c_gate_a_b, c_gate_x_w, c_gate_x_b, c_lambda, d_conv_w, d_a_log, d_dt_bias, d_norm_g, peer_wq, peer_subkeys, peer_u, peer_v):
    pb, ps, _ = x_prompt.shape
    sb, ss, _ = x_sample.shape
    dt = x_prompt.dtype
    xp = x_prompt.reshape(pb * ps, D_MODEL)
    xs = x_sample.reshape(sb * ss, D_MODEL)
    scfg = _sample_cfg(ss, cache_a_k.shape[2] + ss)
    prompt_out, sample_out = [], []
    for l in range(DEPTH):
        p = {
            'norm1_g': norm1_g[l], 'norm2_g': norm2_g[l], 'rel_bias': rel_bias, 'a_lambda': a_lambda[l],
            'a_norm_g': a_norm_g[l], 'b_forget_bias': b_forget_bias[l],
            'c_conv_w': c_conv_w[l], 'c_conv_b': c_conv_b[l],
            'c_gate_a_w': c_gate_a_w[l], 'c_gate_a_b': c_gate_a_b[l],
            'c_gate_x_w': c_gate_x_w[l], 'c_gate_x_b': c_gate_x_b[l], 'c_lambda': c_lambda[l],
            'd_conv_w': d_conv_w[l], 'd_a_log': d_a_log[l], 'd_dt_bias': d_dt_bias[l], 'd_norm_g': d_norm_g[l],
            'peer_subkeys': peer_subkeys[l],
        }
        pw = _prep_layer_weights({'w_in': w_in[l], 'w_out': w_out[l], 'peer_wq': peer_wq[l],
                                  'peer_u': peer_u[l], 'peer_v': peer_v[l]})
        lam_init = 0.8 - 0.6 * math.exp(-0.3 * l)
        empty = (
            jnp.zeros((pb, 0, A_HEADS, 2 * A_QK_DIM), dt), jnp.zeros((pb, 0, A_HEADS, A_V_DIM), dt),
            jnp.zeros((pb, 0, B_HEADS, B_HEAD_DIM), dt), jnp.zeros((pb, 0, B_HEADS, B_HEAD_DIM), dt),
            jnp.zeros((pb, 0, B_HEADS), dt),
            jnp.zeros((pb, C_WIDTH), dt), jnp.zeros((pb, CONV_W - 1, C_WIDTH), dt),
            jnp.zeros((pb, D_HEADS, D_HEAD_DIM, D_HEAD_DIM), dt), jnp.zeros((pb, CONV_W - 1, 3 * D_WIDTH), dt),
        )
        last = l == DEPTH - 1
        xp2, peer_p, st_p = _layer(xp, pb, ps, empty, p, pw, lam_init, _PROMPT_CFG)
        xp = _residual(xp2, peer_p, final_norm_g, _PROMPT_CFG['tm_res'], last)
        prompt_out.append(st_p)
        past = (cache_a_k[l], cache_a_v[l], cache_b_k[l], cache_b_v[l], cache_b_logf[l],
                state_c_h[l], state_c_conv[l], state_d_s[l], state_d_conv[l])
        xs2, peer_s, st_s = _layer(xs, sb, ss, past, p, pw, lam_init, scfg)
        xs = _residual(xs2, peer_s, final_norm_g, scfg['tm_res'], last)
        sample_out.append(st_s)

    y_prompt = xp.reshape(pb, ps, D_MODEL)
    y_sample = xs.reshape(sb, ss, D_MODEL)
    p_states = [jnp.stack(z, axis=0) for z in zip(*prompt_out)]
    s_states = [jnp.stack(z, axis=0) for z in zip(*sample_out)]
    return (y_prompt, y_sample, *p_states, *s_states)
```

```python
import functools
import math

import numpy as np
import jax
import jax.numpy as jnp
from jax import lax
from jax.experimental import pallas as pl
from jax.experimental.pallas import tpu as pltpu

F32 = jnp.float32
BF16 = jnp.bfloat16
HI = lax.Precision.HIGHEST

D_MODEL = 1024
DEPTH = 2
CHUNK = 64
EPS = 1e-6
CONV_W = 4
GROUP_WIDTH = D_MODEL // 4
A_HEADS = 4
A_QK_DIM = GROUP_WIDTH // (2 * A_HEADS)
A_V_DIM = GROUP_WIDTH // A_HEADS
NUM_BUCKETS = 32
REL_MAX_DIST = 256
B_HEADS = 4
B_HEAD_DIM = GROUP_WIDTH // B_HEADS
C_WIDTH = GROUP_WIDTH
C_BLOCKS = 4
C_BLOCK_DIM = C_WIDTH // C_BLOCKS
C_POWER = 8.0
D_HEADS = 4
D_HEAD_DIM = GROUP_WIDTH // D_HEADS
D_WIDTH = GROUP_WIDTH
PEER_HEADS = 8
PEER_N_KEYS = 128
PEER_EXPERTS = PEER_N_KEYS * PEER_N_KEYS
PEER_TOPK = 16
PEER_QUERY_DIM = 256
PEER_HALF = PEER_QUERY_DIM // 2

LANES = 128
NEG_BIG = -1e30
VMEM_LIMIT = 56 * 1024 * 1024


def _cparams(sem):
    return pltpu.CompilerParams(dimension_semantics=sem, vmem_limit_bytes=VMEM_LIMIT)


def _dot(a, b, precision=None):
    return jnp.dot(a, b, preferred_element_type=F32, precision=precision)


def _dot_nt(a, b, precision=None):
    return lax.dot_general(a, b, (((1,), (1,)), ((), ())), preferred_element_type=F32, precision=precision)


def _dot_tn(a, b, precision=None):
    return lax.dot_general(a, b, (((0,), (0,)), ((), ())), preferred_element_type=F32, precision=precision)


_IN_WIDTHS = (256,) * 8 + (768, 256, LANES)
_IN_TOTAL = sum(_IN_WIDTHS)


def _permute_w_in(w):
    pad = jnp.zeros((w.shape[0], LANES - 12), w.dtype)
    return jnp.concatenate([w[:, 0:1536], w[:, 1540:3076], w[:, 1536:1540], w[:, 3076:3084], pad], axis=1).astype(BF16)


def _in_proj_body(x_ref, g_ref, w_ref, *outs):
    x = x_ref[...]
    h = x * lax.rsqrt(jnp.mean(x * x, axis=-1, keepdims=True) + EPS) * g_ref[...]
    hb = h.astype(BF16)
    off = 0
    for o_ref, n in zip(outs, _IN_WIDTHS):
        o_ref[...] = _dot(hb, w_ref[:, off:off + n])
        off += n


def _in_proj(x, g, w_perm, tm):
    t = x.shape[0]
    return pl.pallas_call(
        _in_proj_body,
        grid=(t // tm,),
        in_specs=[pl.BlockSpec((tm, D_MODEL), lambda i: (i, 0)),
                  pl.BlockSpec((1, D_MODEL), lambda i: (0, 0)),
                  pl.BlockSpec((D_MODEL, _IN_TOTAL), lambda i: (0, 0))],
        out_specs=[pl.BlockSpec((tm, n), lambda i: (i, 0)) for n in _IN_WIDTHS],
        out_shape=[jax.ShapeDtypeStruct((t, n), F32) for n in _IN_WIDTHS],
        compiler_params=_cparams(("parallel",)),
        name="in_proj",
    )(x, g.reshape(1, D_MODEL), w_perm)


def _logf_cumsum_body(v_ref, b_ref, logf_ref, cum_ref, *, plen, rows):
    vals = v_ref[0, 0]
    pos = lax.broadcasted_iota(jnp.int32, (rows, LANES), 0) * LANES + lax.broadcasted_iota(jnp.int32, (rows, LANES), 1)
    logf = jnp.where(pos >= plen, jax.nn.log_sigmoid(vals + b_ref[0]), vals)
    logf_ref[0, 0] = logf
    kk = lax.broadcasted_iota(jnp.int32, (LANES, LANES), 0)
    jj = lax.broadcasted_iota(jnp.int32, (LANES, LANES), 1)
    in_row = _dot(logf, (kk <= jj).astype(F32), HI)
    tot = jnp.broadcast_to(in_row[:, LANES - 1:LANES], (rows, LANES))
    ri = lax.broadcasted_iota(jnp.int32, (rows, rows), 0)
    rj = lax.broadcasted_iota(jnp.int32, (rows, rows), 1)
    cum_ref[0, 0] = in_row + _dot((rj < ri).astype(F32), tot, HI)


def _logf_cumsum(vals, bias, plen):
    b, h, rows, _ = vals.shape
    spec = pl.BlockSpec((1, 1, rows, LANES), lambda i, j: (i, j, 0, 0))
    return pl.pallas_call(
        functools.partial(_logf_cumsum_body, plen=plen, rows=rows),
        grid=(b, h),
        in_specs=[spec, pl.BlockSpec((1, 1, 1), lambda i, j: (j, 0, 0))],
        out_specs=[spec, spec],
        out_shape=[jax.ShapeDtypeStruct(vals.shape, F32)] * 2,
        compiler_params=_cparams(("parallel", "parallel")),
        name="logf_cumsum",
    )(vals, bias.reshape(h, 1, 1))


LOG2E = math.log2(math.e)
FAR = 'far'


def _sweep_plan(kinds_per_q):
    n_pairs, sig_id, sigs = [], [], []
    for kinds in kinds_per_q:
        n_far = 0
        while n_far < len(kinds) and kinds[n_far] == FAR:
            n_far += 1
        pairs = min(n_far, len(kinds) - 1) // 2
        tail = tuple(kinds[2 * pairs:])
        if tail not in sigs:
            sigs.append(tail)
        n_pairs.append(pairs)
        sig_id.append(sigs.index(tail))
    return np.asarray(n_pairs, np.int32), np.asarray(sig_id, np.int32), sigs


def _kv_sweep(n_pairs, sig, sigs, qk, consume, bufs_a, bufs_b):
    def put(bufs, tiles):
        for buf, tile in zip(bufs, tiles):
            buf[...] = tile

    def get(bufs):
        return tuple(buf[...] for buf in bufs)

    put(bufs_a, qk(0))

    def pair(t, carry):
        j = 2 * t
        put(bufs_b, qk(j + 1))
        consume(get(bufs_a), j, FAR)
        put(bufs_a, qk(j + 2))
        consume(get(bufs_b), j + 1, FAR)
        return carry

    lax.fori_loop(0, n_pairs, pair, 0)
    base = 2 * n_pairs
    for sid, tail in enumerate(sigs):
        @pl.when(sig == sid)
        def _():
            cur, nxt = bufs_a, bufs_b
            for off, kind in enumerate(tail):
                if off + 1 < len(tail):
                    put(nxt, qk(base + off + 1))
                consume(get(cur), base + off, kind)
                cur, nxt = nxt, cur


def _softmax_block(s, v_blk, m_ref, acc_ref):
    m_old = m_ref[...]
    m_new = jnp.maximum(m_old, jnp.max(s, axis=-1, keepdims=True))
    alpha = jnp.exp2(m_old - m_new)
    bk = s.shape[1]
    m_wide = jnp.concatenate([m_new] * (bk // LANES), axis=1) if bk % LANES == 0 else m_new[:, 0:1]
    p = jnp.exp2(s - m_wide)
    acc_ref[...] = alpha * acc_ref[...] + _dot(p.astype(BF16), v_blk)
    m_ref[...] = m_new


def _with_ones(v):
    b, h, s, d = v.shape
    return jnp.concatenate([v, jnp.ones((b, h, s, 1), v.dtype), jnp.zeros((b, h, s, LANES - d - 1), v.dtype)],
                           axis=-1).astype(BF16)


def _t5_bucket(rel):
    half = NUM_BUCKETS // 2
    max_exact = half // 2
    ret = jnp.where(rel > 0, half, 0)
    n = jnp.abs(rel)
    nf = jnp.maximum(n, 1).astype(F32)
    large = max_exact + (jnp.log(nf / max_exact) / math.log(REL_MAX_DIST / max_exact)
                         * (half - max_exact)).astype(jnp.int32)
    large = jnp.minimum(large, half - 1)
    return ret + jnp.where(n < max_exact, n, large)


def _bucket_bias(rel_bias, rel):
    bucket = _t5_bucket(rel)
    table = rel_bias.astype(F32)
    out = jnp.zeros((table.shape[1],) + rel.shape, F32)
    for b in range(NUM_BUCKETS):
        out = jnp.where(bucket[None] == b, table[b].reshape((-1,) + (1,) * rel.ndim), out)
    return out


def _diff_plan(nq, nk, plen, bq, bk):
    far = bk + 2 * REL_MAX_DIST
    kinds_per_q, deltas = [], []
    for i in range(nq):
        q_hi = (plen + i * bq + bq - 1) // CHUNK
        kinds = []
        for j in range(nk):
            if (j * bk) // CHUNK > q_hi:
                break
            d = plen + i * bq - j * bk
            if d >= far:
                kinds.append(FAR)
            else:
                if d not in deltas:
                    deltas.append(d)
                kinds.append(('tile', deltas.index(d)))
        kinds_per_q.append(kinds)
    return kinds_per_q, deltas


def _diff_bias_tiles(rel_bias, deltas, bq, bk):
    r = jnp.arange(bq, dtype=jnp.int32)[:, None]
    c = jnp.arange(bk, dtype=jnp.int32)[None, :]
    far_bias = _bucket_bias(rel_bias, jnp.full((1, 1), -4 * REL_MAX_DIST, jnp.int32))
    tiles = []
    for d in deltas:
        bias = (_bucket_bias(rel_bias, c - r - d) - far_bias) * LOG2E
        vis = (c // CHUNK) <= ((r + d) // CHUNK)
        tiles.append(jnp.where(vis[None], bias, NEG_BIG))
    return jnp.stack(tiles, axis=0)


def _diff_attn_body(pairs_ref, sig_ref, q_ref, k_ref, v_ref, bias_ref, lam_ref, g_ref, o_ref,
                    sa1, sa2, sb1, sb2, m1, acc1, m2, acc2, *, bk, sigs, lam_init):
    i = pl.program_id(2)
    dv = A_V_DIM
    q = q_ref[0, 0] * (A_QK_DIM ** -0.5 * LOG2E)
    lane = lax.broadcasted_iota(jnp.int32, (1, 2 * A_QK_DIM), 1)
    q1 = jnp.where(lane < A_QK_DIM, q, 0.0).astype(BF16)
    q2 = jnp.where(lane >= A_QK_DIM, q, 0.0).astype(BF16)
    for m, acc in ((m1, acc1), (m2, acc2)):
        m[...] = jnp.full(m.shape, -jnp.inf, F32)
        acc[...] = jnp.zeros(acc.shape, F32)

    def rows(j):
        return pl.ds(j * bk if isinstance(j, int) else pl.multiple_of(j * bk, bk), bk)

    def qk(j):
        kb = k_ref[0, 0, rows(j), :]
        return _dot_nt(q1, kb), _dot_nt(q2, kb)

    def consume(tiles, j, kind):
        vb = v_ref[0, 0, rows(j), :]
        for s, m, acc in zip(tiles, (m1, m2), (acc1, acc2)):
            if kind != FAR:
                s = s + bias_ref[kind[1], 0]
            _softmax_block(s, vb, m, acc)

    _kv_sweep(pairs_ref[i], sig_ref[i], sigs, qk, consume, (sa1, sa2), (sb1, sb2))

    lp = lam_ref[...]
    lam = (jnp.exp(jnp.sum(lp[0:1] * lp[1:2], axis=-1, keepdims=True))
           - jnp.exp(jnp.sum(lp[2:3] * lp[3:4], axis=-1, keepdims=True)) + lam_init)
    a1 = acc1[...]
    a2 = acc2[...]
    o = a1[:, 0:dv] / a1[:, dv:dv + 1] - lam * (a2[:, 0:dv] / a2[:, dv:dv + 1])
    o = o * lax.rsqrt(jnp.mean(o * o, axis=-1, keepdims=True) + EPS) * g_ref[...]
    o_ref[0, 0] = o * (1.0 - lam_init)


def _diff_attn(q, k, v_aug, rel_bias, a_lambda, a_norm_g, plen, bq, bk, lam_init):
    b, h, sq, dq = q.shape
    sk = k.shape[2]
    dv = A_V_DIM
    nq, nk = sq // bq, sk // bk
    kinds, deltas = _diff_plan(nq, nk, plen, bq, bk)
    n_pairs, sig_id, sigs = _sweep_plan(kinds)
    tiles = _diff_bias_tiles(rel_bias, deltas, bq, bk)
    nt = tiles.shape[0]
    per_head = lambda shape: pl.BlockSpec((1, 1) + shape, lambda b_, h_, i, *_: (b_, h_, 0, 0))
    grid_spec = pltpu.PrefetchScalarGridSpec(
        num_scalar_prefetch=2,
        grid=(b, h, nq),
        in_specs=[pl.BlockSpec((1, 1, bq, dq), lambda b_, h_, i, *_: (b_, h_, i, 0)),
                  per_head((sk, dq)), per_head((sk, LANES)),
                  pl.BlockSpec((nt, 1, bq, bk), lambda b_, h_, i, *_: (0, h_, 0, 0)),
                  pl.BlockSpec((4, A_QK_DIM), lambda b_, h_, i, *_: (0, 0)),
                  pl.BlockSpec((1, dv), lambda b_, h_, i, *_: (0, 0))],
        out_specs=pl.BlockSpec((1, 1, bq, dv), lambda b_, h_, i, *_: (b_, h_, i, 0)),
        scratch_shapes=[pltpu.VMEM((bq, bk), F32)] * 4 + [pltpu.VMEM((bq, LANES), F32)] * 4,
    )
    return pl.pallas_call(
        functools.partial(_diff_attn_body, bk=bk, sigs=sigs, lam_init=lam_init),
        grid_spec=grid_spec,
        out_shape=jax.ShapeDtypeStruct((b, h, sq, dv), F32),
        compiler_params=_cparams(("parallel", "parallel", "arbitrary")),
        name="diff_attn",
    )(jnp.asarray(n_pairs), jnp.asarray(sig_id), q, k, v_aug, tiles, a_lambda, a_norm_g.reshape(1, dv))


MASK = 'mask'


def _forget_attn_body(pairs_ref, sig_ref, q_ref, k_ref, v_ref, ck_ref, cref_ref, o_ref, sa, sb, m, acc,
                      *, plen, bq, bk, sigs):
    i = pl.program_id(2)
    d = B_HEAD_DIM
    q = (q_ref[0, 0] * (d ** -0.5 * LOG2E)).astype(BF16)
    cref = cref_ref[0, 0, 0]
    m[...] = jnp.full(m.shape, -jnp.inf, F32)
    acc[...] = jnp.zeros(acc.shape, F32)

    def rows(j):
        return pl.ds(j * bk if isinstance(j, int) else pl.multiple_of(j * bk, bk), bk)

    def qk(j):
        e = (ck_ref[0, 0, pl.ds(j, 1), :] - cref) * LOG2E
        return (_dot_nt(q, k_ref[0, 0, rows(j), :]) - e,)

    def consume(tiles, j, kind):
        s = tiles[0]
        if kind == MASK:
            kpos = j * bk + lax.broadcasted_iota(jnp.int32, (bq, bk), 1)
            qpos = plen + i * bq + lax.broadcasted_iota(jnp.int32, (bq, bk), 0)
            s = jnp.where(kpos <= qpos, s, NEG_BIG)
        _softmax_block(s, v_ref[0, 0, rows(j), :], m, acc)

    _kv_sweep(pairs_ref[i], sig_ref[i], sigs, qk, consume, (sa,), (sb,))
    a = acc[...]
    o_ref[0, 0] = a[:, 0:d] / a[:, d:d + 1]


def _forget_attn(q, k, v_aug, cum, plen, bq, bk):
    b, h, sq, d = q.shape
    sk = k.shape[2]
    nq, nk = sq // bq, sk // bk
    kinds = []
    for i in range(nq):
        q_lo, q_hi = plen + i * bq, plen + i * bq + bq - 1
        kinds.append([FAR if j * bk + bk - 1 <= q_lo else MASK for j in range(nk) if j * bk <= q_hi])
    n_pairs, sig_id, sigs = _sweep_plan(kinds)
    cref = cum[:, :, plen:plen + sq:bq].reshape(b, h, nq, 1, 1)
    ck = cum.reshape(b, h, nk, bk)
    per_head = lambda shape: pl.BlockSpec((1, 1) + shape, lambda b_, h_, i, *_: (b_, h_, 0, 0))
    grid_spec = pltpu.PrefetchScalarGridSpec(
        num_scalar_prefetch=2,
        grid=(b, h, nq),
        in_specs=[pl.BlockSpec((1, 1, bq, d), lambda b_, h_, i, *_: (b_, h_, i, 0)),
                  per_head((sk, d)), per_head((sk, LANES)), per_head((nk, bk)),
                  pl.BlockSpec((1, 1, 1, 1, 1), lambda b_, h_, i, *_: (b_, h_, i, 0, 0))],
        out_specs=pl.BlockSpec((1, 1, bq, d), lambda b_, h_, i, *_: (b_, h_, i, 0)),
        scratch_shapes=[pltpu.VMEM((bq, bk), F32)] * 2 + [pltpu.VMEM((bq, LANES), F32)] * 2,
    )
    return pl.pallas_call(
        functools.partial(_forget_attn_body, plen=plen, bq=bq, bk=bk, sigs=sigs),
        grid_spec=grid_spec,
        out_shape=jax.ShapeDtypeStruct((b, h, sq, d), F32),
        compiler_params=_cparams(("parallel", "parallel", "arbitrary")),
        name="forget_attn",
    )(jnp.asarray(n_pairs), jnp.asarray(sig_id), q, k, v_aug, ck, cref)


_CONV_PAD = 8


def _causal_conv_tile(buf, x, cw_ref, rows):
    buf[_CONV_PAD:_CONV_PAD + rows, :] = x
    lo = _CONV_PAD - (CONV_W - 1)
    y = buf[lo:lo + rows, :] * cw_ref[0:1, :]
    for j in range(1, CONV_W):
        y = y + buf[lo + j:lo + j + rows, :] * cw_ref[j:j + 1, :]
    buf[lo:_CONV_PAD, :] = x[rows - (CONV_W - 1):rows, :]
    return y


def _rg_lru_body(x_ref, gate_ref, h0_ref, conv0_ref, cw_ref, cb_ref, wa_ref, ba_ref, wx_ref, bx_ref, lam_ref,
                 y_ref, hn_ref, convn_ref, buf, hcar, *, ts):
    n = pl.program_id(1)

    @pl.when(n == 0)
    def _():
        buf[_CONV_PAD - (CONV_W - 1):_CONV_PAD, :] = conv0_ref[0]
        hcar[...] = h0_ref[0]

    x = x_ref[0]
    xc = _causal_conv_tile(buf, x, cw_ref, ts) + cb_ref[...]
    r = jax.nn.sigmoid(_dot(xc, wa_ref[...], HI) + ba_ref[...])
    i = jax.nn.sigmoid(_dot(xc, wx_ref[...], HI) + bx_ref[...])
    log_a = -C_POWER * r * jax.nn.softplus(-lam_ref[...])
    a = jnp.exp(log_a)
    th = jnp.tanh(log_a)
    b = jnp.sqrt(-2.0 * th / (1.0 - th)) * i * xc
    row = lax.broadcasted_iota(jnp.int32, a.shape, 0)
    d = 1
    while d < ts:
        keep = row >= d
        a_sh = jnp.where(keep, pltpu.roll(a, d, 0), 1.0)
        b_sh = jnp.where(keep, pltpu.roll(b, d, 0), 0.0)
        b = a * b_sh + b
        a = a * a_sh
        d *= 2
    h = a * hcar[...] + b
    y_ref[0] = h * jax.nn.gelu(gate_ref[0])
    hcar[...] = h[ts - 1:ts, :]

    @pl.when(n == pl.num_programs(1) - 1)
    def _():
        hn_ref[0] = h[ts - 1:ts, :]
        convn_ref[0] = x[ts - (CONV_W - 1):ts, :]


def _block_diag(w):
    n, d, _ = w.shape
    eye = jnp.eye(n, dtype=w.dtype)
    return (eye[:, None, :, None] * w[:, :, None, :]).reshape(n * d, n * d)


def _rg_lru(x, gate, h0, conv0, p, ts):
    b, s, w = x.shape
    row = lambda a: a.reshape(1, w)
    tile = pl.BlockSpec((1, ts, w), lambda i, n: (i, n, 0))
    const = lambda shape: pl.BlockSpec(shape, lambda i, n: (0,) * len(shape))
    return pl.pallas_call(
        functools.partial(_rg_lru_body, ts=ts),
        grid=(b, s // ts),
        in_specs=[tile, tile,
                  pl.BlockSpec((1, 1, w), lambda i, n: (i, 0, 0)),
                  pl.BlockSpec((1, CONV_W - 1, w), lambda i, n: (i, 0, 0)),
                  const((CONV_W, w)), const((1, w)), const((w, w)), const((1, w)), const((w, w)), const((1, w)),
                  const((1, w))],
        out_specs=[tile,
                   pl.BlockSpec((1, 1, w), lambda i, n: (i, 0, 0)),
                   pl.BlockSpec((1, CONV_W - 1, w), lambda i, n: (i, 0, 0))],
        out_shape=[jax.ShapeDtypeStruct((b, s, w), F32), jax.ShapeDtypeStruct((b, 1, w), F32),
                   jax.ShapeDtypeStruct((b, CONV_W - 1, w), F32)],
        scratch_shapes=[pltpu.VMEM((_CONV_PAD + ts, w), F32), pltpu.VMEM((1, w), F32)],
        compiler_params=_cparams(("parallel", "arbitrary")),
        name="rg_lru",
    )(x, gate, h0.reshape(b, 1, w), conv0, p['c_conv_w'], row(p['c_conv_b']),
      _block_diag(p['c_gate_a_w']), row(p['c_gate_a_b']), _block_diag(p['c_gate_x_w']), row(p['c_gate_x_b']),
      row(p['c_lambda']))


def _gdn_body(qkv_ref, z_ref, small_ref, s0_ref, conv0_ref, cw_ref, alog_ref, dtb_ref, ng_ref,
              o_ref, sn_ref, convn_ref, buf, st, *, c):
    n = pl.program_id(1)
    w = D_WIDTH
    hd = D_HEAD_DIM

    @pl.when(n == 0)
    def _():
        buf[_CONV_PAD - (CONV_W - 1):_CONV_PAD, :] = conv0_ref[0]
        st[...] = s0_ref[0]

    x = qkv_ref[0]
    y = jax.nn.silu(_causal_conv_tile(buf, x, cw_ref, c))
    q, k, v = y[:, 0:w], y[:, w:2 * w], y[:, 2 * w:3 * w]

    same_head = (lax.broadcasted_iota(jnp.int32, (w, w), 0) // hd) == (lax.broadcasted_iota(jnp.int32, (w, w), 1) // hd)
    head_sum = same_head.astype(F32)
    q = q * lax.rsqrt(_dot(q * q, head_sum, HI) + EPS) * (hd ** -0.5)
    k = k * lax.rsqrt(_dot(k * k, head_sum, HI) + EPS)

    small = small_ref[0]
    e_row = lax.broadcasted_iota(jnp.int32, (LANES, w), 0)
    e_head = lax.broadcasted_iota(jnp.int32, (LANES, w), 1) // hd
    beta = jax.nn.sigmoid(_dot(small, (e_row == D_HEADS + e_head).astype(F32), HI))
    alpha = _dot(small, (e_row == 2 * D_HEADS + e_head).astype(F32), HI)
    g = -jnp.exp(alog_ref[...]) * jax.nn.softplus(alpha + dtb_ref[...])

    ti = lax.broadcasted_iota(jnp.int32, (c, c), 0)
    tj = lax.broadcasted_iota(jnp.int32, (c, c), 1)
    incl = ti >= tj
    strict = ti > tj
    eye = (ti == tj).astype(F32)
    gcum = _dot(incl.astype(F32), g, HI)
    eg = jnp.exp(gcum)
    vb = v * beta
    kb = k * beta
    kbe = kb * eg
    q_g = q * eg
    g_last = gcum[c - 1:c, :]
    k_d = k * jnp.exp(g_last - gcum)

    lane_head = lax.broadcasted_iota(jnp.int32, (1, w), 1) // hd
    lane = lax.broadcasted_iota(jnp.int32, (c, w), 1)
    v_w = jnp.zeros((c, w), F32)
    k_w = jnp.zeros((c, w), F32)
    attn = []
    heads = []
    for h in range(D_HEADS):
        mh = lane_head == h
        heads.append(mh)
        pick = (lane == h * hd).astype(F32)
        gi = _dot_nt(gcum, pick, HI)
        gj = _dot_nt(pick, gcum, HI)
        decay = jnp.where(incl, jnp.exp(gi - gj), 0.0)
        low = jnp.where(strict, _dot_nt(jnp.where(mh, kb, 0.0), k, HI) * decay, 0.0)
        inv = eye - low
        pw = low
        span = 2
        while span < c:
            pw = _dot(pw, pw, HI)
            inv = inv + _dot(inv, pw, HI)
            span *= 2
        v_w = v_w + jnp.where(mh, _dot(inv, vb, HI), 0.0)
        k_w = k_w + jnp.where(mh, _dot(inv, kbe, HI), 0.0)
        attn.append(_dot_nt(jnp.where(mh, q, 0.0), k, HI) * decay)

    state = st[...]
    v_new = v_w - _dot(k_w, state, HI)
    o = _dot(q_g, state, HI)
    for h in range(D_HEADS):
        o = o + jnp.where(heads[h], _dot(attn[h], v_new, HI), 0.0)
    state = state * jnp.exp(g_last) + jnp.where(same_head, _dot_tn(k_d, v_new, HI), 0.0)
    st[...] = state

    o = o * lax.rsqrt(_dot(o * o, head_sum, HI) * (1.0 / hd) + EPS) * ng_ref[...]
    o_ref[0] = o * jax.nn.silu(z_ref[0])

    @pl.when(n == pl.num_programs(1) - 1)
    def _():
        sn_ref[0] = state
        convn_ref[0] = x[c - (CONV_W - 1):c, :]


def _gdn(qkv, z, small, s0, conv0, p, c):
    b, s, w3 = qkv.shape
    w = D_WIDTH
    s0_bd = jax.vmap(_block_diag)(s0)
    per_lane = lambda a: jnp.repeat(a, D_HEAD_DIM).reshape(1, w)
    tile = lambda n_: pl.BlockSpec((1, c, n_), lambda i, n: (i, n, 0))
    const = lambda shape: pl.BlockSpec(shape, lambda i, n: (0,) * len(shape))
    per_b = lambda shape: pl.BlockSpec((1,) + shape, lambda i, n: (i,) + (0,) * len(shape))
    o, sn, convn = pl.pallas_call(
        functools.partial(_gdn_body, c=c),
        grid=(b, s // c),
        in_specs=[tile(w3), tile(w), tile(LANES), per_b((w, w)), per_b((CONV_W - 1, w3)),
                  const((CONV_W, w3)), const((1, w)), const((1, w)), const((1, w))],
        out_specs=[tile(w), per_b((w, w)), per_b((CONV_W - 1, w3))],
        out_shape=[jax.ShapeDtypeStruct((b, s, w), F32), jax.ShapeDtypeStruct((b, w, w), F32),
                   jax.ShapeDtypeStruct((b, CONV_W - 1, w3), F32)],
        scratch_shapes=[pltpu.VMEM((_CONV_PAD + c, w3), F32), pltpu.VMEM((w, w), F32)],
        compiler_params=_cparams(("parallel", "arbitrary")),
        name="gdn",
    )(qkv, z, small, s0_bd, conv0, p['d_conv_w'], per_lane(p['d_a_log']), per_lane(p['d_dt_bias']),
      jnp.tile(p['d_norm_g'], D_HEADS).reshape(1, w))
    hd = D_HEAD_DIM
    sn = jnp.stack([sn[:, h * hd:(h + 1) * hd, h * hd:(h + 1) * hd] for h in range(D_HEADS)], axis=1)
    return o, sn, convn


_PEER_SLABS = 2 * PEER_HEADS


def _out_proj_body(x_ref, oa_ref, ob_ref, oc_ref, od_ref, wo_ref, g_ref, wq_ref, sk_ref,
                   x2_ref, h2_ref, st_ref):
    x2 = x_ref[...]
    for grp, o_ref in enumerate((oa_ref, ob_ref, oc_ref, od_ref)):
        x2 = x2 + _dot(o_ref[...].astype(BF16), wo_ref[grp * GROUP_WIDTH:(grp + 1) * GROUP_WIDTH, :])
    x2_ref[...] = x2
    hb = (x2 * lax.rsqrt(jnp.mean(x2 * x2, axis=-1, keepdims=True) + EPS) * g_ref[...]).astype(BF16)
    h2_ref[...] = hb
    q = _dot(hb, wq_ref[...])
    for slab in range(_PEER_SLABS):
        st_ref[slab] = _dot_nt(sk_ref[slab % 2], q[:, slab * PEER_HALF:(slab + 1) * PEER_HALF], HI)


def _out_proj(x, o_a, o_b, o_c, o_d, w_out, norm2_g, wq, subkeys, tm):
    t = x.shape[0]
    tile = lambda n: pl.BlockSpec((tm, n), lambda i: (i, 0))
    const = lambda shape: pl.BlockSpec(shape, lambda i: (0,) * len(shape))
    nq = PEER_HEADS * PEER_QUERY_DIM
    return pl.pallas_call(
        _out_proj_body,
        grid=(t // tm,),
        in_specs=[tile(D_MODEL), tile(GROUP_WIDTH), tile(GROUP_WIDTH), tile(GROUP_WIDTH), tile(GROUP_WIDTH),
                  const((D_MODEL, D_MODEL)), const((1, D_MODEL)), const((D_MODEL, nq)),
                  const((2, PEER_N_KEYS, PEER_HALF))],
        out_specs=[tile(D_MODEL), tile(D_MODEL),
                   pl.BlockSpec((_PEER_SLABS, PEER_N_KEYS, tm), lambda i: (0, 0, i))],
        out_shape=[jax.ShapeDtypeStruct((t, D_MODEL), F32), jax.ShapeDtypeStruct((t, D_MODEL), BF16),
                   jax.ShapeDtypeStruct((_PEER_SLABS, PEER_N_KEYS, t), F32)],
        compiler_params=_cparams(("parallel",)),
        name="out_proj",
    )(x, o_a, o_b, o_c, o_d, w_out, norm2_g.reshape(1, D_MODEL), wq, subkeys)


def _top_values(x, count):
    vals = []
    for _ in range(count):
        m = jnp.max(x, axis=0, keepdims=True)
        vals.append(m)
        x = jnp.where(x == m, -jnp.inf, x)
    return vals, x


def _peer_select_body(s_ref, c_ref, th_ref, s2m_ref, e2_ref):
    s1 = s_ref[0]
    s2 = s_ref[1]
    v1, rest1 = _top_values(s1, PEER_TOPK)
    v2, rest2 = _top_values(s2, PEER_TOPK)
    sel1 = rest1 == -jnp.inf
    sel2 = rest2 == -jnp.inf
    v2_all = jnp.concatenate(v2, axis=0)
    cand = jnp.concatenate([v1[i] + v2_all for i in range(PEER_TOPK)], axis=0)
    top, _ = _top_values(cand, PEER_TOPK)
    tau = top[PEER_TOPK - 1]
    z = jnp.ones_like(tau)
    for t in top[1:]:
        z = z + jnp.exp(t - top[0])
    s1m = jnp.where(sel1, s1, -jnp.inf)
    theta = jnp.full(s1.shape, jnp.inf, F32)
    for j in range(PEER_TOPK):
        theta = jnp.minimum(theta, jnp.where(s1m + v2[j] >= tau, v2[j], jnp.inf))
    th_ref[0] = theta
    c_ref[0] = jnp.where(sel1, jnp.exp(s1 - v1[0]), 0.0) / z
    s2m_ref[0] = jnp.where(sel2, s2, -jnp.inf)
    e2_ref[0] = jnp.where(sel2, jnp.exp(s2 - v2[0]), 0.0)


def _peer_select(st, tm):
    t = st.shape[2]
    out_spec = pl.BlockSpec((1, PEER_N_KEYS, tm), lambda i, h: (h, 0, i))
    shape = jax.ShapeDtypeStruct((PEER_HEADS, PEER_N_KEYS, t), F32)
    return pl.pallas_call(
        _peer_select_body,
        grid=(t // tm, PEER_HEADS),
        in_specs=[pl.BlockSpec((2, PEER_N_KEYS, tm), lambda i, h: (h, 0, i))],
        out_specs=[out_spec] * 4,
        out_shape=[shape] * 4,
        compiler_params=_cparams(("parallel", "parallel")),
        name="peer_select",
    )(st)


def _peer_dense_body(ht_ref, u_ref, vt_ref, c_ref, th_ref, s2m_ref, e2_ref, o_ref, act, pbuf, *, na):
    e = pl.program_id(1)
    act[...] = _dot(u_ref[...], ht_ref[...])

    def per_key(a, carry):
        rows = pl.ds(pl.multiple_of(a * PEER_N_KEYS, PEER_N_KEYS), PEER_N_KEYS)
        w = None
        for h in range(PEER_HEADS):
            theta = th_ref[h, pl.ds(a, 1), :]
            term = jnp.where(s2m_ref[h] >= theta, e2_ref[h], 0.0) * c_ref[h, pl.ds(a, 1), :]
            w = term if w is None else w + term
        pbuf[rows, :] = (w * jax.nn.gelu(act[rows, :])).astype(BF16)
        return carry

    lax.fori_loop(0, na, per_key, 0)
    contrib = _dot(vt_ref[...], pbuf[...])

    @pl.when(e == 0)
    def _():
        o_ref[...] = contrib

    @pl.when(e > 0)
    def _():
        o_ref[...] += contrib


def _peer_dense(ht, u, vt, c, theta, s2m, e2, tm, eb):
    t = ht.shape[1]
    na = eb // PEER_N_KEYS
    key_rows = pl.BlockSpec((PEER_HEADS, na, tm), lambda i, e: (0, e, i))
    all_keys = pl.BlockSpec((PEER_HEADS, PEER_N_KEYS, tm), lambda i, e: (0, 0, i))
    return pl.pallas_call(
        functools.partial(_peer_dense_body, na=na),
        grid=(t // tm, PEER_EXPERTS // eb),
        in_specs=[pl.BlockSpec((D_MODEL, tm), lambda i, e: (0, i)),
                  pl.BlockSpec((eb, D_MODEL), lambda i, e: (e, 0)),
                  pl.BlockSpec((D_MODEL, eb), lambda i, e: (0, e)),
                  key_rows, key_rows, all_keys, all_keys],
        out_specs=pl.BlockSpec((D_MODEL, tm), lambda i, e: (0, i)),
        out_shape=jax.ShapeDtypeStruct((D_MODEL, t), F32),
        scratch_shapes=[pltpu.VMEM((eb, tm), F32), pltpu.VMEM((eb, tm), BF16)],
        compiler_params=_cparams(("parallel", "arbitrary")),
        name="peer_dense",
    )(ht, u, vt, c, theta, s2m, e2)


def _residual_body(x_ref, pt_ref, g_ref, o_ref, *, final_norm):
    x = x_ref[...] + pt_ref[...].T
    if final_norm:
        x = x * lax.rsqrt(jnp.mean(x * x, axis=-1, keepdims=True) + EPS) * g_ref[...]
    o_ref[...] = x


def _residual(x, pt, g, tm, final_norm):
    t = x.shape[0]
    return pl.pallas_call(
        functools.partial(_residual_body, final_norm=final_norm),
        grid=(t // tm,),
        in_specs=[pl.BlockSpec((tm, D_MODEL), lambda i: (i, 0)),
                  pl.BlockSpec((D_MODEL, tm), lambda i: (0, i)),
                  pl.BlockSpec((1, D_MODEL), lambda i: (0, 0))],
        out_specs=pl.BlockSpec((tm, D_MODEL), lambda i: (i, 0)),
        out_shape=jax.ShapeDtypeStruct((t, D_MODEL), F32),
        compiler_params=_cparams(("parallel",)),
        name="residual",
    )(x, pt, g.reshape(1, D_MODEL))


def _heads_first(a):
    return jnp.transpose(a, (0, 2, 1, 3))


def _prep_layer_weights(w):
    return {
        'w_in': _permute_w_in(w['w_in']),
        'w_out': w['w_out'].astype(BF16),
        'wq': w['peer_wq'].astype(BF16),
        'u': w['peer_u'].astype(BF16),
        'vt': w['peer_v'].astype(BF16).T,
    }


def _mixers(x2d, bsz, s, past, p, pw, lam_init, cfg):
    pa_k, pa_v, pb_k, pb_v, pb_logf, c_h0, c_conv0, d_s0, d_conv0 = past
    plen = pa_k.shape[1]
    sk = plen + s
    a_q, a_k, a_v, b_q, b_k, b_v, c_x, c_g, d_qkv, d_z, small = _in_proj(x2d, p['norm1_g'], pw['w_in'], cfg['tm_in'])
    bq, bk = cfg['bq'], cfg['bk']

    def seq(a, *tail):
        return a.reshape(bsz, s, *tail)

    k_all = jnp.concatenate([pa_k, seq(a_k, A_HEADS, 2 * A_QK_DIM)], axis=1)
    v_all = jnp.concatenate([pa_v, seq(a_v, A_HEADS, A_V_DIM)], axis=1)
    o_a = _diff_attn(_heads_first(seq(a_q, A_HEADS, 2 * A_QK_DIM)), _heads_first(k_all).astype(BF16),
                     _with_ones(_heads_first(v_all)),
                     p['rel_bias'], p['a_lambda'], p['a_norm_g'], plen, bq, bk, lam_init)
    o_a = _heads_first(o_a).reshape(bsz * s, GROUP_WIDTH)

    rows = -(-sk // LANES)
    rows = -(-rows // 8) * 8
    f_vals = jnp.concatenate([pb_logf, seq(small[:, 0:B_HEADS], B_HEADS)], axis=1)
    f_vals = jnp.pad(f_vals, ((0, 0), (0, rows * LANES - sk), (0, 0)))
    f_vals = jnp.transpose(f_vals, (0, 2, 1)).reshape(bsz, B_HEADS, rows, LANES)
    logf, cum = _logf_cumsum(f_vals, p['b_forget_bias'], plen)
    cum = cum.reshape(bsz, B_HEADS, rows * LANES)[:, :, :sk]
    b_logf = jnp.transpose(logf.reshape(bsz, B_HEADS, rows * LANES)[:, :, plen:sk], (0, 2, 1))
    kb_all = jnp.concatenate([pb_k, seq(b_k, B_HEADS, B_HEAD_DIM)], axis=1)
    vb_all = jnp.concatenate([pb_v, seq(b_v, B_HEADS, B_HEAD_DIM)], axis=1)
    o_b = _forget_attn(_heads_first(seq(b_q, B_HEADS, B_HEAD_DIM)), _heads_first(kb_all).astype(BF16),
                       _with_ones(_heads_first(vb_all)), cum, plen, bq, bk)
    o_b = _heads_first(o_b).reshape(bsz * s, GROUP_WIDTH)

    o_c, c_h, c_conv = _rg_lru(seq(c_x, C_WIDTH), seq(c_g, C_WIDTH), c_h0, c_conv0, p, cfg['ts'])

    o_d, d_s, d_conv = _gdn(seq(d_qkv, 3 * D_WIDTH), seq(d_z, D_WIDTH), seq(small, LANES), d_s0, d_conv0, p,
                            min(CHUNK, s))

    state = (seq(a_k, A_HEADS, 2 * A_QK_DIM), seq(a_v, A_HEADS, A_V_DIM), seq(b_k, B_HEADS, B_HEAD_DIM),
             seq(b_v, B_HEADS, B_HEAD_DIM), b_logf, c_h.reshape(bsz, C_WIDTH), c_conv, d_s, d_conv)
    return (o_a, o_b, o_c.reshape(bsz * s, C_WIDTH), o_d.reshape(bsz * s, D_WIDTH)), state


def _layer(x2d, bsz, s, past, p, pw, lam_init, cfg):
    mix, state = _mixers(x2d, bsz, s, past, p, pw, lam_init, cfg)
    x2, h2, st = _out_proj(x2d, *mix, pw['w_out'], p['norm2_g'], pw['wq'], p['peer_subkeys'], cfg['tm_out'])
    c, theta, s2m, e2 = _peer_select(st, cfg['tm_sel'])
    peer_t = _peer_dense(h2.T, pw['u'], pw['vt'], c, theta, s2m, e2, cfg['tm_peer'], cfg['eb'])
    return x2, peer_t, state


_PROMPT_CFG = dict(tm_in=512, bq=1024, bk=512, ts=256, tm_out=256, tm_sel=512, tm_peer=256, eb=2048, tm_res=256)


def _sample_cfg(s, sk):
    return dict(tm_in=512, bq=s, bk=sk, ts=s, tm_out=256, tm_sel=512, tm_peer=256, eb=2048, tm_res=256)


def kernel(x_prompt, x_sample, cache_a_k, cache_a_v, cache_b_k, cache_b_v, cache_b_logf, state_c_h, state_c_conv, state_d_s, state_d_conv, norm1_g, norm2_g, final_norm_g, w_in, w_out, rel_bias, a_lambda, a_norm_g, b_forget_bias, c_conv_w, c_conv_b, c_gate_a_w, c_gate_a_b, c_gate_x_w, c_gate_x_b, c_lambda, d_conv_w, d_a_log, d_dt_bias, d_norm_g, peer_wq, peer_subkeys, peer_u, peer_v):
    pb, ps, _ = x_prompt.shape
    sb, ss, _ = x_sample.shape
    dt = x_prompt.dtype
    xp = x_prompt.reshape(pb * ps, D_MODEL)
    xs = x_sample.reshape(sb * ss, D_MODEL)
    scfg = _sample_cfg(ss, cache_a_k.shape[2] + ss)
    prompt_out, sample_out = [], []
    for l in range(DEPTH):
        p = {
            'norm1_g': norm1_g[l], 'norm2_g': norm2_g[l], 'rel_bias': rel_bias, 'a_lambda': a_lambda[l],
            'a_norm_g': a_norm_g[l], 'b_forget_bias': b_forget_bias[l],
            'c_conv_w': c_conv_w[l], 'c_conv_b': c_conv_b[l],
            'c_gate_a_w': c_gate_a_w[l], 'c_gate_a_b': c_gate_a_b[l],
            'c_gate_x_w': c_gate_x_w[l], 'c_gate_x_b': c_gate_x_b[l], 'c_lambda': c_lambda[l],
            'd_conv_w': d_conv_w[l], 'd_a_log': d_a_log[l], 'd_dt_bias': d_dt_bias[l], 'd_norm_g': d_norm_g[l],
            'peer_subkeys': peer_subkeys[l],
        }
        pw = _prep_layer_weights({'w_in': w_in[l], 'w_out': w_out[l], 'peer_wq': peer_wq[l],
                                  'peer_u': peer_u[l], 'peer_v': peer_v[l]})
        lam_init = 0.8 - 0.6 * math.exp(-0.3 * l)
        empty = (
            jnp.zeros((pb, 0, A_HEADS, 2 * A_QK_DIM), dt), jnp.zeros((pb, 0, A_HEADS, A_V_DIM), dt),
            jnp.zeros((pb, 0, B_HEADS, B_HEAD_DIM), dt), jnp.zeros((pb, 0, B_HEADS, B_HEAD_DIM), dt),
            jnp.zeros((pb, 0, B_HEADS), dt),
            jnp.zeros((pb, C_WIDTH), dt), jnp.zeros((pb, CONV_W - 1, C_WIDTH), dt),
            jnp.zeros((pb, D_HEADS, D_HEAD_DIM, D_HEAD_DIM), dt), jnp.zeros((pb, CONV_W - 1, 3 * D_WIDTH), dt),
        )
        last = l == DEPTH - 1
        xp2, peer_p, st_p = _layer(xp, pb, ps, empty, p, pw, lam_init, _PROMPT_CFG)
        xp = _residual(xp2, peer_p, final_norm_g, _PROMPT_CFG['tm_res'], last)
        prompt_out.append(st_p)
        past = (cache_a_k[l], cache_a_v[l], cache_b_k[l], cache_b_v[l], cache_b_logf[l],
                state_c_h[l], state_c_conv[l], state_d_s[l], state_d_conv[l])
        xs2, peer_s, st_s = _layer(xs, sb, ss, past, p, pw, lam_init, scfg)
        xs = _residual(xs2, peer_s, final_norm_g, scfg['tm_res'], last)
        sample_out.append(st_s)

    y_prompt = xp.reshape(pb, ps, D_MODEL)
    y_sample = xs.reshape(sb, ss, D_MODEL)
    p_states = [jnp.stack(z, axis=0) for z in zip(*prompt_out)]
    s_states = [jnp.stack(z, axis=0) for z in zip(*sample_out)]
    return (y_prompt, y_sample, *p_states, *s_states)
```

```python
import functools
import math

import numpy as np
import jax
import jax.numpy as jnp
from jax import lax
from jax.experimental import pallas as pl
from jax.experimental.pallas import tpu as pltpu

F32 = jnp.float32
BF16 = jnp.bfloat16
HI = lax.Precision.HIGHEST

D_MODEL = 1024
DEPTH = 2
CHUNK = 64
EPS = 1e-6
CONV_W = 4
GROUP_WIDTH = D_MODEL // 4
A_HEADS = 4
A_QK_DIM = GROUP_WIDTH // (2 * A_HEADS)
A_V_DIM = GROUP_WIDTH // A_HEADS
NUM_BUCKETS = 32
REL_MAX_DIST = 256
B_HEADS = 4
B_HEAD_DIM = GROUP_WIDTH // B_HEADS
C_WIDTH = GROUP_WIDTH
C_BLOCKS = 4
C_BLOCK_DIM = C_WIDTH // C_BLOCKS
C_POWER = 8.0
D_HEADS = 4
D_HEAD_DIM = GROUP_WIDTH // D_HEADS
D_WIDTH = GROUP_WIDTH
PEER_HEADS = 8
PEER_N_KEYS = 128
PEER_EXPERTS = PEER_N_KEYS * PEER_N_KEYS
PEER_TOPK = 16
PEER_QUERY_DIM = 256
PEER_HALF = PEER_QUERY_DIM // 2

LANES = 128
NEG_BIG = -1e30
VMEM_LIMIT = 56 * 1024 * 1024


def _cparams(sem):
    return pltpu.CompilerParams(dimension_semantics=sem, vmem_limit_bytes=VMEM_LIMIT)


def _dot(a, b, precision=None):
    return jnp.dot(a, b, preferred_element_type=F32, precision=precision)


def _dot_nt(a, b, precision=None):
    return lax.dot_general(a, b, (((1,), (1,)), ((), ())), preferred_element_type=F32, precision=precision)


def _dot_tn(a, b, precision=None):
    return lax.dot_general(a, b, (((0,), (0,)), ((), ())), preferred_element_type=F32, precision=precision)


def _split_bf16(x):
    hi = x.astype(BF16)
    return hi, (x - hi.astype(F32)).astype(BF16)


def _dot3(a, b):
    ah, al = _split_bf16(a)
    bh, bl = _split_bf16(b)
    return _dot(ah, bh) + (_dot(ah, bl) + _dot(al, bh))


def _dot_exact_left(a, b):
    b1 = b.astype(BF16)
    r1 = b - b1.astype(F32)
    b2 = r1.astype(BF16)
    b3 = (r1 - b2.astype(F32)).astype(BF16)
    return _dot(a, b1) + (_dot(a, b2) + _dot(a, b3))


_IN_WIDTHS = (256,) * 8 + (768, 256, 256, 256, LANES)
_IN_TOTAL = sum(_IN_WIDTHS)


def _permute_w_in(w):
    pad = jnp.zeros((w.shape[0], LANES - B_HEADS), w.dtype)
    per_lane = lambda cols: jnp.repeat(cols, D_HEAD_DIM, axis=1)
    return jnp.concatenate([w[:, 0:1536], w[:, 1540:3076], per_lane(w[:, 3076:3080]), per_lane(w[:, 3080:3084]),
                            w[:, 1536:1540], pad], axis=1).astype(BF16)


def _in_proj_body(x_ref, g_ref, w_ref, *outs):
    x = x_ref[...]
    h = x * lax.rsqrt(jnp.mean(x * x, axis=-1, keepdims=True) + EPS) * g_ref[...]
    hb = h.astype(BF16)
    off = 0
    for o_ref, n in zip(outs, _IN_WIDTHS):
        o_ref[...] = _dot(hb, w_ref[:, off:off + n])
        off += n


def _in_proj(x, g, w_perm, tm):
    t = x.shape[0]
    return pl.pallas_call(
        _in_proj_body,
        grid=(t // tm,),
        in_specs=[pl.BlockSpec((tm, D_MODEL), lambda i: (i, 0)),
                  pl.BlockSpec((1, D_MODEL), lambda i: (0, 0)),
                  pl.BlockSpec((D_MODEL, _IN_TOTAL), lambda i: (0, 0))],
        out_specs=[pl.BlockSpec((tm, n), lambda i: (i, 0)) for n in _IN_WIDTHS],
        out_shape=[jax.ShapeDtypeStruct((t, n), F32) for n in _IN_WIDTHS],
        compiler_params=_cparams(("parallel",)),
        name="in_proj",
    )(x, g.reshape(1, D_MODEL), w_perm)


def _logf_cumsum_body(v_ref, b_ref, logf_ref, cum_ref, *, plen, rows):
    vals = v_ref[0, 0]
    pos = lax.broadcasted_iota(jnp.int32, (rows, LANES), 0) * LANES + lax.broadcasted_iota(jnp.int32, (rows, LANES), 1)
    logf = jnp.where(pos >= plen, jax.nn.log_sigmoid(vals + b_ref[0]), vals)
    logf_ref[0, 0] = logf
    kk = lax.broadcasted_iota(jnp.int32, (LANES, LANES), 0)
    jj = lax.broadcasted_iota(jnp.int32, (LANES, LANES), 1)
    in_row = _dot(logf, (kk <= jj).astype(F32), HI)
    tot = jnp.broadcast_to(in_row[:, LANES - 1:LANES], (rows, LANES))
    ri = lax.broadcasted_iota(jnp.int32, (rows, rows), 0)
    rj = lax.broadcasted_iota(jnp.int32, (rows, rows), 1)
    cum_ref[0, 0] = in_row + _dot((rj < ri).astype(F32), tot, HI)


def _logf_cumsum(vals, bias, plen):
    b, h, rows, _ = vals.shape
    spec = pl.BlockSpec((1, 1, rows, LANES), lambda i, j: (i, j, 0, 0))
    return pl.pallas_call(
        functools.partial(_logf_cumsum_body, plen=plen, rows=rows),
        grid=(b, h),
        in_specs=[spec, pl.BlockSpec((1, 1, 1), lambda i, j: (j, 0, 0))],
        out_specs=[spec, spec],
        out_shape=[jax.ShapeDtypeStruct(vals.shape, F32)] * 2,
        compiler_params=_cparams(("parallel", "parallel")),
        name="logf_cumsum",
    )(vals, bias.reshape(h, 1, 1))


LOG2E = math.log2(math.e)
FAR = 'far'


def _sweep_plan(kinds_per_q):
    n_pairs, sig_id, sigs = [], [], []
    for kinds in kinds_per_q:
        n_far = 0
        while n_far < len(kinds) and kinds[n_far] == FAR:
            n_far += 1
        pairs = min(n_far, len(kinds) - 1) // 2
        tail = tuple(kinds[2 * pairs:])
        if tail not in sigs:
            sigs.append(tail)
        n_pairs.append(pairs)
        sig_id.append(sigs.index(tail))
    return np.asarray(n_pairs, np.int32), np.asarray(sig_id, np.int32), sigs


def _kv_sweep(n_pairs, sig, sigs, qk, consume, bufs_a, bufs_b):
    def put(bufs, tiles):
        for buf, tile in zip(bufs, tiles):
            buf[...] = tile

    def get(bufs):
        return tuple(buf[...] for buf in bufs)

    put(bufs_a, qk(0))

    def pair(t, carry):
        j = 2 * t
        put(bufs_b, qk(j + 1))
        consume(get(bufs_a), j, FAR)
        put(bufs_a, qk(j + 2))
        consume(get(bufs_b), j + 1, FAR)
        return carry

    lax.fori_loop(0, n_pairs, pair, 0)
    base = 2 * n_pairs
    for sid, tail in enumerate(sigs):
        @pl.when(sig == sid)
        def _():
            cur, nxt = bufs_a, bufs_b
            for off, kind in enumerate(tail):
                if off + 1 < len(tail):
                    put(nxt, qk(base + off + 1))
                consume(get(cur), base + off, kind)
                cur, nxt = nxt, cur


def _softmax_block(s, v_blk, m_ref, acc_ref):
    m_old = m_ref[...]
    m_new = jnp.maximum(m_old, jnp.max(s, axis=-1, keepdims=True))
    alpha = jnp.exp2(m_old - m_new)
    bk = s.shape[1]
    m_wide = jnp.concatenate([m_new] * (bk // LANES), axis=1) if bk % LANES == 0 else m_new[:, 0:1]
    p = jnp.exp2(s - m_wide)
    acc_ref[...] = alpha * acc_ref[...] + _dot(p.astype(BF16), v_blk)
    m_ref[...] = m_new


def _with_ones(v):
    b, h, s, d = v.shape
    return jnp.concatenate([v, jnp.ones((b, h, s, 1), v.dtype), jnp.zeros((b, h, s, LANES - d - 1), v.dtype)],
                           axis=-1).astype(BF16)


def _t5_bucket(rel):
    half = NUM_BUCKETS // 2
    max_exact = half // 2
    ret = jnp.where(rel > 0, half, 0)
    n = jnp.abs(rel)
    nf = jnp.maximum(n, 1).astype(F32)
    large = max_exact + (jnp.log(nf / max_exact) / math.log(REL_MAX_DIST / max_exact)
                         * (half - max_exact)).astype(jnp.int32)
    large = jnp.minimum(large, half - 1)
    return ret + jnp.where(n < max_exact, n, large)


def _bucket_bias(rel_bias, rel):
    bucket = _t5_bucket(rel)
    table = rel_bias.astype(F32)
    out = jnp.zeros((table.shape[1],) + rel.shape, F32)
    for b in range(NUM_BUCKETS):
        out = jnp.where(bucket[None] == b, table[b].reshape((-1,) + (1,) * rel.ndim), out)
    return out


def _diff_plan(nq, nk, plen, bq, bk):
    far = bk + 2 * REL_MAX_DIST
    kinds_per_q, deltas = [], []
    for i in range(nq):
        q_hi = (plen + i * bq + bq - 1) // CHUNK
        kinds = []
        for j in range(nk):
            if (j * bk) // CHUNK > q_hi:
                break
            d = plen + i * bq - j * bk
            if d >= far:
                kinds.append(FAR)
            else:
                if d not in deltas:
                    deltas.append(d)
                kinds.append(('tile', deltas.index(d)))
        kinds_per_q.append(kinds)
    return kinds_per_q, deltas


def _diff_bias_tiles(rel_bias, deltas, bq, bk):
    r = jnp.arange(bq, dtype=jnp.int32)[:, None]
    c = jnp.arange(bk, dtype=jnp.int32)[None, :]
    far_bias = _bucket_bias(rel_bias, jnp.full((1, 1), -4 * REL_MAX_DIST, jnp.int32))
    tiles = []
    for d in deltas:
        bias = (_bucket_bias(rel_bias, c - r - d) - far_bias) * LOG2E
        vis = (c // CHUNK) <= ((r + d) // CHUNK)
        tiles.append(jnp.where(vis[None], bias, NEG_BIG))
    return jnp.stack(tiles, axis=0)


def _diff_attn_body(pairs_ref, sig_ref, q_ref, k_ref, v_ref, bias_ref, lam_ref, g_ref, o_ref,
                    sa1, sa2, sb1, sb2, m1, acc1, m2, acc2, *, bk, sigs, lam_init):
    i = pl.program_id(2)
    dv = A_V_DIM
    q = q_ref[0, 0] * (A_QK_DIM ** -0.5 * LOG2E)
    lane = lax.broadcasted_iota(jnp.int32, (1, 2 * A_QK_DIM), 1)
    q1 = jnp.where(lane < A_QK_DIM, q, 0.0).astype(BF16)
    q2 = jnp.where(lane >= A_QK_DIM, q, 0.0).astype(BF16)
    for m, acc in ((m1, acc1), (m2, acc2)):
        m[...] = jnp.full(m.shape, -jnp.inf, F32)
        acc[...] = jnp.zeros(acc.shape, F32)

    def rows(j):
        return pl.ds(j * bk if isinstance(j, int) else pl.multiple_of(j * bk, bk), bk)

    def qk(j):
        kb = k_ref[0, 0, rows(j), :]
        return _dot_nt(q1, kb), _dot_nt(q2, kb)

    def consume(tiles, j, kind):
        vb = v_ref[0, 0, rows(j), :]
        for s, m, acc in zip(tiles, (m1, m2), (acc1, acc2)):
            if kind != FAR:
                s = s + bias_ref[kind[1], 0]
            _softmax_block(s, vb, m, acc)

    _kv_sweep(pairs_ref[i], sig_ref[i], sigs, qk, consume, (sa1, sa2), (sb1, sb2))

    lp = lam_ref[...]
    lam = (jnp.exp(jnp.sum(lp[0:1] * lp[1:2], axis=-1, keepdims=True))
           - jnp.exp(jnp.sum(lp[2:3] * lp[3:4], axis=-1, keepdims=True)) + lam_init)
    a1 = acc1[...]
    a2 = acc2[...]
    o = a1[:, 0:dv] / a1[:, dv:dv + 1] - lam * (a2[:, 0:dv] / a2[:, dv:dv + 1])
    o = o * lax.rsqrt(jnp.mean(o * o, axis=-1, keepdims=True) + EPS) * g_ref[...]
    o_ref[0, 0] = o * (1.0 - lam_init)


def _diff_attn(q, k, v_aug, rel_bias, a_lambda, a_norm_g, plen, bq, bk, lam_init):
    b, h, sq, dq = q.shape
    sk = k.shape[2]
    dv = A_V_DIM
    nq, nk = sq // bq, sk // bk
    kinds, deltas = _diff_plan(nq, nk, plen, bq, bk)
    n_pairs, sig_id, sigs = _sweep_plan(kinds)
    tiles = _diff_bias_tiles(rel_bias, deltas, bq, bk)
    nt = tiles.shape[0]
    per_head = lambda shape: pl.BlockSpec((1, 1) + shape, lambda b_, h_, i, *_: (b_, h_, 0, 0))
    grid_spec = pltpu.PrefetchScalarGridSpec(
        num_scalar_prefetch=2,
        grid=(b, h, nq),
        in_specs=[pl.BlockSpec((1, 1, bq, dq), lambda b_, h_, i, *_: (b_, h_, i, 0)),
                  per_head((sk, dq)), per_head((sk, LANES)),
                  pl.BlockSpec((nt, 1, bq, bk), lambda b_, h_, i, *_: (0, h_, 0, 0)),
                  pl.BlockSpec((4, A_QK_DIM), lambda b_, h_, i, *_: (0, 0)),
                  pl.BlockSpec((1, dv), lambda b_, h_, i, *_: (0, 0))],
        out_specs=pl.BlockSpec((1, 1, bq, dv), lambda b_, h_, i, *_: (b_, h_, i, 0)),
        scratch_shapes=[pltpu.VMEM((bq, bk), F32)] * 4 + [pltpu.VMEM((bq, LANES), F32)] * 4,
    )
    return pl.pallas_call(
        functools.partial(_diff_attn_body, bk=bk, sigs=sigs, lam_init=lam_init),
        grid_spec=grid_spec,
        out_shape=jax.ShapeDtypeStruct((b, h, sq, dv), F32),
        compiler_params=_cparams(("parallel", "parallel", "arbitrary")),
        name="diff_attn",
    )(jnp.asarray(n_pairs), jnp.asarray(sig_id), q, k, v_aug, tiles, a_lambda, a_norm_g.reshape(1, dv))


MASK = 'mask'


def _forget_attn_body(pairs_ref, sig_ref, q_ref, k_ref, v_ref, ck_ref, cref_ref, o_ref, sa, sb, m, acc,
                      *, plen, bq, bk, sigs):
    i = pl.program_id(2)
    d = B_HEAD_DIM
    q = (q_ref[0, 0] * (d ** -0.5 * LOG2E)).astype(BF16)
    cref = cref_ref[0, 0, 0]
    m[...] = jnp.full(m.shape, -jnp.inf, F32)
    acc[...] = jnp.zeros(acc.shape, F32)

    def rows(j):
        return pl.ds(j * bk if isinstance(j, int) else pl.multiple_of(j * bk, bk), bk)

    def qk(j):
        e = (ck_ref[0, 0, pl.ds(j, 1), :] - cref) * LOG2E
        return (_dot_nt(q, k_ref[0, 0, rows(j), :]) - e,)

    def consume(tiles, j, kind):
        s = tiles[0]
        if kind == MASK:
            kpos = j * bk + lax.broadcasted_iota(jnp.int32, (bq, bk), 1)
            qpos = plen + i * bq + lax.broadcasted_iota(jnp.int32, (bq, bk), 0)
            s = jnp.where(kpos <= qpos, s, NEG_BIG)
        _softmax_block(s, v_ref[0, 0, rows(j), :], m, acc)

    _kv_sweep(pairs_ref[i], sig_ref[i], sigs, qk, consume, (sa,), (sb,))
    a = acc[...]
    o_ref[0, 0] = a[:, 0:d] / a[:, d:d + 1]


def _forget_attn(q, k, v_aug, cum, plen, bq, bk):
    b, h, sq, d = q.shape
    sk = k.shape[2]
    nq, nk = sq // bq, sk // bk
    kinds = []
    for i in range(nq):
        q_lo, q_hi = plen + i * bq, plen + i * bq + bq - 1
        kinds.append([FAR if j * bk + bk - 1 <= q_lo else MASK for j in range(nk) if j * bk <= q_hi])
    n_pairs, sig_id, sigs = _sweep_plan(kinds)
    cref = cum[:, :, plen:plen + sq:bq].reshape(b, h, nq, 1, 1)
    ck = cum.reshape(b, h, nk, bk)
    per_head = lambda shape: pl.BlockSpec((1, 1) + shape, lambda b_, h_, i, *_: (b_, h_, 0, 0))
    grid_spec = pltpu.PrefetchScalarGridSpec(
        num_scalar_prefetch=2,
        grid=(b, h, nq),
        in_specs=[pl.BlockSpec((1, 1, bq, d), lambda b_, h_, i, *_: (b_, h_, i, 0)),
                  per_head((sk, d)), per_head((sk, LANES)), per_head((nk, bk)),
                  pl.BlockSpec((1, 1, 1, 1, 1), lambda b_, h_, i, *_: (b_, h_, i, 0, 0))],
        out_specs=pl.BlockSpec((1, 1, bq, d), lambda b_, h_, i, *_: (b_, h_, i, 0)),
        scratch_shapes=[pltpu.VMEM((bq, bk), F32)] * 2 + [pltpu.VMEM((bq, LANES), F32)] * 2,
    )
    return pl.pallas_call(
        functools.partial(_forget_attn_body, plen=plen, bq=bq, bk=bk, sigs=sigs),
        grid_spec=grid_spec,
        out_shape=jax.ShapeDtypeStruct((b, h, sq, d), F32),
        compiler_params=_cparams(("parallel", "parallel", "arbitrary")),
        name="forget_attn",
    )(jnp.asarray(n_pairs), jnp.asarray(sig_id), q, k, v_aug, ck, cref)


_CONV_PAD = 8


def _causal_conv_tile(buf, x, cw_ref, rows):
    buf[_CONV_PAD:_CONV_PAD + rows, :] = x
    lo = _CONV_PAD - (CONV_W - 1)
    y = buf[lo:lo + rows, :] * cw_ref[0:1, :]
    for j in range(1, CONV_W):
        y = y + buf[lo + j:lo + j + rows, :] * cw_ref[j:j + 1, :]
    buf[lo:_CONV_PAD, :] = x[rows - (CONV_W - 1):rows, :]
    return y


def _rg_lru_body(x_ref, gate_ref, h0_ref, conv0_ref, cw_ref, cb_ref, wa_ref, ba_ref, wx_ref, bx_ref, lam_ref,
                 y_ref, hn_ref, convn_ref, buf, hcar, *, ts):
    n = pl.program_id(1)

    @pl.when(n == 0)
    def _():
        buf[_CONV_PAD - (CONV_W - 1):_CONV_PAD, :] = conv0_ref[0]
        hcar[...] = h0_ref[0]

    x = x_ref[0]
    xc = _causal_conv_tile(buf, x, cw_ref, ts) + cb_ref[...]
    r = jax.nn.sigmoid(_dot(xc, wa_ref[...], HI) + ba_ref[...])
    i = jax.nn.sigmoid(_dot(xc, wx_ref[...], HI) + bx_ref[...])
    log_a = -C_POWER * r * jax.nn.softplus(-lam_ref[...])
    a = jnp.exp(log_a)
    th = jnp.tanh(log_a)
    b = jnp.sqrt(-2.0 * th / (1.0 - th)) * i * xc
    row = lax.broadcasted_iota(jnp.int32, a.shape, 0)
    d = 1
    while d < ts:
        keep = row >= d
        a_sh = jnp.where(keep, pltpu.roll(a, d, 0), 1.0)
        b_sh = jnp.where(keep, pltpu.roll(b, d, 0), 0.0)
        b = a * b_sh + b
        a = a * a_sh
        d *= 2
    h = a * hcar[...] + b
    y_ref[0] = h * jax.nn.gelu(gate_ref[0])
    hcar[...] = h[ts - 1:ts, :]

    @pl.when(n == pl.num_programs(1) - 1)
    def _():
        hn_ref[0] = h[ts - 1:ts, :]
        convn_ref[0] = x[ts - (CONV_W - 1):ts, :]


def _block_diag(w):
    n, d, _ = w.shape
    eye = jnp.eye(n, dtype=w.dtype)
    return (eye[:, None, :, None] * w[:, :, None, :]).reshape(n * d, n * d)


def _rg_lru(x, gate, h0, conv0, p, ts):
    b, s, w = x.shape
    row = lambda a: a.reshape(1, w)
    tile = pl.BlockSpec((1, ts, w), lambda i, n: (i, n, 0))
    const = lambda shape: pl.BlockSpec(shape, lambda i, n: (0,) * len(shape))
    return pl.pallas_call(
        functools.partial(_rg_lru_body, ts=ts),
        grid=(b, s // ts),
        in_specs=[tile, tile,
                  pl.BlockSpec((1, 1, w), lambda i, n: (i, 0, 0)),
                  pl.BlockSpec((1, CONV_W - 1, w), lambda i, n: (i, 0, 0)),
                  const((CONV_W, w)), const((1, w)), const((w, w)), const((1, w)), const((w, w)), const((1, w)),
                  const((1, w))],
        out_specs=[tile,
                   pl.BlockSpec((1, 1, w), lambda i, n: (i, 0, 0)),
                   pl.BlockSpec((1, CONV_W - 1, w), lambda i, n: (i, 0, 0))],
        out_shape=[jax.ShapeDtypeStruct((b, s, w), F32), jax.ShapeDtypeStruct((b, 1, w), F32),
                   jax.ShapeDtypeStruct((b, CONV_W - 1, w), F32)],
        scratch_shapes=[pltpu.VMEM((_CONV_PAD + ts, w), F32), pltpu.VMEM((1, w), F32)],
        compiler_params=_cparams(("parallel", "arbitrary")),
        name="rg_lru",
    )(x, gate, h0.reshape(b, 1, w), conv0, p['c_conv_w'], row(p['c_conv_b']),
      _block_diag(p['c_gate_a_w']), row(p['c_gate_a_b']), _block_diag(p['c_gate_x_w']), row(p['c_gate_x_b']),
      row(p['c_lambda']))


def _gdn_body(qkv_ref, z_ref, bl_ref, al_ref, s0_ref, conv0_ref, cw_ref, alog_ref, dtb_ref, ng_ref,
              o_ref, sn_ref, convn_ref, buf, st, *, c, tail):
    n = pl.program_id(1)
    w = D_WIDTH
    hd = D_HEAD_DIM

    @pl.when(n == 0)
    def _():
        buf[_CONV_PAD - (CONV_W - 1):_CONV_PAD, :] = conv0_ref[0]
        st[...] = s0_ref[0]

    x = qkv_ref[0]
    y = jax.nn.silu(_causal_conv_tile(buf, x, cw_ref, c))
    q, k, v = y[:, 0:w], y[:, w:2 * w], y[:, 2 * w:3 * w]

    same_head = (lax.broadcasted_iota(jnp.int32, (w, w), 0) // hd) == (lax.broadcasted_iota(jnp.int32, (w, w), 1) // hd)
    head_sum = same_head.astype(BF16)

    def head_sums(x2):
        hi, lo = _split_bf16(x2)
        return _dot(hi, head_sum) + _dot(lo, head_sum)

    def bd(a):
        return jnp.where(same_head, jnp.concatenate([a] * D_HEADS, axis=0), 0.0)

    def dot3_bd(a, b):
        return _dot3(a, bd(b))

    q = q * lax.rsqrt(head_sums(q * q) + EPS) * (hd ** -0.5)
    k = k * lax.rsqrt(head_sums(k * k) + EPS)
    beta = jax.nn.sigmoid(bl_ref[0])
    g = -jnp.exp(alog_ref[...]) * jax.nn.softplus(al_ref[0] + dtb_ref[...])
    if tail < c:
        real = lax.broadcasted_iota(jnp.int32, (c, w), 0) < jnp.where(n == pl.num_programs(1) - 1, tail, c)
        beta = jnp.where(real, beta, 0.0)
        g = jnp.where(real, g, 0.0)

    ti = lax.broadcasted_iota(jnp.int32, (c, w), 0)
    tj = lax.broadcasted_iota(jnp.int32, (c, w), 1) % hd
    incl = ti >= tj
    strict = ti > tj
    diag = ti == tj
    ci = lax.broadcasted_iota(jnp.int32, (c, c), 0)
    cj = lax.broadcasted_iota(jnp.int32, (c, c), 1)
    gcum = _dot_exact_left((ci >= cj).astype(BF16), g)
    g_row = _dot_exact_left(jnp.ones((c, c), BF16), jnp.where(diag, gcum, 0.0))
    decay = jnp.where(incl, jnp.exp(gcum - g_row), 0.0)
    eg = jnp.exp(gcum)
    vb = v * beta
    kb = k * beta
    kbe = kb * eg
    q_g = q * eg
    g_last = gcum[c - 1:c, :]
    k_d = k * jnp.exp(g_last - gcum)

    k_t = jnp.where(same_head, _dot_tn(k.astype(BF16), diag.astype(BF16)), 0.0).astype(BF16)
    gram = _dot(jnp.concatenate([kb, q], axis=0).astype(BF16), k_t)
    low = jnp.where(strict, gram[0:c] * decay, 0.0)
    attn = gram[c:2 * c] * decay
    inv = diag.astype(F32) - low
    pw = low
    span = 2
    while span < c:
        pw = dot3_bd(pw, pw)
        inv = inv + dot3_bd(inv, pw)
        span *= 2
    v_w = dot3_bd(inv, vb)
    k_w = dot3_bd(inv, kbe)

    state = st[...]
    through = _dot3(jnp.concatenate([k_w, q_g], axis=0), state)
    v_new = v_w - through[0:c]
    o = through[c:2 * c] + dot3_bd(attn, v_new)
    kd_hi, kd_lo = _split_bf16(k_d)
    vn_hi, vn_lo = _split_bf16(v_new)
    outer = _dot_tn(kd_hi, vn_hi) + (_dot_tn(kd_hi, vn_lo) + _dot_tn(kd_lo, vn_hi))
    state = state * jnp.exp(g_last) + jnp.where(same_head, outer, 0.0)
    st[...] = state

    o = o * lax.rsqrt(head_sums(o * o) * (1.0 / hd) + EPS) * ng_ref[...]
    o_ref[0] = o * jax.nn.silu(z_ref[0])

    @pl.when(n == pl.num_programs(1) - 1)
    def _():
        sn_ref[0] = state
        convn_ref[0] = x[tail - (CONV_W - 1):tail, :]


def _gdn(qkv, z, beta_logit, alpha_logit, s0, conv0, p):
    b, s_true, w3 = qkv.shape
    w = D_WIDTH
    c = D_HEAD_DIM
    s = -(-s_true // c) * c
    tail = s_true - (s - c)
    assert tail >= CONV_W - 1
    if s != s_true:
        pad = lambda a: jnp.pad(a, ((0, 0), (0, s - s_true), (0, 0)))
        qkv, z, beta_logit, alpha_logit = pad(qkv), pad(z), pad(beta_logit), pad(alpha_logit)
    s0_bd = jax.vmap(_block_diag)(s0)
    per_lane = lambda a: jnp.repeat(a, D_HEAD_DIM).reshape(1, w)
    tile = lambda n_: pl.BlockSpec((1, c, n_), lambda i, n: (i, n, 0))
    const = lambda shape: pl.BlockSpec(shape, lambda i, n: (0,) * len(shape))
    per_b = lambda shape: pl.BlockSpec((1,) + shape, lambda i, n: (i,) + (0,) * len(shape))
    o, sn, convn = pl.pallas_call(
        functools.partial(_gdn_body, c=c, tail=tail),
        grid=(b, s // c),
        in_specs=[tile(w3), tile(w), tile(w), tile(w), per_b((w, w)), per_b((CONV_W - 1, w3)),
                  const((CONV_W, w3)), const((1, w)), const((1, w)), const((1, w))],
        out_specs=[tile(w), per_b((w, w)), per_b((CONV_W - 1, w3))],
        out_shape=[jax.ShapeDtypeStruct((b, s, w), F32), jax.ShapeDtypeStruct((b, w, w), F32),
                   jax.ShapeDtypeStruct((b, CONV_W - 1, w3), F32)],
        scratch_shapes=[pltpu.VMEM((_CONV_PAD + c, w3), F32), pltpu.VMEM((w, w), F32)],
        compiler_params=_cparams(("parallel", "arbitrary")),
        name="gdn",
    )(qkv, z, beta_logit, alpha_logit, s0_bd, conv0, p['d_conv_w'], per_lane(p['d_a_log']), per_lane(p['d_dt_bias']),
      jnp.tile(p['d_norm_g'], D_HEADS).reshape(1, w))
    hd = D_HEAD_DIM
    sn = jnp.stack([sn[:, h * hd:(h + 1) * hd, h * hd:(h + 1) * hd] for h in range(D_HEADS)], axis=1)
    return o[:, :s_true], sn, convn


_PEER_SLABS = 2 * PEER_HEADS


def _out_proj_body(x_ref, oa_ref, ob_ref, oc_ref, od_ref, wo_ref, g_ref, wq_ref, sk_ref,
                   x2_ref, h2_ref, st_ref):
    x2 = x_ref[...]
    for grp, o_ref in enumerate((oa_ref, ob_ref, oc_ref, od_ref)):
        x2 = x2 + _dot(o_ref[...].astype(BF16), wo_ref[grp * GROUP_WIDTH:(grp + 1) * GROUP_WIDTH, :])
    x2_ref[...] = x2
    hb = (x2 * lax.rsqrt(jnp.mean(x2 * x2, axis=-1, keepdims=True) + EPS) * g_ref[...]).astype(BF16)
    h2_ref[...] = hb
    q = _dot(hb, wq_ref[...])
    for slab in range(_PEER_SLABS):
        st_ref[slab] = _dot_nt(sk_ref[slab % 2], q[:, slab * PEER_HALF:(slab + 1) * PEER_HALF], HI)


def _out_proj(x, o_a, o_b, o_c, o_d, w_out, norm2_g, wq, subkeys, tm):
    t = x.shape[0]
    tile = lambda n: pl.BlockSpec((tm, n), lambda i: (i, 0))
    const = lambda shape: pl.BlockSpec(shape, lambda i: (0,) * len(shape))
    nq = PEER_HEADS * PEER_QUERY_DIM
    return pl.pallas_call(
        _out_proj_body,
        grid=(t // tm,),
        in_specs=[tile(D_MODEL), tile(GROUP_WIDTH), tile(GROUP_WIDTH), tile(GROUP_WIDTH), tile(GROUP_WIDTH),
                  const((D_MODEL, D_MODEL)), const((1, D_MODEL)), const((D_MODEL, nq)),
                  const((2, PEER_N_KEYS, PEER_HALF))],
        out_specs=[tile(D_MODEL), tile(D_MODEL),
                   pl.BlockSpec((_PEER_SLABS, PEER_N_KEYS, tm), lambda i: (0, 0, i))],
        out_shape=[jax.ShapeDtypeStruct((t, D_MODEL), F32), jax.ShapeDtypeStruct((t, D_MODEL), BF16),
                   jax.ShapeDtypeStruct((_PEER_SLABS, PEER_N_KEYS, t), F32)],
        compiler_params=_cparams(("parallel",)),
        name="out_proj",
    )(x, o_a, o_b, o_c, o_d, w_out, norm2_g.reshape(1, D_MODEL), wq, subkeys)


def _top_values(x, count):
    vals = []
    for _ in range(count):
        m = jnp.max(x, axis=0, keepdims=True)
        vals.append(m)
        x = jnp.where(x == m, -jnp.inf, x)
    return vals, x


def _peer_select_body(s_ref, c_ref, th_ref, s2m_ref, e2_ref):
    s1 = s_ref[0]
    s2 = s_ref[1]
    v1, rest1 = _top_values(s1, PEER_TOPK)
    v2, rest2 = _top_values(s2, PEER_TOPK)
    sel1 = rest1 == -jnp.inf
    sel2 = rest2 == -jnp.inf
    v2_all = jnp.concatenate(v2, axis=0)
    cand = jnp.concatenate([v1[i] + v2_all for i in range(PEER_TOPK)], axis=0)
    top, _ = _top_values(cand, PEER_TOPK)
    tau = top[PEER_TOPK - 1]
    z = jnp.ones_like(tau)
    for t in top[1:]:
        z = z + jnp.exp(t - top[0])
    s1m = jnp.where(sel1, s1, -jnp.inf)
    theta = jnp.full(s1.shape, jnp.inf, F32)
    for j in range(PEER_TOPK):
        theta = jnp.minimum(theta, jnp.where(s1m + v2[j] >= tau, v2[j], jnp.inf))
    th_ref[0] = theta
    c_ref[0] = jnp.where(sel1, jnp.exp(s1 - v1[0]), 0.0) / z
    s2m_ref[0] = jnp.where(sel2, s2, -jnp.inf)
    e2_ref[0] = jnp.where(sel2, jnp.exp(s2 - v2[0]), 0.0)


def _peer_select(st, tm):
    t = st.shape[2]
    out_spec = pl.BlockSpec((1, PEER_N_KEYS, tm), lambda i, h: (h, 0, i))
    shape = jax.ShapeDtypeStruct((PEER_HEADS, PEER_N_KEYS, t), F32)
    return pl.pallas_call(
        _peer_select_body,
        grid=(t // tm, PEER_HEADS),
        in_specs=[pl.BlockSpec((2, PEER_N_KEYS, tm), lambda i, h: (h, 0, i))],
        out_specs=[out_spec] * 4,
        out_shape=[shape] * 4,
        compiler_params=_cparams(("parallel", "parallel")),
        name="peer_select",
    )(st)


_PEER_GROUP = 2


def _peer_dense_body(ht_ref, u_ref, vt_ref, c_ref, th_ref, s2m_ref, e2_ref, o_ref, act, pbuf, *, na):
    e = pl.program_id(1)
    ht = ht_ref[...]
    rows_g = _PEER_GROUP * PEER_N_KEYS
    groups = na // _PEER_GROUP

    def span(g):
        return slice(g * rows_g, (g + 1) * rows_g)

    def activations(g):
        act[span(g), :] = _dot(u_ref[span(g), :], ht)

    def weights(g):
        for a in range(g * _PEER_GROUP, (g + 1) * _PEER_GROUP):
            rows = slice(a * PEER_N_KEYS, (a + 1) * PEER_N_KEYS)
            w = None
            for h in range(PEER_HEADS):
                theta = th_ref[h, a:a + 1, :]
                term = jnp.where(s2m_ref[h] >= theta, e2_ref[h], 0.0) * c_ref[h, a:a + 1, :]
                w = term if w is None else w + term
            pbuf[rows, :] = (w * jax.nn.gelu(act[rows, :])).astype(BF16)

    @pl.when(e == 0)
    def _():
        o_ref[...] = jnp.zeros(o_ref.shape, F32)

    activations(0)
    contrib = None
    for g in range(groups):
        if g + 1 < groups:
            activations(g + 1)
        weights(g)
        part = _dot(vt_ref[:, span(g)], pbuf[span(g), :])
        contrib = part if contrib is None else contrib + part
    o_ref[...] += contrib


def _peer_dense(ht, u, vt, c, theta, s2m, e2, tm, eb):
    t = ht.shape[1]
    na = eb // PEER_N_KEYS
    key_rows = pl.BlockSpec((PEER_HEADS, na, tm), lambda i, e: (0, e, i))
    all_keys = pl.BlockSpec((PEER_HEADS, PEER_N_KEYS, tm), lambda i, e: (0, 0, i))
    return pl.pallas_call(
        functools.partial(_peer_dense_body, na=na),
        grid=(t // tm, PEER_EXPERTS // eb),
        in_specs=[pl.BlockSpec((D_MODEL, tm), lambda i, e: (0, i)),
                  pl.BlockSpec((eb, D_MODEL), lambda i, e: (e, 0)),
                  pl.BlockSpec((D_MODEL, eb), lambda i, e: (0, e)),
                  key_rows, key_rows, all_keys, all_keys],
        out_specs=pl.BlockSpec((D_MODEL, tm), lambda i, e: (0, i)),
        out_shape=jax.ShapeDtypeStruct((D_MODEL, t), F32),
        scratch_shapes=[pltpu.VMEM((eb, tm), F32), pltpu.VMEM((eb, tm), BF16)],
        compiler_params=_cparams(("parallel", "arbitrary")),
        name="peer_dense",
    )(ht, u, vt, c, theta, s2m, e2)


def _residual_body(x_ref, pt_ref, g_ref, o_ref, *, final_norm):
    x = x_ref[...] + pt_ref[...].T
    if final_norm:
        x = x * lax.rsqrt(jnp.mean(x * x, axis=-1, keepdims=True) + EPS) * g_ref[...]
    o_ref[...] = x


def _residual(x, pt, g, tm, final_norm):
    t = x.shape[0]
    return pl.pallas_call(
        functools.partial(_residual_body, final_norm=final_norm),
        grid=(t // tm,),
        in_specs=[pl.BlockSpec((tm, D_MODEL), lambda i: (i, 0)),
                  pl.BlockSpec((D_MODEL, tm), lambda i: (0, i)),
                  pl.BlockSpec((1, D_MODEL), lambda i: (0, 0))],
        out_specs=pl.BlockSpec((tm, D_MODEL), lambda i: (i, 0)),
        out_shape=jax.ShapeDtypeStruct((t, D_MODEL), F32),
        compiler_params=_cparams(("parallel",)),
        name="residual",
    )(x, pt, g.reshape(1, D_MODEL))


def _heads_first(a):
    return jnp.transpose(a, (0, 2, 1, 3))


def _prep_layer_weights(w):
    return {
        'w_in': _permute_w_in(w['w_in']),
        'w_out': w['w_out'].astype(BF16),
        'wq': w['peer_wq'].astype(BF16),
        'u': w['peer_u'].astype(BF16),
        'vt': w['peer_v'].astype(BF16).T,
    }


def _mixers(x2d, bsz, s, past, p, pw, lam_init, cfg):
    pa_k, pa_v, pb_k, pb_v, pb_logf, c_h0, c_conv0, d_s0, d_conv0 = past
    plen = pa_k.shape[1]
    sk = plen + s
    (a_q, a_k, a_v, b_q, b_k, b_v, c_x, c_g, d_qkv, d_z, d_beta, d_alpha, small) = _in_proj(
        x2d, p['norm1_g'], pw['w_in'], cfg['tm_in'])
    bq, bk = cfg['bq'], cfg['bk']

    def seq(a, *tail):
        return a.reshape(bsz, s, *tail)

    k_all = jnp.concatenate([pa_k, seq(a_k, A_HEADS, 2 * A_QK_DIM)], axis=1)
    v_all = jnp.concatenate([pa_v, seq(a_v, A_HEADS, A_V_DIM)], axis=1)
    o_a = _diff_attn(_heads_first(seq(a_q, A_HEADS, 2 * A_QK_DIM)), _heads_first(k_all).astype(BF16),
                     _with_ones(_heads_first(v_all)),
                     p['rel_bias'], p['a_lambda'], p['a_norm_g'], plen, bq, bk, lam_init)
    o_a = _heads_first(o_a).reshape(bsz * s, GROUP_WIDTH)

    rows = -(-sk // LANES)
    rows = -(-rows // 8) * 8
    f_vals = jnp.concatenate([pb_logf, seq(small[:, 0:B_HEADS], B_HEADS)], axis=1)
    f_vals = jnp.pad(f_vals, ((0, 0), (0, rows * LANES - sk), (0, 0)))
    f_vals = jnp.transpose(f_vals, (0, 2, 1)).reshape(bsz, B_HEADS, rows, LANES)
    logf, cum = _logf_cumsum(f_vals, p['b_forget_bias'], plen)
    cum = cum.reshape(bsz, B_HEADS, rows * LANES)[:, :, :sk]
    b_logf = jnp.transpose(logf.reshape(bsz, B_HEADS, rows * LANES)[:, :, plen:sk], (0, 2, 1))
    kb_all = jnp.concatenate([pb_k, seq(b_k, B_HEADS, B_HEAD_DIM)], axis=1)
    vb_all = jnp.concatenate([pb_v, seq(b_v, B_HEADS, B_HEAD_DIM)], axis=1)
    o_b = _forget_attn(_heads_first(seq(b_q, B_HEADS, B_HEAD_DIM)), _heads_first(kb_all).astype(BF16),
                       _with_ones(_heads_first(vb_all)), cum, plen, bq, bk)
    o_b = _heads_first(o_b).reshape(bsz * s, GROUP_WIDTH)

    o_c, c_h, c_conv = _rg_lru(seq(c_x, C_WIDTH), seq(c_g, C_WIDTH), c_h0, c_conv0, p, cfg['ts'])

    o_d, d_s, d_conv = _gdn(seq(d_qkv, 3 * D_WIDTH), seq(d_z, D_WIDTH), seq(d_beta, D_WIDTH), seq(d_alpha, D_WIDTH),
                            d_s0, d_conv0, p)

    state = (seq(a_k, A_HEADS, 2 * A_QK_DIM), seq(a_v, A_HEADS, A_V_DIM), seq(b_k, B_HEADS, B_HEAD_DIM),
             seq(b_v, B_HEADS, B_HEAD_DIM), b_logf, c_h.reshape(bsz, C_WIDTH), c_conv, d_s, d_conv)
    return (o_a, o_b, o_c.reshape(bsz * s, C_WIDTH), o_d.reshape(bsz * s, D_WIDTH)), state


def _layer(x2d, bsz, s, past, p, pw, lam_init, cfg):
    mix, state = _mixers(x2d, bsz, s, past, p, pw, lam_init, cfg)
    x2, h2, st = _out_proj(x2d, *mix, pw['w_out'], p['norm2_g'], pw['wq'], p['peer_subkeys'], cfg['tm_out'])
    c, theta, s2m, e2 = _peer_select(st, cfg['tm_sel'])
    peer_t = _peer_dense(h2.T, pw['u'], pw['vt'], c, theta, s2m, e2, cfg['tm_peer'], cfg['eb'])
    return x2, peer_t, state


_PROMPT_CFG = dict(tm_in=512, bq=1024, bk=512, ts=256, tm_out=256, tm_sel=512, tm_peer=256, eb=2048, tm_res=256)


def _sample_cfg(s, sk):
    return dict(tm_in=512, bq=s, bk=sk, ts=s, tm_out=256, tm_sel=512, tm_peer=256, eb=2048, tm_res=256)


def kernel(x_prompt, x_sample, cache_a_k, cache_a_v, cache_b_k, cache_b_v, cache_b_logf, state_c_h, state_c_conv, state_d_s, state_d_conv, norm1_g, norm2_g, final_norm_g, w_in, w_out, rel_bias, a_lambda, a_norm_g, b_forget_bias, c_conv_w, c_conv_b, c_gate_a_w, c_gate_a_b, c_gate_x_w, c_gate_x_b, c_lambda, d_conv_w, d_a_log, d_dt_bias, d_norm_g, peer_wq, peer_subkeys, peer_u, peer_v):
    pb, ps, _ = x_prompt.shape
    sb, ss, _ = x_sample.shape
    dt = x_prompt.dtype
    xp = x_prompt.reshape(pb * ps, D_MODEL)
    xs = x_sample.reshape(sb * ss, D_MODEL)
    scfg = _sample_cfg(ss, cache_a_k.shape[2] + ss)
    prompt_out, sample_out = [], []
    for l in range(DEPTH):
        p = {
            'norm1_g': norm1_g[l], 'norm2_g': norm2_g[l], 'rel_bias': rel_bias, 'a_lambda': a_lambda[l],
            'a_norm_g': a_norm_g[l], 'b_forget_bias': b_forget_bias[l],
            'c_conv_w': c_conv_w[l], 'c_conv_b': c_conv_b[l],
            'c_gate_a_w': c_gate_a_w[l], 'c_gate_a_b': c_gate_a_b[l],
            'c_gate_x_w': c_gate_x_w[l], 'c_gate_x_b': c_gate_x_b[l], 'c_lambda': c_lambda[l],
            'd_conv_w': d_conv_w[l], 'd_a_log': d_a_log[l], 'd_dt_bias': d_dt_bias[l], 'd_norm_g': d_norm_g[l],
            'peer_subkeys': peer_subkeys[l],
        }
        pw = _prep_layer_weights({'w_in': w_in[l], 'w_out': w_out[l], 'peer_wq': peer_wq[l],
                                  'peer_u': peer_u[l], 'peer_v': peer_v[l]})
        lam_init = 0.8 - 0.6 * math.exp(-0.3 * l)
        empty = (
            jnp.zeros((pb, 0, A_HEADS, 2 * A_QK_DIM), dt), jnp.zeros((pb, 0, A_HEADS, A_V_DIM), dt),
            jnp.zeros((pb, 0, B_HEADS, B_HEAD_DIM), dt), jnp.zeros((pb, 0, B_HEADS, B_HEAD_DIM), dt),
            jnp.zeros((pb, 0, B_HEADS), dt),
            jnp.zeros((pb, C_WIDTH), dt), jnp.zeros((pb, CONV_W - 1, C_WIDTH), dt),
            jnp.zeros((pb, D_HEADS, D_HEAD_DIM, D_HEAD_DIM), dt), jnp.zeros((pb, CONV_W - 1, 3 * D_WIDTH), dt),
        )
        last = l == DEPTH - 1
        xp2, peer_p, st_p = _layer(xp, pb, ps, empty, p, pw, lam_init, _PROMPT_CFG)
        xp = _residual(xp2, peer_p, final_norm_g, _PROMPT_CFG['tm_res'], last)
        prompt_out.append(st_p)
        past = (cache_a_k[l], cache_a_v[l], cache_b_k[l], cache_b_v[l], cache_b_logf[l],
                state_c_h[l], state_c_conv[l], state_d_s[l], state_d_conv[l])
        xs2, peer_s, st_s = _layer(xs, sb, ss, past, p, pw, lam_init, scfg)
        xs = _residual(xs2, peer_s, final_norm_g, scfg['tm_res'], last)
        sample_out.append(st_s)

    y_prompt = xp.reshape(pb, ps, D_MODEL)
    y_sample = xs.reshape(sb, ss, D_MODEL)
    p_states = [jnp.stack(z, axis=0) for z in zip(*prompt_out)]
    s_states = [jnp.stack(z, axis=0) for z in zip(*sample_out)]
    return (y_prompt, y_sample, *p_states, *s_states)
```

```python
import functools
import math

import numpy as np
import jax
import jax.numpy as jnp
from jax import lax
from jax.experimental import pallas as pl
from jax.experimental.pallas import tpu as pltpu

F32 = jnp.float32
BF16 = jnp.bfloat16
HI = lax.Precision.HIGHEST

D_MODEL = 1024
DEPTH = 2
CHUNK = 64
EPS = 1e-6
CONV_W = 4
GROUP_WIDTH = D_MODEL // 4
A_HEADS = 4
A_QK_DIM = GROUP_WIDTH // (2 * A_HEADS)
A_V_DIM = GROUP_WIDTH // A_HEADS
NUM_BUCKETS = 32
REL_MAX_DIST = 256
B_HEADS = 4
B_HEAD_DIM = GROUP_WIDTH // B_HEADS
C_WIDTH = GROUP_WIDTH
C_BLOCKS = 4
C_BLOCK_DIM = C_WIDTH // C_BLOCKS
C_POWER = 8.0
D_HEADS = 4
D_HEAD_DIM = GROUP_WIDTH // D_HEADS
D_WIDTH = GROUP_WIDTH
PEER_HEADS = 8
PEER_N_KEYS = 128
PEER_EXPERTS = PEER_N_KEYS * PEER_N_KEYS
PEER_TOPK = 16
PEER_QUERY_DIM = 256
PEER_HALF = PEER_QUERY_DIM // 2

LANES = 128
NEG_BIG = -1e30
VMEM_LIMIT = 56 * 1024 * 1024


def _cparams(sem):
    return pltpu.CompilerParams(dimension_semantics=sem, vmem_limit_bytes=VMEM_LIMIT)


def _dot(a, b, precision=None):
    return jnp.dot(a, b, preferred_element_type=F32, precision=precision)


def _dot_nt(a, b, precision=None):
    return lax.dot_general(a, b, (((1,), (1,)), ((), ())), preferred_element_type=F32, precision=precision)


def _dot_tn(a, b, precision=None):
    return lax.dot_general(a, b, (((0,), (0,)), ((), ())), preferred_element_type=F32, precision=precision)


def _split_bf16(x):
    hi = x.astype(BF16)
    return hi, (x - hi.astype(F32)).astype(BF16)


def _dot3(a, b):
    ah, al = _split_bf16(a)
    bh, bl = _split_bf16(b)
    return _dot(ah, bh) + (_dot(ah, bl) + _dot(al, bh))


def _dot_exact_left(a, b):
    b1 = b.astype(BF16)
    r1 = b - b1.astype(F32)
    b2 = r1.astype(BF16)
    b3 = (r1 - b2.astype(F32)).astype(BF16)
    return _dot(a, b1) + (_dot(a, b2) + _dot(a, b3))


_IN_WIDTHS = (256,) * 8 + (768, 256, 256, 256, LANES)
_IN_TOTAL = sum(_IN_WIDTHS)


def _permute_w_in(w):
    pad = jnp.zeros((w.shape[0], LANES - B_HEADS), w.dtype)
    per_lane = lambda cols: jnp.repeat(cols, D_HEAD_DIM, axis=1)
    return jnp.concatenate([w[:, 0:1536], w[:, 1540:3076], per_lane(w[:, 3076:3080]), per_lane(w[:, 3080:3084]),
                            w[:, 1536:1540], pad], axis=1).astype(BF16)


def _in_proj_body(x_ref, g_ref, w_ref, *outs):
    x = x_ref[...]
    h = x * lax.rsqrt(jnp.mean(x * x, axis=-1, keepdims=True) + EPS) * g_ref[...]
    hb = h.astype(BF16)
    off = 0
    for o_ref, n in zip(outs, _IN_WIDTHS):
        o_ref[...] = _dot(hb, w_ref[:, off:off + n])
        off += n


def _in_proj(x, g, w_perm, tm):
    t = x.shape[0]
    return pl.pallas_call(
        _in_proj_body,
        grid=(t // tm,),
        in_specs=[pl.BlockSpec((tm, D_MODEL), lambda i: (i, 0)),
                  pl.BlockSpec((1, D_MODEL), lambda i: (0, 0)),
                  pl.BlockSpec((D_MODEL, _IN_TOTAL), lambda i: (0, 0))],
        out_specs=[pl.BlockSpec((tm, n), lambda i: (i, 0)) for n in _IN_WIDTHS],
        out_shape=[jax.ShapeDtypeStruct((t, n), F32) for n in _IN_WIDTHS],
        compiler_params=_cparams(("parallel",)),
        name="in_proj",
    )(x, g.reshape(1, D_MODEL), w_perm)


def _logf_cumsum_body(v_ref, b_ref, logf_ref, cum_ref, *, plen, rows):
    vals = v_ref[0, 0]
    pos = lax.broadcasted_iota(jnp.int32, (rows, LANES), 0) * LANES + lax.broadcasted_iota(jnp.int32, (rows, LANES), 1)
    logf = jnp.where(pos >= plen, jax.nn.log_sigmoid(vals + b_ref[0]), vals)
    logf_ref[0, 0] = logf
    kk = lax.broadcasted_iota(jnp.int32, (LANES, LANES), 0)
    jj = lax.broadcasted_iota(jnp.int32, (LANES, LANES), 1)
    in_row = _dot(logf, (kk <= jj).astype(F32), HI)
    tot = jnp.broadcast_to(in_row[:, LANES - 1:LANES], (rows, LANES))
    ri = lax.broadcasted_iota(jnp.int32, (rows, rows), 0)
    rj = lax.broadcasted_iota(jnp.int32, (rows, rows), 1)
    cum_ref[0, 0] = in_row + _dot((rj < ri).astype(F32), tot, HI)


def _logf_cumsum(vals, bias, plen):
    b, h, rows, _ = vals.shape
    spec = pl.BlockSpec((1, 1, rows, LANES), lambda i, j: (i, j, 0, 0))
    return pl.pallas_call(
        functools.partial(_logf_cumsum_body, plen=plen, rows=rows),
        grid=(b, h),
        in_specs=[spec, pl.BlockSpec((1, 1, 1), lambda i, j: (j, 0, 0))],
        out_specs=[spec, spec],
        out_shape=[jax.ShapeDtypeStruct(vals.shape, F32)] * 2,
        compiler_params=_cparams(("parallel", "parallel")),
        name="logf_cumsum",
    )(vals, bias.reshape(h, 1, 1))


LOG2E = math.log2(math.e)
FAR = 'far'


def _sweep_plan(kinds_per_q):
    n_pairs, sig_id, sigs = [], [], []
    for kinds in kinds_per_q:
        n_far = 0
        while n_far < len(kinds) and kinds[n_far] == FAR:
            n_far += 1
        pairs = min(n_far, len(kinds) - 1) // 2
        tail = tuple(kinds[2 * pairs:])
        if tail not in sigs:
            sigs.append(tail)
        n_pairs.append(pairs)
        sig_id.append(sigs.index(tail))
    return np.asarray(n_pairs, np.int32), np.asarray(sig_id, np.int32), sigs


def _kv_sweep(n_pairs, sig, sigs, qk, consume, bufs_a, bufs_b):
    def put(bufs, tiles):
        for buf, tile in zip(bufs, tiles):
            buf[...] = tile

    def get(bufs):
        return tuple(buf[...] for buf in bufs)

    put(bufs_a, qk(0))

    def pair(t, carry):
        j = 2 * t
        put(bufs_b, qk(j + 1))
        consume(get(bufs_a), j, FAR)
        put(bufs_a, qk(j + 2))
        consume(get(bufs_b), j + 1, FAR)
        return carry

    lax.fori_loop(0, n_pairs, pair, 0)
    base = 2 * n_pairs
    for sid, tail in enumerate(sigs):
        @pl.when(sig == sid)
        def _():
            cur, nxt = bufs_a, bufs_b
            for off, kind in enumerate(tail):
                if off + 1 < len(tail):
                    put(nxt, qk(base + off + 1))
                consume(get(cur), base + off, kind)
                cur, nxt = nxt, cur


def _softmax_block(s, v_blk, m_ref, acc_ref):
    m_old = m_ref[...]
    m_new = jnp.maximum(m_old, jnp.max(s, axis=-1, keepdims=True))
    alpha = jnp.exp2(m_old - m_new)
    bk = s.shape[1]
    m_wide = jnp.concatenate([m_new] * (bk // LANES), axis=1) if bk % LANES == 0 else m_new[:, 0:1]
    p = jnp.exp2(s - m_wide)
    acc_ref[...] = alpha * acc_ref[...] + _dot(p.astype(BF16), v_blk)
    m_ref[...] = m_new


def _with_ones(v):
    b, h, s, d = v.shape
    return jnp.concatenate([v, jnp.ones((b, h, s, 1), v.dtype), jnp.zeros((b, h, s, LANES - d - 1), v.dtype)],
                           axis=-1).astype(BF16)


def _t5_bucket(rel):
    half = NUM_BUCKETS // 2
    max_exact = half // 2
    ret = jnp.where(rel > 0, half, 0)
    n = jnp.abs(rel)
    nf = jnp.maximum(n, 1).astype(F32)
    large = max_exact + (jnp.log(nf / max_exact) / math.log(REL_MAX_DIST / max_exact)
                         * (half - max_exact)).astype(jnp.int32)
    large = jnp.minimum(large, half - 1)
    return ret + jnp.where(n < max_exact, n, large)


def _bucket_bias(rel_bias, rel):
    bucket = _t5_bucket(rel)
    table = rel_bias.astype(F32)
    out = jnp.zeros((table.shape[1],) + rel.shape, F32)
    for b in range(NUM_BUCKETS):
        out = jnp.where(bucket[None] == b, table[b].reshape((-1,) + (1,) * rel.ndim), out)
    return out


def _diff_plan(nq, nk, plen, bq, bk):
    far = bk + 2 * REL_MAX_DIST
    kinds_per_q, deltas = [], []
    for i in range(nq):
        q_hi = (plen + i * bq + bq - 1) // CHUNK
        kinds = []
        for j in range(nk):
            if (j * bk) // CHUNK > q_hi:
                break
            d = plen + i * bq - j * bk
            if d >= far:
                kinds.append(FAR)
            else:
                if d not in deltas:
                    deltas.append(d)
                kinds.append(('tile', deltas.index(d)))
        kinds_per_q.append(kinds)
    return kinds_per_q, deltas


def _diff_bias_tiles(rel_bias, deltas, bq, bk):
    r = jnp.arange(bq, dtype=jnp.int32)[:, None]
    c = jnp.arange(bk, dtype=jnp.int32)[None, :]
    far_bias = _bucket_bias(rel_bias, jnp.full((1, 1), -4 * REL_MAX_DIST, jnp.int32))
    tiles = []
    for d in deltas:
        bias = (_bucket_bias(rel_bias, c - r - d) - far_bias) * LOG2E
        vis = (c // CHUNK) <= ((r + d) // CHUNK)
        tiles.append(jnp.where(vis[None], bias, NEG_BIG))
    return jnp.stack(tiles, axis=0)


def _diff_attn_body(pairs_ref, sig_ref, q_ref, k_ref, v_ref, bias_ref, lam_ref, g_ref, o_ref,
                    sa1, sa2, sb1, sb2, m1, acc1, m2, acc2, *, bk, sigs, lam_init):
    i = pl.program_id(2)
    dv = A_V_DIM
    q = q_ref[0, 0] * (A_QK_DIM ** -0.5 * LOG2E)
    lane = lax.broadcasted_iota(jnp.int32, (1, 2 * A_QK_DIM), 1)
    q1 = jnp.where(lane < A_QK_DIM, q, 0.0).astype(BF16)
    q2 = jnp.where(lane >= A_QK_DIM, q, 0.0).astype(BF16)
    for m, acc in ((m1, acc1), (m2, acc2)):
        m[...] = jnp.full(m.shape, -jnp.inf, F32)
        acc[...] = jnp.zeros(acc.shape, F32)

    def rows(j):
        return pl.ds(j * bk if isinstance(j, int) else pl.multiple_of(j * bk, bk), bk)

    def qk(j):
        kb = k_ref[0, 0, rows(j), :]
        return _dot_nt(q1, kb), _dot_nt(q2, kb)

    def consume(tiles, j, kind):
        vb = v_ref[0, 0, rows(j), :]
        for s, m, acc in zip(tiles, (m1, m2), (acc1, acc2)):
            if kind != FAR:
                s = s + bias_ref[kind[1], 0]
            _softmax_block(s, vb, m, acc)

    _kv_sweep(pairs_ref[i], sig_ref[i], sigs, qk, consume, (sa1, sa2), (sb1, sb2))

    lp = lam_ref[...]
    lam = (jnp.exp(jnp.sum(lp[0:1] * lp[1:2], axis=-1, keepdims=True))
           - jnp.exp(jnp.sum(lp[2:3] * lp[3:4], axis=-1, keepdims=True)) + lam_init)
    a1 = acc1[...]
    a2 = acc2[...]
    o = a1[:, 0:dv] / a1[:, dv:dv + 1] - lam * (a2[:, 0:dv] / a2[:, dv:dv + 1])
    o = o * lax.rsqrt(jnp.mean(o * o, axis=-1, keepdims=True) + EPS) * g_ref[...]
    o_ref[0, 0] = o * (1.0 - lam_init)


def _diff_attn(q, k, v_aug, rel_bias, a_lambda, a_norm_g, plen, bq, bk, lam_init):
    b, h, sq, dq = q.shape
    sk = k.shape[2]
    dv = A_V_DIM
    nq, nk = sq // bq, sk // bk
    kinds, deltas = _diff_plan(nq, nk, plen, bq, bk)
    n_pairs, sig_id, sigs = _sweep_plan(kinds)
    tiles = _diff_bias_tiles(rel_bias, deltas, bq, bk)
    nt = tiles.shape[0]
    per_head = lambda shape: pl.BlockSpec((1, 1) + shape, lambda b_, h_, i, *_: (b_, h_, 0, 0))
    grid_spec = pltpu.PrefetchScalarGridSpec(
        num_scalar_prefetch=2,
        grid=(b, h, nq),
        in_specs=[pl.BlockSpec((1, 1, bq, dq), lambda b_, h_, i, *_: (b_, h_, i, 0)),
                  per_head((sk, dq)), per_head((sk, LANES)),
                  pl.BlockSpec((nt, 1, bq, bk), lambda b_, h_, i, *_: (0, h_, 0, 0)),
                  pl.BlockSpec((4, A_QK_DIM), lambda b_, h_, i, *_: (0, 0)),
                  pl.BlockSpec((1, dv), lambda b_, h_, i, *_: (0, 0))],
        out_specs=pl.BlockSpec((1, 1, bq, dv), lambda b_, h_, i, *_: (b_, h_, i, 0)),
        scratch_shapes=[pltpu.VMEM((bq, bk), F32)] * 4 + [pltpu.VMEM((bq, LANES), F32)] * 4,
    )
    return pl.pallas_call(
        functools.partial(_diff_attn_body, bk=bk, sigs=sigs, lam_init=lam_init),
        grid_spec=grid_spec,
        out_shape=jax.ShapeDtypeStruct((b, h, sq, dv), F32),
        compiler_params=_cparams(("parallel", "parallel", "arbitrary")),
        name="diff_attn",
    )(jnp.asarray(n_pairs), jnp.asarray(sig_id), q, k, v_aug, tiles, a_lambda, a_norm_g.reshape(1, dv))


MASK = 'mask'


def _forget_attn_body(pairs_ref, sig_ref, q_ref, k_ref, v_ref, ck_ref, cref_ref, o_ref, sa, sb, m, acc,
                      *, plen, bq, bk, sigs):
    i = pl.program_id(2)
    d = B_HEAD_DIM
    q = (q_ref[0, 0] * (d ** -0.5 * LOG2E)).astype(BF16)
    cref = cref_ref[0, 0, 0]
    m[...] = jnp.full(m.shape, -jnp.inf, F32)
    acc[...] = jnp.zeros(acc.shape, F32)

    def rows(j):
        return pl.ds(j * bk if isinstance(j, int) else pl.multiple_of(j * bk, bk), bk)

    def qk(j):
        e = (ck_ref[0, 0, pl.ds(j, 1), :] - cref) * LOG2E
        return (_dot_nt(q, k_ref[0, 0, rows(j), :]) - e,)

    def consume(tiles, j, kind):
        s = tiles[0]
        if kind == MASK:
            kpos = j * bk + lax.broadcasted_iota(jnp.int32, (bq, bk), 1)
            qpos = plen + i * bq + lax.broadcasted_iota(jnp.int32, (bq, bk), 0)
            s = jnp.where(kpos <= qpos, s, NEG_BIG)
        _softmax_block(s, v_ref[0, 0, rows(j), :], m, acc)

    _kv_sweep(pairs_ref[i], sig_ref[i], sigs, qk, consume, (sa,), (sb,))
    a = acc[...]
    o_ref[0, 0] = a[:, 0:d] / a[:, d:d + 1]


def _forget_attn(q, k, v_aug, cum, plen, bq, bk):
    b, h, sq, d = q.shape
    sk = k.shape[2]
    nq, nk = sq // bq, sk // bk
    kinds = []
    for i in range(nq):
        q_lo, q_hi = plen + i * bq, plen + i * bq + bq - 1
        kinds.append([FAR if j * bk + bk - 1 <= q_lo else MASK for j in range(nk) if j * bk <= q_hi])
    n_pairs, sig_id, sigs = _sweep_plan(kinds)
    cref = cum[:, :, plen:plen + sq:bq].reshape(b, h, nq, 1, 1)
    ck = cum.reshape(b, h, nk, bk)
    per_head = lambda shape: pl.BlockSpec((1, 1) + shape, lambda b_, h_, i, *_: (b_, h_, 0, 0))
    grid_spec = pltpu.PrefetchScalarGridSpec(
        num_scalar_prefetch=2,
        grid=(b, h, nq),
        in_specs=[pl.BlockSpec((1, 1, bq, d), lambda b_, h_, i, *_: (b_, h_, i, 0)),
                  per_head((sk, d)), per_head((sk, LANES)), per_head((nk, bk)),
                  pl.BlockSpec((1, 1, 1, 1, 1), lambda b_, h_, i, *_: (b_, h_, i, 0, 0))],
        out_specs=pl.BlockSpec((1, 1, bq, d), lambda b_, h_, i, *_: (b_, h_, i, 0)),
        scratch_shapes=[pltpu.VMEM((bq, bk), F32)] * 2 + [pltpu.VMEM((bq, LANES), F32)] * 2,
    )
    return pl.pallas_call(
        functools.partial(_forget_attn_body, plen=plen, bq=bq, bk=bk, sigs=sigs),
        grid_spec=grid_spec,
        out_shape=jax.ShapeDtypeStruct((b, h, sq, d), F32),
        compiler_params=_cparams(("parallel", "parallel", "arbitrary")),
        name="forget_attn",
    )(jnp.asarray(n_pairs), jnp.asarray(sig_id), q, k, v_aug, ck, cref)


_CONV_PAD = 8


def _causal_conv_tile(buf, x, cw_ref, rows):
    buf[_CONV_PAD:_CONV_PAD + rows, :] = x
    lo = _CONV_PAD - (CONV_W - 1)
    y = buf[lo:lo + rows, :] * cw_ref[0:1, :]
    for j in range(1, CONV_W):
        y = y + buf[lo + j:lo + j + rows, :] * cw_ref[j:j + 1, :]
    buf[lo:_CONV_PAD, :] = x[rows - (CONV_W - 1):rows, :]
    return y


def _rg_lru_body(x_ref, gate_ref, h0_ref, conv0_ref, cw_ref, cb_ref, wa_ref, ba_ref, wx_ref, bx_ref, lam_ref,
                 y_ref, hn_ref, convn_ref, buf, hcar, *, ts):
    n = pl.program_id(1)

    @pl.when(n == 0)
    def _():
        buf[_CONV_PAD - (CONV_W - 1):_CONV_PAD, :] = conv0_ref[0]
        hcar[...] = h0_ref[0]

    x = x_ref[0]
    xc = _causal_conv_tile(buf, x, cw_ref, ts) + cb_ref[...]
    r = jax.nn.sigmoid(_dot(xc, wa_ref[...], HI) + ba_ref[...])
    i = jax.nn.sigmoid(_dot(xc, wx_ref[...], HI) + bx_ref[...])
    log_a = -C_POWER * r * jax.nn.softplus(-lam_ref[...])
    a = jnp.exp(log_a)
    th = jnp.tanh(log_a)
    b = jnp.sqrt(-2.0 * th / (1.0 - th)) * i * xc
    row = lax.broadcasted_iota(jnp.int32, a.shape, 0)
    d = 1
    while d < ts:
        keep = row >= d
        a_sh = jnp.where(keep, pltpu.roll(a, d, 0), 1.0)
        b_sh = jnp.where(keep, pltpu.roll(b, d, 0), 0.0)
        b = a * b_sh + b
        a = a * a_sh
        d *= 2
    h = a * hcar[...] + b
    y_ref[0] = h * jax.nn.gelu(gate_ref[0])
    hcar[...] = h[ts - 1:ts, :]

    @pl.when(n == pl.num_programs(1) - 1)
    def _():
        hn_ref[0] = h[ts - 1:ts, :]
        convn_ref[0] = x[ts - (CONV_W - 1):ts, :]


def _block_diag(w):
    n, d, _ = w.shape
    eye = jnp.eye(n, dtype=w.dtype)
    return (eye[:, None, :, None] * w[:, :, None, :]).reshape(n * d, n * d)


def _rg_lru(x, gate, h0, conv0, p, ts):
    b, s, w = x.shape
    row = lambda a: a.reshape(1, w)
    tile = pl.BlockSpec((1, ts, w), lambda i, n: (i, n, 0))
    const = lambda shape: pl.BlockSpec(shape, lambda i, n: (0,) * len(shape))
    return pl.pallas_call(
        functools.partial(_rg_lru_body, ts=ts),
        grid=(b, s // ts),
        in_specs=[tile, tile,
                  pl.BlockSpec((1, 1, w), lambda i, n: (i, 0, 0)),
                  pl.BlockSpec((1, CONV_W - 1, w), lambda i, n: (i, 0, 0)),
                  const((CONV_W, w)), const((1, w)), const((w, w)), const((1, w)), const((w, w)), const((1, w)),
                  const((1, w))],
        out_specs=[tile,
                   pl.BlockSpec((1, 1, w), lambda i, n: (i, 0, 0)),
                   pl.BlockSpec((1, CONV_W - 1, w), lambda i, n: (i, 0, 0))],
        out_shape=[jax.ShapeDtypeStruct((b, s, w), F32), jax.ShapeDtypeStruct((b, 1, w), F32),
                   jax.ShapeDtypeStruct((b, CONV_W - 1, w), F32)],
        scratch_shapes=[pltpu.VMEM((_CONV_PAD + ts, w), F32), pltpu.VMEM((1, w), F32)],
        compiler_params=_cparams(("parallel", "arbitrary")),
        name="rg_lru",
    )(x, gate, h0.reshape(b, 1, w), conv0, p['c_conv_w'], row(p['c_conv_b']),
      _block_diag(p['c_gate_a_w']), row(p['c_gate_a_b']), _block_diag(p['c_gate_x_w']), row(p['c_gate_x_b']),
      row(p['c_lambda']))


def _gdn_body(qkv_ref, z_ref, bl_ref, al_ref, s0_ref, conv0_ref, cw_ref, alog_ref, dtb_ref, ng_ref,
              o_ref, sn_ref, convn_ref, buf, st, *, c, tail):
    n = pl.program_id(1)
    w = D_WIDTH
    hd = D_HEAD_DIM

    @pl.when(n == 0)
    def _():
        buf[_CONV_PAD - (CONV_W - 1):_CONV_PAD, :] = conv0_ref[0]
        st[...] = s0_ref[0]

    x = qkv_ref[0]
    y = jax.nn.silu(_causal_conv_tile(buf, x, cw_ref, c))
    q, k, v = y[:, 0:w], y[:, w:2 * w], y[:, 2 * w:3 * w]

    same_head = (lax.broadcasted_iota(jnp.int32, (w, w), 0) // hd) == (lax.broadcasted_iota(jnp.int32, (w, w), 1) // hd)
    head_sum = same_head.astype(BF16)

    def head_sums(x2):
        hi, lo = _split_bf16(x2)
        return _dot(hi, head_sum) + _dot(lo, head_sum)

    def bd(a):
        return jnp.where(same_head, jnp.concatenate([a] * D_HEADS, axis=0), 0.0)

    def dot3_bd(a, b):
        return _dot3(a, bd(b))

    q = q * lax.rsqrt(head_sums(q * q) + EPS) * (hd ** -0.5)
    k = k * lax.rsqrt(head_sums(k * k) + EPS)
    beta = jax.nn.sigmoid(bl_ref[0])
    g = -jnp.exp(alog_ref[...]) * jax.nn.softplus(al_ref[0] + dtb_ref[...])
    if tail < c:
        real = lax.broadcasted_iota(jnp.int32, (c, w), 0) < jnp.where(n == pl.num_programs(1) - 1, tail, c)
        beta = jnp.where(real, beta, 0.0)
        g = jnp.where(real, g, 0.0)

    ti = lax.broadcasted_iota(jnp.int32, (c, w), 0)
    tj = lax.broadcasted_iota(jnp.int32, (c, w), 1) % hd
    incl = ti >= tj
    strict = ti > tj
    diag = ti == tj
    ci = lax.broadcasted_iota(jnp.int32, (c, c), 0)
    cj = lax.broadcasted_iota(jnp.int32, (c, c), 1)
    gcum = _dot_exact_left((ci >= cj).astype(BF16), g)
    g_row = _dot_exact_left(jnp.ones((c, c), BF16), jnp.where(diag, gcum, 0.0))
    decay = jnp.where(incl, jnp.exp(gcum - g_row), 0.0)
    eg = jnp.exp(gcum)
    vb = v * beta
    kb = k * beta
    kbe = kb * eg
    q_g = q * eg
    g_last = gcum[c - 1:c, :]
    k_d = k * jnp.exp(g_last - gcum)

    k_t = jnp.where(same_head, _dot_tn(k.astype(BF16), diag.astype(BF16)), 0.0).astype(BF16)
    gram = _dot(jnp.concatenate([kb, q], axis=0).astype(BF16), k_t)
    low = jnp.where(strict, gram[0:c] * decay, 0.0)
    attn = gram[c:2 * c] * decay
    inv = diag.astype(F32) - low
    pw = low
    span = 2
    while span < c:
        pw = dot3_bd(pw, pw)
        inv = inv + dot3_bd(inv, pw)
        span *= 2
    v_w = dot3_bd(inv, vb)
    k_w = dot3_bd(inv, kbe)

    state = st[...]
    through = _dot3(jnp.concatenate([k_w, q_g], axis=0), state)
    v_new = v_w - through[0:c]
    o = through[c:2 * c] + dot3_bd(attn, v_new)
    kd_hi, kd_lo = _split_bf16(k_d)
    vn_hi, vn_lo = _split_bf16(v_new)
    outer = _dot_tn(kd_hi, vn_hi) + (_dot_tn(kd_hi, vn_lo) + _dot_tn(kd_lo, vn_hi))
    state = state * jnp.exp(g_last) + jnp.where(same_head, outer, 0.0)
    st[...] = state

    o = o * lax.rsqrt(head_sums(o * o) * (1.0 / hd) + EPS) * ng_ref[...]
    o_ref[0] = o * jax.nn.silu(z_ref[0])

    @pl.when(n == pl.num_programs(1) - 1)
    def _():
        sn_ref[0] = state
        convn_ref[0] = x[tail - (CONV_W - 1):tail, :]


def _gdn(qkv, z, beta_logit, alpha_logit, s0, conv0, p):
    b, s_true, w3 = qkv.shape
    w = D_WIDTH
    c = D_HEAD_DIM
    s = -(-s_true // c) * c
    tail = s_true - (s - c)
    assert tail >= CONV_W - 1
    if s != s_true:
        pad = lambda a: jnp.pad(a, ((0, 0), (0, s - s_true), (0, 0)))
        qkv, z, beta_logit, alpha_logit = pad(qkv), pad(z), pad(beta_logit), pad(alpha_logit)
    s0_bd = jax.vmap(_block_diag)(s0)
    per_lane = lambda a: jnp.repeat(a, D_HEAD_DIM).reshape(1, w)
    tile = lambda n_: pl.BlockSpec((1, c, n_), lambda i, n: (i, n, 0))
    const = lambda shape: pl.BlockSpec(shape, lambda i, n: (0,) * len(shape))
    per_b = lambda shape: pl.BlockSpec((1,) + shape, lambda i, n: (i,) + (0,) * len(shape))
    o, sn, convn = pl.pallas_call(
        functools.partial(_gdn_body, c=c, tail=tail),
        grid=(b, s // c),
        in_specs=[tile(w3), tile(w), tile(w), tile(w), per_b((w, w)), per_b((CONV_W - 1, w3)),
                  const((CONV_W, w3)), const((1, w)), const((1, w)), const((1, w))],
        out_specs=[tile(w), per_b((w, w)), per_b((CONV_W - 1, w3))],
        out_shape=[jax.ShapeDtypeStruct((b, s, w), F32), jax.ShapeDtypeStruct((b, w, w), F32),
                   jax.ShapeDtypeStruct((b, CONV_W - 1, w3), F32)],
        scratch_shapes=[pltpu.VMEM((_CONV_PAD + c, w3), F32), pltpu.VMEM((w, w), F32)],
        compiler_params=_cparams(("parallel", "arbitrary")),
        name="gdn",
    )(qkv, z, beta_logit, alpha_logit, s0_bd, conv0, p['d_conv_w'], per_lane(p['d_a_log']), per_lane(p['d_dt_bias']),
      jnp.tile(p['d_norm_g'], D_HEADS).reshape(1, w))
    hd = D_HEAD_DIM
    sn = jnp.stack([sn[:, h * hd:(h + 1) * hd, h * hd:(h + 1) * hd] for h in range(D_HEADS)], axis=1)
    return o[:, :s_true], sn, convn


_PEER_SLABS = 2 * PEER_HEADS


def _out_proj_body(x_ref, oa_ref, ob_ref, oc_ref, od_ref, wo_ref, g_ref, wq_ref, sk_ref,
                   x2_ref, h2_ref, st_ref):
    x2 = x_ref[...]
    for grp, o_ref in enumerate((oa_ref, ob_ref, oc_ref, od_ref)):
        x2 = x2 + _dot(o_ref[...].astype(BF16), wo_ref[grp * GROUP_WIDTH:(grp + 1) * GROUP_WIDTH, :])
    x2_ref[...] = x2
    hb = (x2 * lax.rsqrt(jnp.mean(x2 * x2, axis=-1, keepdims=True) + EPS) * g_ref[...]).astype(BF16)
    h2_ref[...] = hb
    q = _dot(hb, wq_ref[...]).astype(BF16)
    for slab in range(_PEER_SLABS):
        st_ref[slab] = _dot_nt(sk_ref[slab % 2], q[:, slab * PEER_HALF:(slab + 1) * PEER_HALF])


def _out_proj(x, o_a, o_b, o_c, o_d, w_out, norm2_g, wq, subkeys, tm):
    t = x.shape[0]
    tile = lambda n: pl.BlockSpec((tm, n), lambda i: (i, 0))
    const = lambda shape: pl.BlockSpec(shape, lambda i: (0,) * len(shape))
    nq = PEER_HEADS * PEER_QUERY_DIM
    return pl.pallas_call(
        _out_proj_body,
        grid=(t // tm,),
        in_specs=[tile(D_MODEL), tile(GROUP_WIDTH), tile(GROUP_WIDTH), tile(GROUP_WIDTH), tile(GROUP_WIDTH),
                  const((D_MODEL, D_MODEL)), const((1, D_MODEL)), const((D_MODEL, nq)),
                  const((2, PEER_N_KEYS, PEER_HALF))],
        out_specs=[tile(D_MODEL), tile(D_MODEL),
                   pl.BlockSpec((_PEER_SLABS, PEER_N_KEYS, tm), lambda i: (0, 0, i))],
        out_shape=[jax.ShapeDtypeStruct((t, D_MODEL), F32), jax.ShapeDtypeStruct((t, D_MODEL), BF16),
                   jax.ShapeDtypeStruct((_PEER_SLABS, PEER_N_KEYS, t), F32)],
        compiler_params=_cparams(("parallel",)),
        name="out_proj",
    )(x, o_a, o_b, o_c, o_d, w_out, norm2_g.reshape(1, D_MODEL), wq, subkeys)


_NO_RANK = 64.0


def _top_values(x, count, with_rank=False):
    vals = []
    rank = jnp.full(x.shape, _NO_RANK, F32) if with_rank else None
    for k in range(count):
        m = jnp.max(x, axis=0, keepdims=True)
        vals.append(m)
        hit = x == m
        if with_rank:
            rank = jnp.where(hit, float(k), rank)
        x = jnp.where(hit, -jnp.inf, x)
    return vals, x, rank


def _peer_select_body(s_ref, c_ref, cnt_ref, r2_ref, e2_ref):
    s1 = s_ref[0]
    s2 = s_ref[1]
    n = s1.shape[1]
    v1, rest1, _ = _top_values(s1, PEER_TOPK)
    v2, _, rank2 = _top_values(s2, PEER_TOPK, with_rank=True)
    sel1 = rest1 == -jnp.inf
    v1_all = jnp.concatenate(v1, axis=0)
    v2_all = jnp.concatenate(v2, axis=0)
    row8 = lax.broadcasted_iota(jnp.int32, (8, n), 0)
    cand = [v1[0] + v2_all, v1[1] + v2_all[0:8]]
    for i in range(2, 8):
        cand.append(jnp.where(row8 < PEER_TOPK // (i + 1), v1[i] + v2_all[0:8], -jnp.inf))
    cand.append(v1_all[8:16] + v2[0])
    top, _, _ = _top_values(jnp.concatenate(cand, axis=0), PEER_TOPK)
    tau = top[PEER_TOPK - 1]
    z = jnp.ones_like(tau)
    for t in top[1:]:
        z = z + jnp.exp(t - top[0])
    count_sorted = jnp.zeros((PEER_TOPK, n), F32)
    for j in range(PEER_TOPK):
        count_sorted = count_sorted + jnp.where(v1_all + v2[j] >= tau, 1.0, 0.0)
    count = jnp.zeros(s1.shape, F32)
    for i in range(PEER_TOPK):
        count = jnp.where(s1 == v1[i], count_sorted[i:i + 1, :], count)
    cnt_ref[0] = count
    c_ref[0] = jnp.where(sel1, jnp.exp(s1 - v1[0]), 0.0) / z
    r2_ref[0] = rank2.astype(BF16)
    e2_ref[0] = jnp.where(rank2 < _NO_RANK, jnp.exp(s2 - v2[0]), 0.0).astype(BF16)


def _peer_select(st, tm):
    t = st.shape[2]
    out_spec = pl.BlockSpec((1, PEER_N_KEYS, tm), lambda i, h: (h, 0, i))
    f32 = jax.ShapeDtypeStruct((PEER_HEADS, PEER_N_KEYS, t), F32)
    bf16 = jax.ShapeDtypeStruct((PEER_HEADS, PEER_N_KEYS, t), BF16)
    return pl.pallas_call(
        _peer_select_body,
        grid=(t // tm, PEER_HEADS),
        in_specs=[pl.BlockSpec((2, PEER_N_KEYS, tm), lambda i, h: (h, 0, i))],
        out_specs=[out_spec] * 4,
        out_shape=[f32, f32, bf16, bf16],
        compiler_params=_cparams(("parallel", "parallel")),
        name="peer_select",
    )(st)


_PEER_GROUP = 4


def _peer_dense_body(ht_ref, u_ref, vt_ref, c_ref, cnt_ref, r2_ref, e2_ref, o_ref, act, pbuf, *, na):
    e = pl.program_id(1)
    ht = ht_ref[...]
    tm = ht.shape[1]
    rows_g = _PEER_GROUP * PEER_N_KEYS
    groups = na // _PEER_GROUP
    packed_rows = 16

    def span(g):
        return slice(g * rows_g, (g + 1) * rows_g)

    def activations(g):
        act[span(g), :] = _dot(u_ref[span(g), :], ht)

    def over_keys(row):
        packed = jnp.broadcast_to(row, (packed_rows, tm)).astype(BF16)
        return jnp.concatenate([packed] * (PEER_N_KEYS // packed_rows), axis=0)

    def weights(g):
        for a in range(g * _PEER_GROUP, (g + 1) * _PEER_GROUP):
            rows = slice(a * PEER_N_KEYS, (a + 1) * PEER_N_KEYS)
            w = None
            for h in range(PEER_HEADS):
                chosen = r2_ref[h] < over_keys(cnt_ref[h, a:a + 1, :])
                term = jnp.where(chosen, e2_ref[h], 0.0) * over_keys(c_ref[h, a:a + 1, :])
                w = term if w is None else w + term
            pbuf[rows, :] = w * jax.nn.gelu(act[rows, :].astype(BF16))

    @pl.when(e == 0)
    def _():
        o_ref[...] = jnp.zeros(o_ref.shape, F32)

    activations(0)
    contrib = None
    for g in range(groups):
        if g + 1 < groups:
            activations(g + 1)
        weights(g)
        part = _dot(vt_ref[:, span(g)], pbuf[span(g), :])
        contrib = part if contrib is None else contrib + part
    o_ref[...] += contrib


def _peer_dense(ht, u, vt, c, count, rank2, e2, tm, eb):
    t = ht.shape[1]
    na = eb // PEER_N_KEYS
    key_rows = pl.BlockSpec((PEER_HEADS, na, tm), lambda i, e: (0, e, i))
    all_keys = pl.BlockSpec((PEER_HEADS, PEER_N_KEYS, tm), lambda i, e: (0, 0, i))
    return pl.pallas_call(
        functools.partial(_peer_dense_body, na=na),
        grid=(t // tm, PEER_EXPERTS // eb),
        in_specs=[pl.BlockSpec((D_MODEL, tm), lambda i, e: (0, i)),
                  pl.BlockSpec((eb, D_MODEL), lambda i, e: (e, 0)),
                  pl.BlockSpec((D_MODEL, eb), lambda i, e: (0, e)),
                  key_rows, key_rows, all_keys, all_keys],
        out_specs=pl.BlockSpec((D_MODEL, tm), lambda i, e: (0, i)),
        out_shape=jax.ShapeDtypeStruct((D_MODEL, t), F32),
        scratch_shapes=[pltpu.VMEM((eb, tm), F32), pltpu.VMEM((eb, tm), BF16)],
        compiler_params=_cparams(("parallel", "arbitrary")),
        name="peer_dense",
    )(ht, u, vt, c, count, rank2, e2)


def _residual_body(x_ref, pt_ref, g_ref, o_ref, *, final_norm):
    x = x_ref[...] + pt_ref[...].T
    if final_norm:
        x = x * lax.rsqrt(jnp.mean(x * x, axis=-1, keepdims=True) + EPS) * g_ref[...]
    o_ref[...] = x


def _residual(x, pt, g, tm, final_norm):
    t = x.shape[0]
    return pl.pallas_call(
        functools.partial(_residual_body, final_norm=final_norm),
        grid=(t // tm,),
        in_specs=[pl.BlockSpec((tm, D_MODEL), lambda i: (i, 0)),
                  pl.BlockSpec((D_MODEL, tm), lambda i: (0, i)),
                  pl.BlockSpec((1, D_MODEL), lambda i: (0, 0))],
        out_specs=pl.BlockSpec((tm, D_MODEL), lambda i: (i, 0)),
        out_shape=jax.ShapeDtypeStruct((t, D_MODEL), F32),
        compiler_params=_cparams(("parallel",)),
        name="residual",
    )(x, pt, g.reshape(1, D_MODEL))


def _heads_first(a):
    return jnp.transpose(a, (0, 2, 1, 3))


def _prep_layer_weights(w):
    return {
        'w_in': _permute_w_in(w['w_in']),
        'w_out': w['w_out'].astype(BF16),
        'wq': w['peer_wq'].astype(BF16),
        'u': w['peer_u'].astype(BF16),
        'vt': w['peer_v'].astype(BF16).T,
    }


def _mixers(x2d, bsz, s, past, p, pw, lam_init, cfg):
    pa_k, pa_v, pb_k, pb_v, pb_logf, c_h0, c_conv0, d_s0, d_conv0 = past
    plen = pa_k.shape[1]
    sk = plen + s
    (a_q, a_k, a_v, b_q, b_k, b_v, c_x, c_g, d_qkv, d_z, d_beta, d_alpha, small) = _in_proj(
        x2d, p['norm1_g'], pw['w_in'], cfg['tm_in'])
    bq, bk = cfg['bq'], cfg['bk']

    def seq(a, *tail):
        return a.reshape(bsz, s, *tail)

    k_all = jnp.concatenate([pa_k, seq(a_k, A_HEADS, 2 * A_QK_DIM)], axis=1)
    v_all = jnp.concatenate([pa_v, seq(a_v, A_HEADS, A_V_DIM)], axis=1)
    o_a = _diff_attn(_heads_first(seq(a_q, A_HEADS, 2 * A_QK_DIM)), _heads_first(k_all).astype(BF16),
                     _with_ones(_heads_first(v_all)),
                     p['rel_bias'], p['a_lambda'], p['a_norm_g'], plen, bq, bk, lam_init)
    o_a = _heads_first(o_a).reshape(bsz * s, GROUP_WIDTH)

    rows = -(-sk // LANES)
    rows = -(-rows // 8) * 8
    f_vals = jnp.concatenate([pb_logf, seq(small[:, 0:B_HEADS], B_HEADS)], axis=1)
    f_vals = jnp.pad(f_vals, ((0, 0), (0, rows * LANES - sk), (0, 0)))
    f_vals = jnp.transpose(f_vals, (0, 2, 1)).reshape(bsz, B_HEADS, rows, LANES)
    logf, cum = _logf_cumsum(f_vals, p['b_forget_bias'], plen)
    cum = cum.reshape(bsz, B_HEADS, rows * LANES)[:, :, :sk]
    b_logf = jnp.transpose(logf.reshape(bsz, B_HEADS, rows * LANES)[:, :, plen:sk], (0, 2, 1))
    kb_all = jnp.concatenate([pb_k, seq(b_k, B_HEADS, B_HEAD_DIM)], axis=1)
    vb_all = jnp.concatenate([pb_v, seq(b_v, B_HEADS, B_HEAD_DIM)], axis=1)
    o_b = _forget_attn(_heads_first(seq(b_q, B_HEADS, B_HEAD_DIM)), _heads_first(kb_all).astype(BF16),
                       _with_ones(_heads_first(vb_all)), cum, plen, bq, bk)
    o_b = _heads_first(o_b).reshape(bsz * s, GROUP_WIDTH)

    o_c, c_h, c_conv = _rg_lru(seq(c_x, C_WIDTH), seq(c_g, C_WIDTH), c_h0, c_conv0, p, cfg['ts'])

    o_d, d_s, d_conv = _gdn(seq(d_qkv, 3 * D_WIDTH), seq(d_z, D_WIDTH), seq(d_beta, D_WIDTH), seq(d_alpha, D_WIDTH),
                            d_s0, d_conv0, p)

    state = (seq(a_k, A_HEADS, 2 * A_QK_DIM), seq(a_v, A_HEADS, A_V_DIM), seq(b_k, B_HEADS, B_HEAD_DIM),
             seq(b_v, B_HEADS, B_HEAD_DIM), b_logf, c_h.reshape(bsz, C_WIDTH), c_conv, d_s, d_conv)
    return (o_a, o_b, o_c.reshape(bsz * s, C_WIDTH), o_d.reshape(bsz * s, D_WIDTH)), state


def _layer(x2d, bsz, s, past, p, pw, lam_init, cfg):
    mix, state = _mixers(x2d, bsz, s, past, p, pw, lam_init, cfg)
    x2, h2, st = _out_proj(x2d, *mix, pw['w_out'], p['norm2_g'], pw['wq'], p['peer_subkeys'].astype(BF16),
                           cfg['tm_out'])
    c, count, rank2, e2 = _peer_select(st, cfg['tm_sel'])
    peer_t = _peer_dense(h2.T, pw['u'], pw['vt'], c, count, rank2, e2, cfg['tm_peer'], cfg['eb'])
    return x2, peer_t, state


_PROMPT_CFG = dict(tm_in=512, bq=1024, bk=512, ts=256, tm_out=256, tm_sel=512, tm_peer=512, eb=2048, tm_res=256)


def _sample_cfg(s, sk):
    return dict(tm_in=512, bq=s, bk=sk, ts=s, tm_out=256, tm_sel=512, tm_peer=512, eb=2048, tm_res=256)


def kernel(x_prompt, x_sample, cache_a_k, cache_a_v, cache_b_k, cache_b_v, cache_b_logf, state_c_h, state_c_conv, state_d_s, state_d_conv, norm1_g, norm2_g, final_norm_g, w_in, w_out, rel_bias, a_lambda, a_norm_g, b_forget_bias, c_conv_w, c_conv_b, c_gate_a_w, c_gate_a_b, c_gate_x_w, c_gate_x_b, c_lambda, d_conv_w, d_a_log, d_dt_bias, d_norm_g, peer_wq, peer_subkeys, peer_u, peer_v):
    pb, ps, _ = x_prompt.shape
    sb, ss, _ = x_sample.shape
    dt = x_prompt.dtype
    xp = x_prompt.reshape(pb * ps, D_MODEL)
    xs = x_sample.reshape(sb * ss, D_MODEL)
    scfg = _sample_cfg(ss, cache_a_k.shape[2] + ss)
    prompt_out, sample_out = [], []
    for l in range(DEPTH):
        p = {
            'norm1_g': norm1_g[l], 'norm2_g': norm2_g[l], 'rel_bias': rel_bias, 'a_lambda': a_lambda[l],
            'a_norm_g': a_norm_g[l], 'b_forget_bias': b_forget_bias[l],
            'c_conv_w': c_conv_w[l], 'c_conv_b': c_conv_b[l],
            'c_gate_a_w': c_gate_a_w[l], 'c_gate_a_b': c_gate_a_b[l],
            'c_gate_x_w': c_gate_x_w[l], 'c_gate_x_b': c_gate_x_b[l], 'c_lambda': c_lambda[l],
            'd_conv_w': d_conv_w[l], 'd_a_log': d_a_log[l], 'd_dt_bias': d_dt_bias[l], 'd_norm_g': d_norm_g[l],
            'peer_subkeys': peer_subkeys[l],
        }
        pw = _prep_layer_weights({'w_in': w_in[l], 'w_out': w_out[l], 'peer_wq': peer_wq[l],
                                  'peer_u': peer_u[l], 'peer_v': peer_v[l]})
        lam_init = 0.8 - 0.6 * math.exp(-0.3 * l)
        empty = (
            jnp.zeros((pb, 0, A_HEADS, 2 * A_QK_DIM), dt), jnp.zeros((pb, 0, A_HEADS, A_V_DIM), dt),
            jnp.zeros((pb, 0, B_HEADS, B_HEAD_DIM), dt), jnp.zeros((pb, 0, B_HEADS, B_HEAD_DIM), dt),
            jnp.zeros((pb, 0, B_HEADS), dt),
            jnp.zeros((pb, C_WIDTH), dt), jnp.zeros((pb, CONV_W - 1, C_WIDTH), dt),
            jnp.zeros((pb, D_HEADS, D_HEAD_DIM, D_HEAD_DIM), dt), jnp.zeros((pb, CONV_W - 1, 3 * D_WIDTH), dt),
        )
        last = l == DEPTH - 1
        xp2, peer_p, st_p = _layer(xp, pb, ps, empty, p, pw, lam_init, _PROMPT_CFG)
        xp = _residual(xp2, peer_p, final_norm_g, _PROMPT_CFG['tm_res'], last)
        prompt_out.append(st_p)
        past = (cache_a_k[l], cache_a_v[l], cache_b_k[l], cache_b_v[l], cache_b_logf[l],
                state_c_h[l], state_c_conv[l], state_d_s[l], state_d_conv[l])
        xs2, peer_s, st_s = _layer(xs, sb, ss, past, p, pw, lam_init, scfg)
        xs = _residual(xs2, peer_s, final_norm_g, scfg['tm_res'], last)
        sample_out.append(st_s)

    y_prompt = xp.reshape(pb, ps, D_MODEL)
    y_sample = xs.reshape(sb, ss, D_MODEL)
    p_states = [jnp.stack(z, axis=0) for z in zip(*prompt_out)]
    s_states = [jnp.stack(z, axis=0) for z in zip(*sample_out)]
    return (y_prompt, y_sample, *p_states, *s_states)
```

```python
import functools
import math

import numpy as np
import jax
import jax.numpy as jnp
from jax import lax
from jax.experimental import pallas as pl
from jax.experimental.pallas import tpu as pltpu

F32 = jnp.float32
BF16 = jnp.bfloat16
HI = lax.Precision.HIGHEST

D_MODEL = 1024
DEPTH = 2
CHUNK = 64
EPS = 1e-6
CONV_W = 4
GROUP_WIDTH = D_MODEL // 4
A_HEADS = 4
A_QK_DIM = GROUP_WIDTH // (2 * A_HEADS)
A_V_DIM = GROUP_WIDTH // A_HEADS
NUM_BUCKETS = 32
REL_MAX_DIST = 256
B_HEADS = 4
B_HEAD_DIM = GROUP_WIDTH // B_HEADS
C_WIDTH = GROUP_WIDTH
C_BLOCKS = 4
C_BLOCK_DIM = C_WIDTH // C_BLOCKS
C_POWER = 8.0
D_HEADS = 4
D_HEAD_DIM = GROUP_WIDTH // D_HEADS
D_WIDTH = GROUP_WIDTH
PEER_HEADS = 8
PEER_N_KEYS = 128
PEER_EXPERTS = PEER_N_KEYS * PEER_N_KEYS
PEER_TOPK = 16
PEER_QUERY_DIM = 256
PEER_HALF = PEER_QUERY_DIM // 2

LANES = 128
NEG_BIG = -1e30
VMEM_LIMIT = 56 * 1024 * 1024


def _cparams(sem):
    return pltpu.CompilerParams(dimension_semantics=sem, vmem_limit_bytes=VMEM_LIMIT)


def _dot(a, b, precision=None):
    return jnp.dot(a, b, preferred_element_type=F32, precision=precision)


def _dot_nt(a, b, precision=None):
    return lax.dot_general(a, b, (((1,), (1,)), ((), ())), preferred_element_type=F32, precision=precision)


def _dot_tn(a, b, precision=None):
    return lax.dot_general(a, b, (((0,), (0,)), ((), ())), preferred_element_type=F32, precision=precision)


def _split_bf16(x):
    hi = x.astype(BF16)
    return hi, (x - hi.astype(F32)).astype(BF16)


def _dot3(a, b):
    ah, al = _split_bf16(a)
    bh, bl = _split_bf16(b)
    return _dot(ah, bh) + (_dot(ah, bl) + _dot(al, bh))


def _dot_exact_left(a, b):
    b1 = b.astype(BF16)
    r1 = b - b1.astype(F32)
    b2 = r1.astype(BF16)
    b3 = (r1 - b2.astype(F32)).astype(BF16)
    return _dot(a, b1) + (_dot(a, b2) + _dot(a, b3))


_IN_WIDTHS = (256,) * 8 + (768, 256, 256, 256, LANES)
_IN_TOTAL = sum(_IN_WIDTHS)


def _permute_w_in(w):
    pad = jnp.zeros((w.shape[0], LANES - B_HEADS), w.dtype)
    per_lane = lambda cols: jnp.repeat(cols, D_HEAD_DIM, axis=1)
    return jnp.concatenate([w[:, 0:1536], w[:, 1540:3076], per_lane(w[:, 3076:3080]), per_lane(w[:, 3080:3084]),
                            w[:, 1536:1540], pad], axis=1).astype(BF16)


_ATTN_HEADS = 4
_ATTN_DIM = GROUP_WIDTH // _ATTN_HEADS


def _in_proj_body(x_ref, g_ref, w_ref, *outs):
    x = x_ref[...]
    tm = x.shape[0]
    h = x * lax.rsqrt(jnp.mean(x * x, axis=-1, keepdims=True) + EPS) * g_ref[...]
    hb = h.astype(BF16)
    lane = lax.broadcasted_iota(jnp.int32, (tm, _ATTN_DIM), 1)
    ones_col = jnp.where(lane == 0, 1.0, 0.0)

    def per_head(z, hd_):
        return z[:, hd_ * _ATTN_DIM:(hd_ + 1) * _ATTN_DIM]

    off = 0
    for grp in range(2):
        q_hm, k_ref, k_hm, v_ref, v_hm = outs[5 * grp:5 * grp + 5]
        q = _dot(hb, w_ref[:, off:off + GROUP_WIDTH])
        k = _dot(hb, w_ref[:, off + GROUP_WIDTH:off + 2 * GROUP_WIDTH])
        v = _dot(hb, w_ref[:, off + 2 * GROUP_WIDTH:off + 3 * GROUP_WIDTH])
        off += 3 * GROUP_WIDTH
        k_ref[...] = k
        v_ref[...] = v
        for hd_ in range(_ATTN_HEADS):
            q_hm[hd_] = per_head(q, hd_)
            k_hm[hd_] = per_head(k, hd_).astype(BF16)
            v_hm[hd_] = jnp.concatenate([per_head(v, hd_), ones_col], axis=1).astype(BF16)
    for o_ref, n in zip(outs[10:], _IN_WIDTHS[6:]):
        o_ref[...] = _dot(hb, w_ref[:, off:off + n])
        off += n


def _in_proj(x, g, w_perm, tm):
    t = x.shape[0]
    flat = lambda n, dt: (pl.BlockSpec((tm, n), lambda i: (i, 0)), jax.ShapeDtypeStruct((t, n), dt))
    head_major = lambda n, dt: (pl.BlockSpec((_ATTN_HEADS, tm, n), lambda i: (0, i, 0)),
                                jax.ShapeDtypeStruct((_ATTN_HEADS, t, n), dt))
    group = [head_major(_ATTN_DIM, F32), flat(GROUP_WIDTH, F32), head_major(_ATTN_DIM, BF16),
             flat(GROUP_WIDTH, F32), head_major(LANES, BF16)]
    outs = group + group + [flat(n, F32) for n in _IN_WIDTHS[6:]]
    return pl.pallas_call(
        _in_proj_body,
        grid=(t // tm,),
        in_specs=[pl.BlockSpec((tm, D_MODEL), lambda i: (i, 0)),
                  pl.BlockSpec((1, D_MODEL), lambda i: (0, 0)),
                  pl.BlockSpec((D_MODEL, _IN_TOTAL), lambda i: (0, 0))],
        out_specs=[o[0] for o in outs],
        out_shape=[o[1] for o in outs],
        compiler_params=_cparams(("parallel",)),
        name="in_proj",
    )(x, g.reshape(1, D_MODEL), w_perm)


def _logf_cumsum_body(v_ref, b_ref, logf_ref, cum_ref, *, plen, rows):
    vals = v_ref[0, 0]
    pos = lax.broadcasted_iota(jnp.int32, (rows, LANES), 0) * LANES + lax.broadcasted_iota(jnp.int32, (rows, LANES), 1)
    logf = jnp.where(pos >= plen, jax.nn.log_sigmoid(vals + b_ref[0]), vals)
    logf_ref[0, 0] = logf
    kk = lax.broadcasted_iota(jnp.int32, (LANES, LANES), 0)
    jj = lax.broadcasted_iota(jnp.int32, (LANES, LANES), 1)
    in_row = _dot(logf, (kk <= jj).astype(F32), HI)
    tot = jnp.broadcast_to(in_row[:, LANES - 1:LANES], (rows, LANES))
    ri = lax.broadcasted_iota(jnp.int32, (rows, rows), 0)
    rj = lax.broadcasted_iota(jnp.int32, (rows, rows), 1)
    cum_ref[0, 0] = in_row + _dot((rj < ri).astype(F32), tot, HI)


def _logf_cumsum(vals, bias, plen):
    b, h, rows, _ = vals.shape
    spec = pl.BlockSpec((1, 1, rows, LANES), lambda i, j: (i, j, 0, 0))
    return pl.pallas_call(
        functools.partial(_logf_cumsum_body, plen=plen, rows=rows),
        grid=(b, h),
        in_specs=[spec, pl.BlockSpec((1, 1, 1), lambda i, j: (j, 0, 0))],
        out_specs=[spec, spec],
        out_shape=[jax.ShapeDtypeStruct(vals.shape, F32)] * 2,
        compiler_params=_cparams(("parallel", "parallel")),
        name="logf_cumsum",
    )(vals, bias.reshape(h, 1, 1))


LOG2E = math.log2(math.e)
FAR = 'far'


def _sweep_plan(kinds_per_q):
    n_pairs, sig_id, sigs = [], [], []
    for kinds in kinds_per_q:
        n_far = 0
        while n_far < len(kinds) and kinds[n_far] == FAR:
            n_far += 1
        pairs = min(n_far, len(kinds) - 1) // 2
        tail = tuple(kinds[2 * pairs:])
        if tail not in sigs:
            sigs.append(tail)
        n_pairs.append(pairs)
        sig_id.append(sigs.index(tail))
    return np.asarray(n_pairs, np.int32), np.asarray(sig_id, np.int32), sigs


def _kv_sweep(n_pairs, sig, sigs, qk, consume, bufs_a, bufs_b):
    def put(bufs, tiles):
        for buf, tile in zip(bufs, tiles):
            buf[...] = tile

    def get(bufs):
        return tuple(buf[...] for buf in bufs)

    put(bufs_a, qk(0))

    def pair(t, carry):
        j = 2 * t
        put(bufs_b, qk(j + 1))
        consume(get(bufs_a), j, FAR)
        put(bufs_a, qk(j + 2))
        consume(get(bufs_b), j + 1, FAR)
        return carry

    lax.fori_loop(0, n_pairs, pair, 0)
    base = 2 * n_pairs
    for sid, tail in enumerate(sigs):
        @pl.when(sig == sid)
        def _():
            cur, nxt = bufs_a, bufs_b
            for off, kind in enumerate(tail):
                if off + 1 < len(tail):
                    put(nxt, qk(base + off + 1))
                consume(get(cur), base + off, kind)
                cur, nxt = nxt, cur


def _softmax_block(s, v_blk, m_ref, acc_ref):
    m_old = m_ref[...]
    m_new = jnp.maximum(m_old, jnp.max(s, axis=-1, keepdims=True))
    alpha = jnp.exp2(m_old - m_new)
    bk = s.shape[1]
    m_wide = jnp.concatenate([m_new] * (bk // LANES), axis=1) if bk % LANES == 0 else m_new[:, 0:1]
    p = jnp.exp2(s - m_wide)
    acc_ref[...] = alpha * acc_ref[...] + _dot(p.astype(BF16), v_blk)
    m_ref[...] = m_new


def _with_ones(v):
    b, h, s, d = v.shape
    return jnp.concatenate([v, jnp.ones((b, h, s, 1), v.dtype), jnp.zeros((b, h, s, LANES - d - 1), v.dtype)],
                           axis=-1).astype(BF16)


def _t5_bucket(rel):
    half = NUM_BUCKETS // 2
    max_exact = half // 2
    ret = jnp.where(rel > 0, half, 0)
    n = jnp.abs(rel)
    nf = jnp.maximum(n, 1).astype(F32)
    large = max_exact + (jnp.log(nf / max_exact) / math.log(REL_MAX_DIST / max_exact)
                         * (half - max_exact)).astype(jnp.int32)
    large = jnp.minimum(large, half - 1)
    return ret + jnp.where(n < max_exact, n, large)


def _bucket_bias(rel_bias, rel):
    bucket = _t5_bucket(rel)
    table = rel_bias.astype(F32)
    out = jnp.zeros((table.shape[1],) + rel.shape, F32)
    for b in range(NUM_BUCKETS):
        out = jnp.where(bucket[None] == b, table[b].reshape((-1,) + (1,) * rel.ndim), out)
    return out


def _diff_plan(nq, nk, plen, bq, bk):
    far = bk + 2 * REL_MAX_DIST
    kinds_per_q, deltas = [], []
    for i in range(nq):
        q_hi = (plen + i * bq + bq - 1) // CHUNK
        kinds = []
        for j in range(nk):
            if (j * bk) // CHUNK > q_hi:
                break
            d = plen + i * bq - j * bk
            if d >= far:
                kinds.append(FAR)
            else:
                if d not in deltas:
                    deltas.append(d)
                kinds.append(('tile', deltas.index(d)))
        kinds_per_q.append(kinds)
    return kinds_per_q, deltas


def _diff_bias_tiles(rel_bias, deltas, bq, bk):
    r = jnp.arange(bq, dtype=jnp.int32)[:, None]
    c = jnp.arange(bk, dtype=jnp.int32)[None, :]
    far_bias = _bucket_bias(rel_bias, jnp.full((1, 1), -4 * REL_MAX_DIST, jnp.int32))
    tiles = []
    for d in deltas:
        bias = (_bucket_bias(rel_bias, c - r - d) - far_bias) * LOG2E
        vis = (c // CHUNK) <= ((r + d) // CHUNK)
        tiles.append(jnp.where(vis[None], bias, NEG_BIG))
    return jnp.stack(tiles, axis=0)


def _diff_attn_body(pairs_ref, sig_ref, q_ref, k_ref, v_ref, bias_ref, lam_ref, g_ref, o_ref,
                    sa1, sa2, sb1, sb2, m1, acc1, m2, acc2, *, bk, sigs, lam_init):
    i = pl.program_id(2)
    dv = A_V_DIM
    q = q_ref[0, 0] * (A_QK_DIM ** -0.5 * LOG2E)
    lane = lax.broadcasted_iota(jnp.int32, (1, 2 * A_QK_DIM), 1)
    q1 = jnp.where(lane < A_QK_DIM, q, 0.0).astype(BF16)
    q2 = jnp.where(lane >= A_QK_DIM, q, 0.0).astype(BF16)
    for m, acc in ((m1, acc1), (m2, acc2)):
        m[...] = jnp.full(m.shape, -jnp.inf, F32)
        acc[...] = jnp.zeros(acc.shape, F32)

    def rows(j):
        return pl.ds(j * bk if isinstance(j, int) else pl.multiple_of(j * bk, bk), bk)

    def qk(j):
        kb = k_ref[0, 0, rows(j), :]
        return _dot_nt(q1, kb), _dot_nt(q2, kb)

    def consume(tiles, j, kind):
        vb = v_ref[0, 0, rows(j), :]
        for s, m, acc in zip(tiles, (m1, m2), (acc1, acc2)):
            if kind != FAR:
                s = s + bias_ref[kind[1], 0]
            _softmax_block(s, vb, m, acc)

    _kv_sweep(pairs_ref[i], sig_ref[i], sigs, qk, consume, (sa1, sa2), (sb1, sb2))

    lp = lam_ref[...]
    lam = (jnp.exp(jnp.sum(lp[0:1] * lp[1:2], axis=-1, keepdims=True))
           - jnp.exp(jnp.sum(lp[2:3] * lp[3:4], axis=-1, keepdims=True)) + lam_init)
    a1 = acc1[...]
    a2 = acc2[...]
    o = a1[:, 0:dv] / a1[:, dv:dv + 1] - lam * (a2[:, 0:dv] / a2[:, dv:dv + 1])
    o = o * lax.rsqrt(jnp.mean(o * o, axis=-1, keepdims=True) + EPS) * g_ref[...]
    o_ref[0, 0] = o * (1.0 - lam_init)


def _diff_attn(q, k, v_aug, rel_bias, a_lambda, a_norm_g, plen, bq, bk, lam_init):
    b, h, sq, dq = q.shape
    sk = k.shape[2]
    dv = A_V_DIM
    nq, nk = sq // bq, sk // bk
    kinds, deltas = _diff_plan(nq, nk, plen, bq, bk)
    n_pairs, sig_id, sigs = _sweep_plan(kinds)
    tiles = _diff_bias_tiles(rel_bias, deltas, bq, bk)
    nt = tiles.shape[0]
    per_head = lambda shape: pl.BlockSpec((1, 1) + shape, lambda b_, h_, i, *_: (b_, h_, 0, 0))
    grid_spec = pltpu.PrefetchScalarGridSpec(
        num_scalar_prefetch=2,
        grid=(b, h, nq),
        in_specs=[pl.BlockSpec((1, 1, bq, dq), lambda b_, h_, i, *_: (b_, h_, i, 0)),
                  per_head((sk, dq)), per_head((sk, LANES)),
                  pl.BlockSpec((nt, 1, bq, bk), lambda b_, h_, i, *_: (0, h_, 0, 0)),
                  pl.BlockSpec((4, A_QK_DIM), lambda b_, h_, i, *_: (0, 0)),
                  pl.BlockSpec((1, dv), lambda b_, h_, i, *_: (0, 0))],
        out_specs=pl.BlockSpec((1, 1, bq, dv), lambda b_, h_, i, *_: (b_, h_, i, 0)),
        scratch_shapes=[pltpu.VMEM((bq, bk), F32)] * 4 + [pltpu.VMEM((bq, LANES), F32)] * 4,
    )
    return pl.pallas_call(
        functools.partial(_diff_attn_body, bk=bk, sigs=sigs, lam_init=lam_init),
        grid_spec=grid_spec,
        out_shape=jax.ShapeDtypeStruct((b, h, sq, dv), F32),
        compiler_params=_cparams(("parallel", "parallel", "arbitrary")),
        name="diff_attn",
    )(jnp.asarray(n_pairs), jnp.asarray(sig_id), q, k, v_aug, tiles, a_lambda, a_norm_g.reshape(1, dv))


MASK = 'mask'


def _forget_attn_body(pairs_ref, sig_ref, q_ref, k_ref, v_ref, ck_ref, cref_ref, o_ref, sa, sb, m, acc,
                      *, plen, bq, bk, sigs):
    i = pl.program_id(2)
    d = B_HEAD_DIM
    q = (q_ref[0, 0] * (d ** -0.5 * LOG2E)).astype(BF16)
    cref = cref_ref[0, 0, 0]
    m[...] = jnp.full(m.shape, -jnp.inf, F32)
    acc[...] = jnp.zeros(acc.shape, F32)

    def rows(j):
        return pl.ds(j * bk if isinstance(j, int) else pl.multiple_of(j * bk, bk), bk)

    def qk(j):
        e = (ck_ref[0, 0, pl.ds(j, 1), :] - cref) * LOG2E
        return (_dot_nt(q, k_ref[0, 0, rows(j), :]) - e,)

    def consume(tiles, j, kind):
        s = tiles[0]
        if kind == MASK:
            kpos = j * bk + lax.broadcasted_iota(jnp.int32, (bq, bk), 1)
            qpos = plen + i * bq + lax.broadcasted_iota(jnp.int32, (bq, bk), 0)
            s = jnp.where(kpos <= qpos, s, NEG_BIG)
        _softmax_block(s, v_ref[0, 0, rows(j), :], m, acc)

    _kv_sweep(pairs_ref[i], sig_ref[i], sigs, qk, consume, (sa,), (sb,))
    a = acc[...]
    o_ref[0, 0] = a[:, 0:d] / a[:, d:d + 1]


def _forget_attn(q, k, v_aug, cum, plen, bq, bk):
    b, h, sq, d = q.shape
    sk = k.shape[2]
    nq, nk = sq // bq, sk // bk
    kinds = []
    for i in range(nq):
        q_lo, q_hi = plen + i * bq, plen + i * bq + bq - 1
        kinds.append([FAR if j * bk + bk - 1 <= q_lo else MASK for j in range(nk) if j * bk <= q_hi])
    n_pairs, sig_id, sigs = _sweep_plan(kinds)
    cref = cum[:, :, plen:plen + sq:bq].reshape(b, h, nq, 1, 1)
    ck = cum.reshape(b, h, nk, bk)
    per_head = lambda shape: pl.BlockSpec((1, 1) + shape, lambda b_, h_, i, *_: (b_, h_, 0, 0))
    grid_spec = pltpu.PrefetchScalarGridSpec(
        num_scalar_prefetch=2,
        grid=(b, h, nq),
        in_specs=[pl.BlockSpec((1, 1, bq, d), lambda b_, h_, i, *_: (b_, h_, i, 0)),
                  per_head((sk, d)), per_head((sk, LANES)), per_head((nk, bk)),
                  pl.BlockSpec((1, 1, 1, 1, 1), lambda b_, h_, i, *_: (b_, h_, i, 0, 0))],
        out_specs=pl.BlockSpec((1, 1, bq, d), lambda b_, h_, i, *_: (b_, h_, i, 0)),
        scratch_shapes=[pltpu.VMEM((bq, bk), F32)] * 2 + [pltpu.VMEM((bq, LANES), F32)] * 2,
    )
    return pl.pallas_call(
        functools.partial(_forget_attn_body, plen=plen, bq=bq, bk=bk, sigs=sigs),
        grid_spec=grid_spec,
        out_shape=jax.ShapeDtypeStruct((b, h, sq, d), F32),
        compiler_params=_cparams(("parallel", "parallel", "arbitrary")),
        name="forget_attn",
    )(jnp.asarray(n_pairs), jnp.asarray(sig_id), q, k, v_aug, ck, cref)


_CONV_PAD = 8


def _causal_conv_tile(buf, x, cw_ref, rows):
    buf[_CONV_PAD:_CONV_PAD + rows, :] = x
    lo = _CONV_PAD - (CONV_W - 1)
    y = buf[lo:lo + rows, :] * cw_ref[0:1, :]
    for j in range(1, CONV_W):
        y = y + buf[lo + j:lo + j + rows, :] * cw_ref[j:j + 1, :]
    buf[lo:_CONV_PAD, :] = x[rows - (CONV_W - 1):rows, :]
    return y


def _rg_lru_body(x_ref, gate_ref, h0_ref, conv0_ref, cw_ref, cb_ref, wa_ref, ba_ref, wx_ref, bx_ref, lam_ref,
                 y_ref, hn_ref, convn_ref, buf, hcar, *, ts):
    n = pl.program_id(1)

    @pl.when(n == 0)
    def _():
        buf[_CONV_PAD - (CONV_W - 1):_CONV_PAD, :] = conv0_ref[0]
        hcar[...] = h0_ref[0]

    x = x_ref[0]
    xc = _causal_conv_tile(buf, x, cw_ref, ts) + cb_ref[...]
    r = jax.nn.sigmoid(_dot(xc, wa_ref[...], HI) + ba_ref[...])
    i = jax.nn.sigmoid(_dot(xc, wx_ref[...], HI) + bx_ref[...])
    log_a = -C_POWER * r * jax.nn.softplus(-lam_ref[...])
    a = jnp.exp(log_a)
    th = jnp.tanh(log_a)
    b = jnp.sqrt(-2.0 * th / (1.0 - th)) * i * xc
    row = lax.broadcasted_iota(jnp.int32, a.shape, 0)
    d = 1
    while d < ts:
        keep = row >= d
        a_sh = jnp.where(keep, pltpu.roll(a, d, 0), 1.0)
        b_sh = jnp.where(keep, pltpu.roll(b, d, 0), 0.0)
        b = a * b_sh + b
        a = a * a_sh
        d *= 2
    h = a * hcar[...] + b
    y_ref[0] = h * jax.nn.gelu(gate_ref[0])
    hcar[...] = h[ts - 1:ts, :]

    @pl.when(n == pl.num_programs(1) - 1)
    def _():
        hn_ref[0] = h[ts - 1:ts, :]
        convn_ref[0] = x[ts - (CONV_W - 1):ts, :]


def _block_diag(w):
    n, d, _ = w.shape
    eye = jnp.eye(n, dtype=w.dtype)
    return (eye[:, None, :, None] * w[:, :, None, :]).reshape(n * d, n * d)


def _rg_lru(x, gate, h0, conv0, p, ts):
    b, s, w = x.shape
    row = lambda a: a.reshape(1, w)
    tile = pl.BlockSpec((1, ts, w), lambda i, n: (i, n, 0))
    const = lambda shape: pl.BlockSpec(shape, lambda i, n: (0,) * len(shape))
    return pl.pallas_call(
        functools.partial(_rg_lru_body, ts=ts),
        grid=(b, s // ts),
        in_specs=[tile, tile,
                  pl.BlockSpec((1, 1, w), lambda i, n: (i, 0, 0)),
                  pl.BlockSpec((1, CONV_W - 1, w), lambda i, n: (i, 0, 0)),
                  const((CONV_W, w)), const((1, w)), const((w, w)), const((1, w)), const((w, w)), const((1, w)),
                  const((1, w))],
        out_specs=[tile,
                   pl.BlockSpec((1, 1, w), lambda i, n: (i, 0, 0)),
                   pl.BlockSpec((1, CONV_W - 1, w), lambda i, n: (i, 0, 0))],
        out_shape=[jax.ShapeDtypeStruct((b, s, w), F32), jax.ShapeDtypeStruct((b, 1, w), F32),
                   jax.ShapeDtypeStruct((b, CONV_W - 1, w), F32)],
        scratch_shapes=[pltpu.VMEM((_CONV_PAD + ts, w), F32), pltpu.VMEM((1, w), F32)],
        compiler_params=_cparams(("parallel", "arbitrary")),
        name="rg_lru",
    )(x, gate, h0.reshape(b, 1, w), conv0, p['c_conv_w'], row(p['c_conv_b']),
      _block_diag(p['c_gate_a_w']), row(p['c_gate_a_b']), _block_diag(p['c_gate_x_w']), row(p['c_gate_x_b']),
      row(p['c_lambda']))


def _gdn_body(qkv_ref, z_ref, bl_ref, al_ref, s0_ref, conv0_ref, cw_ref, alog_ref, dtb_ref, ng_ref,
              o_ref, sn_ref, convn_ref, buf, st, *, c, per_step, tail):
    n = pl.program_id(1)
    w = D_WIDTH
    hd = D_HEAD_DIM

    @pl.when(n == 0)
    def _():
        buf[_CONV_PAD - (CONV_W - 1):_CONV_PAD, :] = conv0_ref[0]
        st[...] = s0_ref[0]

    rows = per_step * c
    x = qkv_ref[0]
    y = jax.nn.silu(_causal_conv_tile(buf, x, cw_ref, rows))
    q, k, v = y[:, 0:w], y[:, w:2 * w], y[:, 2 * w:3 * w]

    same_head = (lax.broadcasted_iota(jnp.int32, (w, w), 0) // hd) == (lax.broadcasted_iota(jnp.int32, (w, w), 1) // hd)
    head_sum = same_head.astype(BF16)

    def head_sums(x2):
        hi, lo = _split_bf16(x2)
        return _dot(hi, head_sum) + _dot(lo, head_sum)

    def bd(a):
        return jnp.where(same_head, jnp.concatenate([a] * D_HEADS, axis=0), 0.0)

    def dot3_bd(a, b):
        return _dot3(a, bd(b))

    q = q * lax.rsqrt(head_sums(q * q) + EPS) * (hd ** -0.5)
    k = k * lax.rsqrt(head_sums(k * k) + EPS)
    beta = jax.nn.sigmoid(bl_ref[0])
    g = -jnp.exp(alog_ref[...]) * jax.nn.softplus(al_ref[0] + dtb_ref[...])
    if tail < rows:
        real = lax.broadcasted_iota(jnp.int32, (rows, w), 0) < jnp.where(n == pl.num_programs(1) - 1, tail, rows)
        beta = jnp.where(real, beta, 0.0)
        g = jnp.where(real, g, 0.0)

    ti = lax.broadcasted_iota(jnp.int32, (c, w), 0)
    tj = lax.broadcasted_iota(jnp.int32, (c, w), 1) % hd
    incl = ti >= tj
    strict = ti > tj
    diag = ti == tj
    ci = lax.broadcasted_iota(jnp.int32, (c, c), 0)
    cj = lax.broadcasted_iota(jnp.int32, (c, c), 1)
    lower_ones = (ci >= cj).astype(BF16)
    all_ones = jnp.ones((c, c), BF16)

    def chunk_local(lo):
        sl = slice(lo, lo + c)
        qc, kc, vc, bc = q[sl], k[sl], v[sl], beta[sl]
        gcum = _dot_exact_left(lower_ones, g[sl])
        g_row = _dot_exact_left(all_ones, jnp.where(diag, gcum, 0.0))
        decay = jnp.where(incl, jnp.exp(gcum - g_row), 0.0)
        eg = jnp.exp(gcum)
        kb = kc * bc
        g_last = gcum[c - 1:c, :]
        k_t = jnp.where(same_head, _dot_tn(kc.astype(BF16), diag.astype(BF16)), 0.0).astype(BF16)
        gram = _dot(jnp.concatenate([kb, qc], axis=0).astype(BF16), k_t)
        low = jnp.where(strict, gram[0:c] * decay, 0.0)
        inv = diag.astype(F32) - low
        pw = low
        span = 2
        while span < c:
            pw = dot3_bd(pw, pw)
            inv = inv + dot3_bd(inv, pw)
            span *= 2
        return dict(v_w=dot3_bd(inv, vc * bc), k_w=dot3_bd(inv, kb * eg), attn=gram[c:2 * c] * decay,
                    q_g=qc * eg, k_d=kc * jnp.exp(g_last - gcum), g_last=g_last)

    local = [chunk_local(i * c) for i in range(per_step)]
    state = st[...]
    outs = []
    for ch in local:
        through = _dot3(jnp.concatenate([ch['k_w'], ch['q_g']], axis=0), state)
        v_new = ch['v_w'] - through[0:c]
        outs.append(through[c:2 * c] + dot3_bd(ch['attn'], v_new))
        kd_hi, kd_lo = _split_bf16(ch['k_d'])
        vn_hi, vn_lo = _split_bf16(v_new)
        outer = _dot_tn(kd_hi, vn_hi) + (_dot_tn(kd_hi, vn_lo) + _dot_tn(kd_lo, vn_hi))
        state = state * jnp.exp(ch['g_last']) + jnp.where(same_head, outer, 0.0)
    st[...] = state

    o = outs[0] if per_step == 1 else jnp.concatenate(outs, axis=0)
    o = o * lax.rsqrt(head_sums(o * o) * (1.0 / hd) + EPS) * ng_ref[...]
    o_ref[0] = o * jax.nn.silu(z_ref[0])

    @pl.when(n == pl.num_programs(1) - 1)
    def _():
        sn_ref[0] = state
        convn_ref[0] = x[tail - (CONV_W - 1):tail, :]


def _gdn(qkv, z, beta_logit, alpha_logit, s0, conv0, p):
    b, s_true, w3 = qkv.shape
    w = D_WIDTH
    c = D_HEAD_DIM
    per_step = 2 if s_true >= 2 * c else 1
    rows = per_step * c
    s = -(-s_true // rows) * rows
    tail = s_true - (s - rows)
    assert tail >= CONV_W - 1
    if s != s_true:
        pad = lambda a: jnp.pad(a, ((0, 0), (0, s - s_true), (0, 0)))
        qkv, z, beta_logit, alpha_logit = pad(qkv), pad(z), pad(beta_logit), pad(alpha_logit)
    s0_bd = jax.vmap(_block_diag)(s0)
    per_lane = lambda a: jnp.repeat(a, D_HEAD_DIM).reshape(1, w)
    tile = lambda n_: pl.BlockSpec((1, rows, n_), lambda i, n: (i, n, 0))
    const = lambda shape: pl.BlockSpec(shape, lambda i, n: (0,) * len(shape))
    per_b = lambda shape: pl.BlockSpec((1,) + shape, lambda i, n: (i,) + (0,) * len(shape))
    o, sn, convn = pl.pallas_call(
        functools.partial(_gdn_body, c=c, per_step=per_step, tail=tail),
        grid=(b, s // rows),
        in_specs=[tile(w3), tile(w), tile(w), tile(w), per_b((w, w)), per_b((CONV_W - 1, w3)),
                  const((CONV_W, w3)), const((1, w)), const((1, w)), const((1, w))],
        out_specs=[tile(w), per_b((w, w)), per_b((CONV_W - 1, w3))],
        out_shape=[jax.ShapeDtypeStruct((b, s, w), F32), jax.ShapeDtypeStruct((b, w, w), F32),
                   jax.ShapeDtypeStruct((b, CONV_W - 1, w3), F32)],
        scratch_shapes=[pltpu.VMEM((_CONV_PAD + rows, w3), F32), pltpu.VMEM((w, w), F32)],
        compiler_params=_cparams(("parallel", "arbitrary")),
        name="gdn",
    )(qkv, z, beta_logit, alpha_logit, s0_bd, conv0, p['d_conv_w'], per_lane(p['d_a_log']), per_lane(p['d_dt_bias']),
      jnp.tile(p['d_norm_g'], D_HEADS).reshape(1, w))
    hd = D_HEAD_DIM
    sn = jnp.stack([sn[:, h * hd:(h + 1) * hd, h * hd:(h + 1) * hd] for h in range(D_HEADS)], axis=1)
    return o[:, :s_true], sn, convn


_PEER_SLABS = 2 * PEER_HEADS


def _out_proj_body(x_ref, oa_ref, ob_ref, oc_ref, od_ref, wo_ref, g_ref, wq_ref, sk_ref,
                   x2_ref, h2t_ref, st_ref):
    x2 = x_ref[...]
    for grp, o_ref in enumerate((oa_ref, ob_ref)):
        for hd_ in range(_ATTN_HEADS):
            lo = grp * GROUP_WIDTH + hd_ * _ATTN_DIM
            x2 = x2 + _dot(o_ref[hd_].astype(BF16), wo_ref[lo:lo + _ATTN_DIM, :])
    for grp, o_ref in ((2, oc_ref), (3, od_ref)):
        x2 = x2 + _dot(o_ref[...].astype(BF16), wo_ref[grp * GROUP_WIDTH:(grp + 1) * GROUP_WIDTH, :])
    x2_ref[...] = x2
    h2 = x2 * lax.rsqrt(jnp.mean(x2 * x2, axis=-1, keepdims=True) + EPS) * g_ref[...]
    h2t_ref[...] = h2.T.astype(BF16)
    q = _dot(h2.astype(BF16), wq_ref[...]).astype(BF16)
    for slab in range(_PEER_SLABS):
        st_ref[slab] = _dot_nt(sk_ref[slab % 2], q[:, slab * PEER_HALF:(slab + 1) * PEER_HALF])


def _out_proj(x, o_a, o_b, o_c, o_d, w_out, norm2_g, wq, subkeys, tm):
    t = x.shape[0]
    tile = lambda n: pl.BlockSpec((tm, n), lambda i: (i, 0))
    heads = pl.BlockSpec((_ATTN_HEADS, tm, _ATTN_DIM), lambda i: (0, i, 0))
    const = lambda shape: pl.BlockSpec(shape, lambda i: (0,) * len(shape))
    nq = PEER_HEADS * PEER_QUERY_DIM
    return pl.pallas_call(
        _out_proj_body,
        grid=(t // tm,),
        in_specs=[tile(D_MODEL), heads, heads, tile(GROUP_WIDTH), tile(GROUP_WIDTH),
                  const((D_MODEL, D_MODEL)), const((1, D_MODEL)), const((D_MODEL, nq)),
                  const((2, PEER_N_KEYS, PEER_HALF))],
        out_specs=[tile(D_MODEL), pl.BlockSpec((D_MODEL, tm), lambda i: (0, i)),
                   pl.BlockSpec((_PEER_SLABS, PEER_N_KEYS, tm), lambda i: (0, 0, i))],
        out_shape=[jax.ShapeDtypeStruct((t, D_MODEL), F32), jax.ShapeDtypeStruct((D_MODEL, t), BF16),
                   jax.ShapeDtypeStruct((_PEER_SLABS, PEER_N_KEYS, t), F32)],
        compiler_params=_cparams(("parallel",)),
        name="out_proj",
    )(x, o_a, o_b, o_c, o_d, w_out, norm2_g.reshape(1, D_MODEL), wq, subkeys)


_NO_RANK = 64.0


def _top_values(x, count, with_rank=False):
    vals = []
    rank = jnp.full(x.shape, _NO_RANK, F32) if with_rank else None
    for k in range(count):
        m = jnp.max(x, axis=0, keepdims=True)
        vals.append(m)
        hit = x == m
        if with_rank:
            rank = jnp.where(hit, float(k), rank)
        x = jnp.where(hit, -jnp.inf, x)
    return vals, x, rank


def _peer_select_body(s_ref, c_ref, cnt_ref, r2_ref, e2_ref):
    s1 = s_ref[0]
    s2 = s_ref[1]
    n = s1.shape[1]
    v1, rest1, _ = _top_values(s1, PEER_TOPK)
    v2, _, rank2 = _top_values(s2, PEER_TOPK, with_rank=True)
    sel1 = rest1 == -jnp.inf
    v1_all = jnp.concatenate(v1, axis=0)
    v2_all = jnp.concatenate(v2, axis=0)
    row8 = lax.broadcasted_iota(jnp.int32, (8, n), 0)
    cand = [v1[0] + v2_all, v1[1] + v2_all[0:8]]
    for i in range(2, 8):
        cand.append(jnp.where(row8 < PEER_TOPK // (i + 1), v1[i] + v2_all[0:8], -jnp.inf))
    cand.append(v1_all[8:16] + v2[0])
    top, _, _ = _top_values(jnp.concatenate(cand, axis=0), PEER_TOPK)
    tau = top[PEER_TOPK - 1]
    z = jnp.ones_like(tau)
    for t in top[1:]:
        z = z + jnp.exp(t - top[0])
    count_sorted = jnp.zeros((PEER_TOPK, n), F32)
    for j in range(PEER_TOPK):
        count_sorted = count_sorted + jnp.where(v1_all + v2[j] >= tau, 1.0, 0.0)
    count = jnp.zeros(s1.shape, F32)
    for i in range(PEER_TOPK):
        count = jnp.where(s1 == v1[i], count_sorted[i:i + 1, :], count)
    cnt_ref[0] = count
    c_ref[0] = jnp.where(sel1, jnp.exp(s1 - v1[0]), 0.0) / z
    r2_ref[0] = rank2.astype(BF16)
    e2_ref[0] = jnp.where(rank2 < _NO_RANK, jnp.exp(s2 - v2[0]), 0.0).astype(BF16)


def _peer_select(st, tm):
    t = st.shape[2]
    out_spec = pl.BlockSpec((1, PEER_N_KEYS, tm), lambda i, h: (h, 0, i))
    f32 = jax.ShapeDtypeStruct((PEER_HEADS, PEER_N_KEYS, t), F32)
    bf16 = jax.ShapeDtypeStruct((PEER_HEADS, PEER_N_KEYS, t), BF16)
    return pl.pallas_call(
        _peer_select_body,
        grid=(t // tm, PEER_HEADS),
        in_specs=[pl.BlockSpec((2, PEER_N_KEYS, tm), lambda i, h: (h, 0, i))],
        out_specs=[out_spec] * 4,
        out_shape=[f32, f32, bf16, bf16],
        compiler_params=_cparams(("parallel", "parallel")),
        name="peer_select",
    )(st)


_PEER_GROUP = 4


def _peer_dense_body(ht_ref, u_ref, vt_ref, c_ref, cnt_ref, r2_ref, e2_ref, o_ref, act, pbuf, *, na):
    e = pl.program_id(1)
    ht = ht_ref[...]
    tm = ht.shape[1]
    rows_g = _PEER_GROUP * PEER_N_KEYS
    groups = na // _PEER_GROUP
    packed_rows = 16

    def span(g):
        return slice(g * rows_g, (g + 1) * rows_g)

    def activations(g):
        act[span(g), :] = _dot(u_ref[span(g), :], ht)

    def over_keys(row):
        packed = jnp.broadcast_to(row, (packed_rows, tm)).astype(BF16)
        return jnp.concatenate([packed] * (PEER_N_KEYS // packed_rows), axis=0)

    def weights(g):
        for a in range(g * _PEER_GROUP, (g + 1) * _PEER_GROUP):
            rows = slice(a * PEER_N_KEYS, (a + 1) * PEER_N_KEYS)
            w = None
            for h in range(PEER_HEADS):
                chosen = r2_ref[h] < over_keys(cnt_ref[h, a:a + 1, :])
                term = jnp.where(chosen, e2_ref[h], 0.0) * over_keys(c_ref[h, a:a + 1, :])
                w = term if w is None else w + term
            pbuf[rows, :] = w * jax.nn.gelu(act[rows, :].astype(BF16))

    @pl.when(e == 0)
    def _():
        o_ref[...] = jnp.zeros(o_ref.shape, F32)

    activations(0)
    contrib = None
    for g in range(groups):
        if g + 1 < groups:
            activations(g + 1)
        weights(g)
        part = _dot(vt_ref[:, span(g)], pbuf[span(g), :])
        contrib = part if contrib is None else contrib + part
    o_ref[...] += contrib


def _peer_dense(ht, u, vt, c, count, rank2, e2, tm, eb):
    t = ht.shape[1]
    na = eb // PEER_N_KEYS
    key_rows = pl.BlockSpec((PEER_HEADS, na, tm), lambda i, e: (0, e, i))
    all_keys = pl.BlockSpec((PEER_HEADS, PEER_N_KEYS, tm), lambda i, e: (0, 0, i))
    return pl.pallas_call(
        functools.partial(_peer_dense_body, na=na),
        grid=(t // tm, PEER_EXPERTS // eb),
        in_specs=[pl.BlockSpec((D_MODEL, tm), lambda i, e: (0, i)),
                  pl.BlockSpec((eb, D_MODEL), lambda i, e: (e, 0)),
                  pl.BlockSpec((D_MODEL, eb), lambda i, e: (0, e)),
                  key_rows, key_rows, all_keys, all_keys],
        out_specs=pl.BlockSpec((D_MODEL, tm), lambda i, e: (0, i)),
        out_shape=jax.ShapeDtypeStruct((D_MODEL, t), F32),
        scratch_shapes=[pltpu.VMEM((eb, tm), F32), pltpu.VMEM((eb, tm), BF16)],
        compiler_params=_cparams(("parallel", "arbitrary")),
        name="peer_dense",
    )(ht, u, vt, c, count, rank2, e2)


def _residual_body(x_ref, pt_ref, g_ref, o_ref, *, final_norm):
    x = x_ref[...] + pt_ref[...].T
    if final_norm:
        x = x * lax.rsqrt(jnp.mean(x * x, axis=-1, keepdims=True) + EPS) * g_ref[...]
    o_ref[...] = x


def _residual(x, pt, g, tm, final_norm):
    t = x.shape[0]
    return pl.pallas_call(
        functools.partial(_residual_body, final_norm=final_norm),
        grid=(t // tm,),
        in_specs=[pl.BlockSpec((tm, D_MODEL), lambda i: (i, 0)),
                  pl.BlockSpec((D_MODEL, tm), lambda i: (0, i)),
                  pl.BlockSpec((1, D_MODEL), lambda i: (0, 0))],
        out_specs=pl.BlockSpec((tm, D_MODEL), lambda i: (i, 0)),
        out_shape=jax.ShapeDtypeStruct((t, D_MODEL), F32),
        compiler_params=_cparams(("parallel",)),
        name="residual",
    )(x, pt, g.reshape(1, D_MODEL))


def _heads_first(a):
    return jnp.transpose(a, (0, 2, 1, 3))


def _prep_layer_weights(w):
    return {
        'w_in': _permute_w_in(w['w_in']),
        'w_out': w['w_out'].astype(BF16),
        'wq': w['peer_wq'].astype(BF16),
        'u': w['peer_u'].astype(BF16),
        'vt': w['peer_v'].astype(BF16).T,
    }


def _mixers(x2d, bsz, s, past, p, pw, lam_init, cfg):
    pa_k, pa_v, pb_k, pb_v, pb_logf, c_h0, c_conv0, d_s0, d_conv0 = past
    plen = pa_k.shape[1]
    sk = plen + s
    (a_q, a_k, a_k16, a_v, a_v16, b_q, b_k, b_k16, b_v, b_v16,
     c_x, c_g, d_qkv, d_z, d_beta, d_alpha, small) = _in_proj(x2d, p['norm1_g'], pw['w_in'], cfg['tm_in'])
    bq, bk = cfg['bq'], cfg['bk']

    def seq(a, *tail):
        return a.reshape(bsz, s, *tail)

    def batch_heads(a):
        return jnp.transpose(a.reshape(a.shape[0], bsz, s, a.shape[2]), (1, 0, 2, 3))

    def head_major(a):
        return jnp.transpose(a, (1, 0, 2, 3)).reshape(a.shape[1], bsz * s, a.shape[3])

    def with_past(past_k, past_v, k16, v16):
        keys = jnp.concatenate([_heads_first(past_k).astype(BF16), batch_heads(k16)], axis=2)
        vals = jnp.concatenate([_with_ones(_heads_first(past_v)), batch_heads(v16)], axis=2)
        return keys, vals

    k_all, v_all = with_past(pa_k, pa_v, a_k16, a_v16)
    o_a = head_major(_diff_attn(batch_heads(a_q), k_all, v_all,
                                p['rel_bias'], p['a_lambda'], p['a_norm_g'], plen, bq, bk, lam_init))

    rows = -(-sk // LANES)
    rows = -(-rows // 8) * 8
    f_vals = jnp.concatenate([pb_logf, seq(small[:, 0:B_HEADS], B_HEADS)], axis=1)
    f_vals = jnp.pad(f_vals, ((0, 0), (0, rows * LANES - sk), (0, 0)))
    f_vals = jnp.transpose(f_vals, (0, 2, 1)).reshape(bsz, B_HEADS, rows, LANES)
    logf, cum = _logf_cumsum(f_vals, p['b_forget_bias'], plen)
    cum = cum.reshape(bsz, B_HEADS, rows * LANES)[:, :, :sk]
    b_logf = jnp.transpose(logf.reshape(bsz, B_HEADS, rows * LANES)[:, :, plen:sk], (0, 2, 1))
    kb_all, vb_all = with_past(pb_k, pb_v, b_k16, b_v16)
    o_b = head_major(_forget_attn(batch_heads(b_q), kb_all, vb_all, cum, plen, bq, bk))

    o_c, c_h, c_conv = _rg_lru(seq(c_x, C_WIDTH), seq(c_g, C_WIDTH), c_h0, c_conv0, p, cfg['ts'])

    o_d, d_s, d_conv = _gdn(seq(d_qkv, 3 * D_WIDTH), seq(d_z, D_WIDTH), seq(d_beta, D_WIDTH), seq(d_alpha, D_WIDTH),
                            d_s0, d_conv0, p)

    state = (seq(a_k, A_HEADS, 2 * A_QK_DIM), seq(a_v, A_HEADS, A_V_DIM), seq(b_k, B_HEADS, B_HEAD_DIM),
             seq(b_v, B_HEADS, B_HEAD_DIM), b_logf, c_h.reshape(bsz, C_WIDTH), c_conv, d_s, d_conv)
    return (o_a, o_b, o_c.reshape(bsz * s, C_WIDTH), o_d.reshape(bsz * s, D_WIDTH)), state


def _layer(x2d, bsz, s, past, p, pw, lam_init, cfg):
    mix, state = _mixers(x2d, bsz, s, past, p, pw, lam_init, cfg)
    x2, h2t, st = _out_proj(x2d, *mix, pw['w_out'], p['norm2_g'], pw['wq'], p['peer_subkeys'].astype(BF16),
                            cfg['tm_out'])
    c, count, rank2, e2 = _peer_select(st, cfg['tm_sel'])
    peer_t = _peer_dense(h2t, pw['u'], pw['vt'], c, count, rank2, e2, cfg['tm_peer'], cfg['eb'])
    return x2, peer_t, state


_PROMPT_CFG = dict(tm_in=512, bq=1024, bk=512, ts=256, tm_out=256, tm_sel=512, tm_peer=512, eb=2048, tm_res=256)


def _sample_cfg(s, sk):
    return dict(tm_in=512, bq=s, bk=sk, ts=s, tm_out=256, tm_sel=512, tm_peer=512, eb=2048, tm_res=256)


def kernel(x_prompt, x_sample, cache_a_k, cache_a_v, cache_b_k, cache_b_v, cache_b_logf, state_c_h, state_c_conv, state_d_s, state_d_conv, norm1_g, norm2_g, final_norm_g, w_in, w_out, rel_bias, a_lambda, a_norm_g, b_forget_bias, c_conv_w, c_conv_b, c_gate_a_w, c_gate_a_b, c_gate_x_w, c_gate_x_b, c_lambda, d_conv_w, d_a_log, d_dt_bias, d_norm_g, peer_wq, peer_subkeys, peer_u, peer_v):
    pb, ps, _ = x_prompt.shape
    sb, ss, _ = x_sample.shape
    dt = x_prompt.dtype
    xp = x_prompt.reshape(pb * ps, D_MODEL)
    xs = x_sample.reshape(sb * ss, D_MODEL)
    scfg = _sample_cfg(ss, cache_a_k.shape[2] + ss)
    prompt_out, sample_out = [], []
    for l in range(DEPTH):
        p = {
            'norm1_g': norm1_g[l], 'norm2_g': norm2_g[l], 'rel_bias': rel_bias, 'a_lambda': a_lambda[l],
            'a_norm_g': a_norm_g[l], 'b_forget_bias': b_forget_bias[l],
            'c_conv_w': c_conv_w[l], 'c_conv_b': c_conv_b[l],
            'c_gate_a_w': c_gate_a_w[l], 'c_gate_a_b': c_gate_a_b[l],
            'c_gate_x_w': c_gate_x_w[l], 'c_gate_x_b': c_gate_x_b[l], 'c_lambda': c_lambda[l],
            'd_conv_w': d_conv_w[l], 'd_a_log': d_a_log[l], 'd_dt_bias': d_dt_bias[l], 'd_norm_g': d_norm_g[l],
            'peer_subkeys': peer_subkeys[l],
        }
        pw = _prep_layer_weights({'w_in': w_in[l], 'w_out': w_out[l], 'peer_wq': peer_wq[l],
                                  'peer_u': peer_u[l], 'peer_v': peer_v[l]})
        lam_init = 0.8 - 0.6 * math.exp(-0.3 * l)
        empty = (
            jnp.zeros((pb, 0, A_HEADS, 2 * A_QK_DIM), dt), jnp.zeros((pb, 0, A_HEADS, A_V_DIM), dt),
            jnp.zeros((pb, 0, B_HEADS, B_HEAD_DIM), dt), jnp.zeros((pb, 0, B_HEADS, B_HEAD_DIM), dt),
            jnp.zeros((pb, 0, B_HEADS), dt),
            jnp.zeros((pb, C_WIDTH), dt), jnp.zeros((pb, CONV_W - 1, C_WIDTH), dt),
            jnp.zeros((pb, D_HEADS, D_HEAD_DIM, D_HEAD_DIM), dt), jnp.zeros((pb, CONV_W - 1, 3 * D_WIDTH), dt),
        )
        last = l == DEPTH - 1
        xp2, peer_p, st_p = _layer(xp, pb, ps, empty, p, pw, lam_init, _PROMPT_CFG)
        xp = _residual(xp2, peer_p, final_norm_g, _PROMPT_CFG['tm_res'], last)
        prompt_out.append(st_p)
        past = (cache_a_k[l], cache_a_v[l], cache_b_k[l], cache_b_v[l], cache_b_logf[l],
                state_c_h[l], state_c_conv[l], state_d_s[l], state_d_conv[l])
        xs2, peer_s, st_s = _layer(xs, sb, ss, past, p, pw, lam_init, scfg)
        xs = _residual(xs2, peer_s, final_norm_g, scfg['tm_res'], last)
        sample_out.append(st_s)

    y_prompt = xp.reshape(pb, ps, D_MODEL)
    y_sample = xs.reshape(sb, ss, D_MODEL)
    p_states = [jnp.stack(z, axis=0) for z in zip(*prompt_out)]
    s_states = [jnp.stack(z, axis=0) for z in zip(*sample_out)]
    return (y_prompt, y_sample, *p_states, *s_states)
```

```python
import functools
import math

import numpy as np
import jax
import jax.numpy as jnp
from jax import lax
from jax.experimental import pallas as pl
from jax.experimental.pallas import tpu as pltpu

F32 = jnp.float32
BF16 = jnp.bfloat16
HI = lax.Precision.HIGHEST

D_MODEL = 1024
DEPTH = 2
CHUNK = 64
EPS = 1e-6
CONV_W = 4
GROUP_WIDTH = D_MODEL // 4
A_HEADS = 4
A_QK_DIM = GROUP_WIDTH // (2 * A_HEADS)
A_V_DIM = GROUP_WIDTH // A_HEADS
NUM_BUCKETS = 32
REL_MAX_DIST = 256
B_HEADS = 4
B_HEAD_DIM = GROUP_WIDTH // B_HEADS
C_WIDTH = GROUP_WIDTH
C_BLOCKS = 4
C_BLOCK_DIM = C_WIDTH // C_BLOCKS
C_POWER = 8.0
D_HEADS = 4
D_HEAD_DIM = GROUP_WIDTH // D_HEADS
D_WIDTH = GROUP_WIDTH
PEER_HEADS = 8
PEER_N_KEYS = 128
PEER_EXPERTS = PEER_N_KEYS * PEER_N_KEYS
PEER_TOPK = 16
PEER_QUERY_DIM = 256
PEER_HALF = PEER_QUERY_DIM // 2

LANES = 128
NEG_BIG = -1e30
VMEM_LIMIT = 56 * 1024 * 1024


def _cparams(sem):
    return pltpu.CompilerParams(dimension_semantics=sem, vmem_limit_bytes=VMEM_LIMIT)


def _dot(a, b, precision=None):
    return jnp.dot(a, b, preferred_element_type=F32, precision=precision)


def _dot_nt(a, b, precision=None):
    return lax.dot_general(a, b, (((1,), (1,)), ((), ())), preferred_element_type=F32, precision=precision)


def _dot_tn(a, b, precision=None):
    return lax.dot_general(a, b, (((0,), (0,)), ((), ())), preferred_element_type=F32, precision=precision)


def _split_bf16(x):
    hi = x.astype(BF16)
    return hi, (x - hi.astype(F32)).astype(BF16)


def _dot3(a, b):
    ah, al = _split_bf16(a)
    bh, bl = _split_bf16(b)
    return _dot(ah, bh) + (_dot(ah, bl) + _dot(al, bh))


def _dot_exact_left(a, b):
    b1 = b.astype(BF16)
    r1 = b - b1.astype(F32)
    b2 = r1.astype(BF16)
    b3 = (r1 - b2.astype(F32)).astype(BF16)
    return _dot(a, b1) + (_dot(a, b2) + _dot(a, b3))


_IN_WIDTHS = (256,) * 8 + (768, 256, 256, 256, LANES)
_IN_TOTAL = sum(_IN_WIDTHS)


_IN_RAW = 3084
_IN_ATTN = 1536
_IN_REST = 1536


def _permute_w_in_body(w_ref, o_ref):
    w = w_ref[0]
    rows = w.shape[0]
    o_ref[0, :, 0:_IN_ATTN] = w[:, 0:_IN_ATTN].astype(BF16)
    o_ref[0, :, _IN_ATTN:_IN_ATTN + _IN_REST] = w[:, _IN_ATTN + B_HEADS:_IN_ATTN + B_HEADS + _IN_REST].astype(BF16)
    tail = w[:, _IN_RAW - LANES:_IN_RAW].astype(BF16)
    src = lax.broadcasted_iota(jnp.int32, (LANES, 2 * D_WIDTH), 0)
    dst = lax.broadcasted_iota(jnp.int32, (LANES, 2 * D_WIDTH), 1)
    spread = (src == LANES - 2 * D_HEADS + dst // D_HEAD_DIM).astype(BF16)
    o_ref[0, :, _IN_ATTN + _IN_REST:_IN_ATTN + _IN_REST + 2 * D_WIDTH] = _dot(tail, spread).astype(BF16)
    lane = lax.broadcasted_iota(jnp.int32, (rows, LANES), 1)
    o_ref[0, :, _IN_TOTAL - LANES:_IN_TOTAL] = jnp.where(lane < B_HEADS, w[:, _IN_ATTN:_IN_ATTN + LANES], 0.0).astype(BF16)


def _permute_w_in(w_in):
    depth, rows, _ = w_in.shape
    tr = 256
    return pl.pallas_call(
        _permute_w_in_body,
        grid=(depth, rows // tr),
        in_specs=[pl.BlockSpec((1, tr, _IN_RAW), lambda l, i: (l, i, 0))],
        out_specs=pl.BlockSpec((1, tr, _IN_TOTAL), lambda l, i: (l, i, 0)),
        out_shape=jax.ShapeDtypeStruct((depth, rows, _IN_TOTAL), BF16),
        compiler_params=_cparams(("parallel", "parallel")),
        name="permute_w_in",
    )(w_in)


_ATTN_HEADS = 4
_ATTN_DIM = GROUP_WIDTH // _ATTN_HEADS


_STATE_OUTS = (1, 3, 6, 8)


def _in_proj_body(x_ref, g_ref, w_ref, *rest, n_carried):
    outs = rest[n_carried:]
    x = x_ref[...]
    tm = x.shape[0]
    h = x * lax.rsqrt(jnp.mean(x * x, axis=-1, keepdims=True) + EPS) * g_ref[...]
    hb = h.astype(BF16)
    lane = lax.broadcasted_iota(jnp.int32, (tm, _ATTN_DIM), 1)
    ones_col = jnp.where(lane == 0, 1.0, 0.0)

    def per_head(z, hd_):
        return z[:, hd_ * _ATTN_DIM:(hd_ + 1) * _ATTN_DIM]

    off = 0
    for grp in range(2):
        q_hm, k_ref, k_hm, v_ref, v_hm = outs[5 * grp:5 * grp + 5]
        q = _dot(hb, w_ref[:, off:off + GROUP_WIDTH])
        k = _dot(hb, w_ref[:, off + GROUP_WIDTH:off + 2 * GROUP_WIDTH])
        v = _dot(hb, w_ref[:, off + 2 * GROUP_WIDTH:off + 3 * GROUP_WIDTH])
        off += 3 * GROUP_WIDTH
        k_ref[...] = k
        v_ref[...] = v
        for hd_ in range(_ATTN_HEADS):
            q_hm[hd_] = per_head(q, hd_)
            k_hm[hd_] = per_head(k, hd_).astype(BF16)
            v_hm[hd_] = jnp.concatenate([per_head(v, hd_), ones_col], axis=1).astype(BF16)
    for o_ref, n in zip(outs[10:], _IN_WIDTHS[6:]):
        o_ref[...] = _dot(hb, w_ref[:, off:off + n])
        off += n


def _in_proj(x, g, w_perm, layer, tm, carried):
    t = x.shape[0]
    flat = lambda n, dt: (pl.BlockSpec((tm, n), lambda i: (i, 0)), jax.ShapeDtypeStruct((t, n), dt))
    head_major = lambda n, dt: (pl.BlockSpec((_ATTN_HEADS, tm, n), lambda i: (0, i, 0)),
                                jax.ShapeDtypeStruct((_ATTN_HEADS, t, n), dt))
    state = (pl.BlockSpec((None, tm, GROUP_WIDTH), lambda i: (layer, i, 0)),
             jax.ShapeDtypeStruct((DEPTH, t, GROUP_WIDTH), F32))
    group = [head_major(_ATTN_DIM, F32), state, head_major(_ATTN_DIM, BF16), state, head_major(LANES, BF16)]
    outs = group + group + [flat(n, F32) for n in _IN_WIDTHS[6:]]
    n_fixed = 3
    return pl.pallas_call(
        functools.partial(_in_proj_body, n_carried=len(carried)),
        grid=(t // tm,),
        in_specs=[pl.BlockSpec((tm, D_MODEL), lambda i: (i, 0)),
                  pl.BlockSpec((1, D_MODEL), lambda i: (0, 0)),
                  pl.BlockSpec((None, D_MODEL, _IN_TOTAL), lambda i: (layer, 0, 0))]
                 + [pl.BlockSpec(memory_space=pl.ANY)] * len(carried),
        out_specs=[o[0] for o in outs],
        out_shape=[o[1] for o in outs],
        input_output_aliases={n_fixed + n: _STATE_OUTS[n] for n in range(len(carried))},
        compiler_params=_cparams(("parallel",)),
        name="in_proj",
    )(x, g.reshape(1, D_MODEL), w_perm, *carried)


def _logf_cumsum_body(v_ref, b_ref, logf_ref, cum_ref, *, plen, rows):
    vals = v_ref[0, 0]
    pos = lax.broadcasted_iota(jnp.int32, (rows, LANES), 0) * LANES + lax.broadcasted_iota(jnp.int32, (rows, LANES), 1)
    logf = jnp.where(pos >= plen, jax.nn.log_sigmoid(vals + b_ref[0]), vals)
    logf_ref[0, 0] = logf
    kk = lax.broadcasted_iota(jnp.int32, (LANES, LANES), 0)
    jj = lax.broadcasted_iota(jnp.int32, (LANES, LANES), 1)
    in_row = _dot(logf, (kk <= jj).astype(F32), HI)
    tot = jnp.broadcast_to(in_row[:, LANES - 1:LANES], (rows, LANES))
    ri = lax.broadcasted_iota(jnp.int32, (rows, rows), 0)
    rj = lax.broadcasted_iota(jnp.int32, (rows, rows), 1)
    cum_ref[0, 0] = in_row + _dot((rj < ri).astype(F32), tot, HI)


def _logf_cumsum(vals, bias, plen):
    b, h, rows, _ = vals.shape
    spec = pl.BlockSpec((1, 1, rows, LANES), lambda i, j: (i, j, 0, 0))
    return pl.pallas_call(
        functools.partial(_logf_cumsum_body, plen=plen, rows=rows),
        grid=(b, h),
        in_specs=[spec, pl.BlockSpec((1, 1, 1), lambda i, j: (j, 0, 0))],
        out_specs=[spec, spec],
        out_shape=[jax.ShapeDtypeStruct(vals.shape, F32)] * 2,
        compiler_params=_cparams(("parallel", "parallel")),
        name="logf_cumsum",
    )(vals, bias.reshape(h, 1, 1))


LOG2E = math.log2(math.e)
FAR = 'far'


def _sweep_plan(kinds_per_q):
    n_pairs, sig_id, sigs = [], [], []
    for kinds in kinds_per_q:
        n_far = 0
        while n_far < len(kinds) and kinds[n_far] == FAR:
            n_far += 1
        pairs = min(n_far, len(kinds) - 1) // 2
        tail = tuple(kinds[2 * pairs:])
        if tail not in sigs:
            sigs.append(tail)
        n_pairs.append(pairs)
        sig_id.append(sigs.index(tail))
    return np.asarray(n_pairs, np.int32), np.asarray(sig_id, np.int32), sigs


def _kv_sweep(n_pairs, sig, sigs, qk, consume, bufs_a, bufs_b):
    def put(bufs, tiles):
        for buf, tile in zip(bufs, tiles):
            buf[...] = tile

    def get(bufs):
        return tuple(buf[...] for buf in bufs)

    put(bufs_a, qk(0))

    def pair(t, carry):
        j = 2 * t
        put(bufs_b, qk(j + 1))
        consume(get(bufs_a), j, FAR)
        put(bufs_a, qk(j + 2))
        consume(get(bufs_b), j + 1, FAR)
        return carry

    lax.fori_loop(0, n_pairs, pair, 0)
    base = 2 * n_pairs
    for sid, tail in enumerate(sigs):
        @pl.when(sig == sid)
        def _():
            cur, nxt = bufs_a, bufs_b
            for off, kind in enumerate(tail):
                if off + 1 < len(tail):
                    put(nxt, qk(base + off + 1))
                consume(get(cur), base + off, kind)
                cur, nxt = nxt, cur


def _softmax_block(s, v_blk, m_ref, acc_ref):
    m_old = m_ref[...]
    m_new = jnp.maximum(m_old, jnp.max(s, axis=-1, keepdims=True))
    alpha = jnp.exp2(m_old - m_new)
    bk = s.shape[1]
    m_wide = jnp.concatenate([m_new] * (bk // LANES), axis=1) if bk % LANES == 0 else m_new[:, 0:1]
    p = jnp.exp2(s - m_wide)
    acc_ref[...] = alpha * acc_ref[...] + _dot(p.astype(BF16), v_blk)
    m_ref[...] = m_new


def _with_ones(v):
    b, h, s, d = v.shape
    return jnp.concatenate([v, jnp.ones((b, h, s, 1), v.dtype), jnp.zeros((b, h, s, LANES - d - 1), v.dtype)],
                           axis=-1).astype(BF16)


def _t5_bucket(rel):
    half = NUM_BUCKETS // 2
    max_exact = half // 2
    ret = jnp.where(rel > 0, half, 0)
    n = jnp.abs(rel)
    nf = jnp.maximum(n, 1).astype(F32)
    large = max_exact + (jnp.log(nf / max_exact) / math.log(REL_MAX_DIST / max_exact)
                         * (half - max_exact)).astype(jnp.int32)
    large = jnp.minimum(large, half - 1)
    return ret + jnp.where(n < max_exact, n, large)


def _bucket_bias(rel_bias, rel):
    bucket = _t5_bucket(rel)
    table = rel_bias.astype(F32)
    out = jnp.zeros((table.shape[1],) + rel.shape, F32)
    for b in range(NUM_BUCKETS):
        out = jnp.where(bucket[None] == b, table[b].reshape((-1,) + (1,) * rel.ndim), out)
    return out


def _diff_plan(nq, nk, plen, bq, bk):
    far = bk + 2 * REL_MAX_DIST
    kinds_per_q, deltas = [], []
    for i in range(nq):
        q_hi = (plen + i * bq + bq - 1) // CHUNK
        kinds = []
        for j in range(nk):
            if (j * bk) // CHUNK > q_hi:
                break
            d = plen + i * bq - j * bk
            if d >= far:
                kinds.append(FAR)
            else:
                if d not in deltas:
                    deltas.append(d)
                kinds.append(('tile', deltas.index(d)))
        kinds_per_q.append(kinds)
    return kinds_per_q, deltas


def _diff_bias_tiles(rel_bias, deltas, bq, bk):
    r = jnp.arange(bq, dtype=jnp.int32)[:, None]
    c = jnp.arange(bk, dtype=jnp.int32)[None, :]
    far_bias = _bucket_bias(rel_bias, jnp.full((1, 1), -4 * REL_MAX_DIST, jnp.int32))
    tiles = []
    for d in deltas:
        bias = (_bucket_bias(rel_bias, c - r - d) - far_bias) * LOG2E
        vis = (c // CHUNK) <= ((r + d) // CHUNK)
        tiles.append(jnp.where(vis[None], bias, NEG_BIG))
    return jnp.stack(tiles, axis=0)


def _diff_attn_body(pairs_ref, sig_ref, q_ref, k_ref, v_ref, bias_ref, lam_ref, g_ref, o_ref,
                    sa1, sa2, sb1, sb2, m1, acc1, m2, acc2, *, bk, sigs, lam_init):
    i = pl.program_id(2)
    dv = A_V_DIM
    q = q_ref[0, 0] * (A_QK_DIM ** -0.5 * LOG2E)
    lane = lax.broadcasted_iota(jnp.int32, (1, 2 * A_QK_DIM), 1)
    q1 = jnp.where(lane < A_QK_DIM, q, 0.0).astype(BF16)
    q2 = jnp.where(lane >= A_QK_DIM, q, 0.0).astype(BF16)
    for m, acc in ((m1, acc1), (m2, acc2)):
        m[...] = jnp.full(m.shape, -jnp.inf, F32)
        acc[...] = jnp.zeros(acc.shape, F32)

    def rows(j):
        return pl.ds(j * bk if isinstance(j, int) else pl.multiple_of(j * bk, bk), bk)

    def qk(j):
        kb = k_ref[0, 0, rows(j), :]
        return _dot_nt(q1, kb), _dot_nt(q2, kb)

    def consume(tiles, j, kind):
        vb = v_ref[0, 0, rows(j), :]
        for s, m, acc in zip(tiles, (m1, m2), (acc1, acc2)):
            if kind != FAR:
                s = s + bias_ref[kind[1], 0]
            _softmax_block(s, vb, m, acc)

    _kv_sweep(pairs_ref[i], sig_ref[i], sigs, qk, consume, (sa1, sa2), (sb1, sb2))

    lp = lam_ref[...]
    lam = (jnp.exp(jnp.sum(lp[0:1] * lp[1:2], axis=-1, keepdims=True))
           - jnp.exp(jnp.sum(lp[2:3] * lp[3:4], axis=-1, keepdims=True)) + lam_init)
    a1 = acc1[...]
    a2 = acc2[...]
    o = a1[:, 0:dv] / a1[:, dv:dv + 1] - lam * (a2[:, 0:dv] / a2[:, dv:dv + 1])
    o = o * lax.rsqrt(jnp.mean(o * o, axis=-1, keepdims=True) + EPS) * g_ref[...]
    o_ref[0, 0] = o * (1.0 - lam_init)


def _diff_attn(q, k, v_aug, rel_bias, a_lambda, a_norm_g, plen, bq, bk, lam_init):
    b, h, sq, dq = q.shape
    sk = k.shape[2]
    dv = A_V_DIM
    nq, nk = sq // bq, sk // bk
    kinds, deltas = _diff_plan(nq, nk, plen, bq, bk)
    n_pairs, sig_id, sigs = _sweep_plan(kinds)
    tiles = _diff_bias_tiles(rel_bias, deltas, bq, bk)
    nt = tiles.shape[0]
    per_head = lambda shape: pl.BlockSpec((1, 1) + shape, lambda b_, h_, i, *_: (b_, h_, 0, 0))
    grid_spec = pltpu.PrefetchScalarGridSpec(
        num_scalar_prefetch=2,
        grid=(b, h, nq),
        in_specs=[pl.BlockSpec((1, 1, bq, dq), lambda b_, h_, i, *_: (b_, h_, i, 0)),
                  per_head((sk, dq)), per_head((sk, LANES)),
                  pl.BlockSpec((nt, 1, bq, bk), lambda b_, h_, i, *_: (0, h_, 0, 0)),
                  pl.BlockSpec((4, A_QK_DIM), lambda b_, h_, i, *_: (0, 0)),
                  pl.BlockSpec((1, dv), lambda b_, h_, i, *_: (0, 0))],
        out_specs=pl.BlockSpec((1, 1, bq, dv), lambda b_, h_, i, *_: (b_, h_, i, 0)),
        scratch_shapes=[pltpu.VMEM((bq, bk), F32)] * 4 + [pltpu.VMEM((bq, LANES), F32)] * 4,
    )
    return pl.pallas_call(
        functools.partial(_diff_attn_body, bk=bk, sigs=sigs, lam_init=lam_init),
        grid_spec=grid_spec,
        out_shape=jax.ShapeDtypeStruct((b, h, sq, dv), F32),
        compiler_params=_cparams(("parallel", "parallel", "arbitrary")),
        name="diff_attn",
    )(jnp.asarray(n_pairs), jnp.asarray(sig_id), q, k, v_aug, tiles, a_lambda, a_norm_g.reshape(1, dv))


MASK = 'mask'


def _forget_attn_body(pairs_ref, sig_ref, q_ref, k_ref, v_ref, ck_ref, cref_ref, o_ref, sa, sb, m, acc,
                      *, plen, bq, bk, sigs):
    i = pl.program_id(2)
    d = B_HEAD_DIM
    q = (q_ref[0, 0] * (d ** -0.5 * LOG2E)).astype(BF16)
    cref = cref_ref[0, 0, 0]
    m[...] = jnp.full(m.shape, -jnp.inf, F32)
    acc[...] = jnp.zeros(acc.shape, F32)

    def rows(j):
        return pl.ds(j * bk if isinstance(j, int) else pl.multiple_of(j * bk, bk), bk)

    def qk(j):
        e = (ck_ref[0, 0, pl.ds(j, 1), :] - cref) * LOG2E
        return (_dot_nt(q, k_ref[0, 0, rows(j), :]) - e,)

    def consume(tiles, j, kind):
        s = tiles[0]
        if kind == MASK:
            kpos = j * bk + lax.broadcasted_iota(jnp.int32, (bq, bk), 1)
            qpos = plen + i * bq + lax.broadcasted_iota(jnp.int32, (bq, bk), 0)
            s = jnp.where(kpos <= qpos, s, NEG_BIG)
        _softmax_block(s, v_ref[0, 0, rows(j), :], m, acc)

    _kv_sweep(pairs_ref[i], sig_ref[i], sigs, qk, consume, (sa,), (sb,))
    a = acc[...]
    o_ref[0, 0] = a[:, 0:d] / a[:, d:d + 1]


def _forget_attn(q, k, v_aug, cum, plen, bq, bk):
    b, h, sq, d = q.shape
    sk = k.shape[2]
    nq, nk = sq // bq, sk // bk
    kinds = []
    for i in range(nq):
        q_lo, q_hi = plen + i * bq, plen + i * bq + bq - 1
        kinds.append([FAR if j * bk + bk - 1 <= q_lo else MASK for j in range(nk) if j * bk <= q_hi])
    n_pairs, sig_id, sigs = _sweep_plan(kinds)
    cref = cum[:, :, plen:plen + sq:bq].reshape(b, h, nq, 1, 1)
    ck = cum.reshape(b, h, nk, bk)
    per_head = lambda shape: pl.BlockSpec((1, 1) + shape, lambda b_, h_, i, *_: (b_, h_, 0, 0))
    grid_spec = pltpu.PrefetchScalarGridSpec(
        num_scalar_prefetch=2,
        grid=(b, h, nq),
        in_specs=[pl.BlockSpec((1, 1, bq, d), lambda b_, h_, i, *_: (b_, h_, i, 0)),
                  per_head((sk, d)), per_head((sk, LANES)), per_head((nk, bk)),
                  pl.BlockSpec((1, 1, 1, 1, 1), lambda b_, h_, i, *_: (b_, h_, i, 0, 0))],
        out_specs=pl.BlockSpec((1, 1, bq, d), lambda b_, h_, i, *_: (b_, h_, i, 0)),
        scratch_shapes=[pltpu.VMEM((bq, bk), F32)] * 2 + [pltpu.VMEM((bq, LANES), F32)] * 2,
    )
    return pl.pallas_call(
        functools.partial(_forget_attn_body, plen=plen, bq=bq, bk=bk, sigs=sigs),
        grid_spec=grid_spec,
        out_shape=jax.ShapeDtypeStruct((b, h, sq, d), F32),
        compiler_params=_cparams(("parallel", "parallel", "arbitrary")),
        name="forget_attn",
    )(jnp.asarray(n_pairs), jnp.asarray(sig_id), q, k, v_aug, ck, cref)


_CONV_PAD = 8


def _causal_conv_tile(buf, x, cw_ref, rows):
    buf[_CONV_PAD:_CONV_PAD + rows, :] = x
    lo = _CONV_PAD - (CONV_W - 1)
    y = buf[lo:lo + rows, :] * cw_ref[0:1, :]
    for j in range(1, CONV_W):
        y = y + buf[lo + j:lo + j + rows, :] * cw_ref[j:j + 1, :]
    buf[lo:_CONV_PAD, :] = x[rows - (CONV_W - 1):rows, :]
    return y


def _rg_lru_body(x_ref, gate_ref, h0_ref, conv0_ref, cw_ref, cb_ref, wa_ref, ba_ref, wx_ref, bx_ref, lam_ref,
                 y_ref, hn_ref, convn_ref, buf, hcar, *, ts):
    n = pl.program_id(1)

    @pl.when(n == 0)
    def _():
        buf[_CONV_PAD - (CONV_W - 1):_CONV_PAD, :] = conv0_ref[0]
        hcar[...] = h0_ref[0]

    x = x_ref[0]
    xc = _causal_conv_tile(buf, x, cw_ref, ts) + cb_ref[...]
    r = jax.nn.sigmoid(_dot(xc, wa_ref[...], HI) + ba_ref[...])
    i = jax.nn.sigmoid(_dot(xc, wx_ref[...], HI) + bx_ref[...])
    log_a = -C_POWER * r * jax.nn.softplus(-lam_ref[...])
    a = jnp.exp(log_a)
    th = jnp.tanh(log_a)
    b = jnp.sqrt(-2.0 * th / (1.0 - th)) * i * xc
    row = lax.broadcasted_iota(jnp.int32, a.shape, 0)
    d = 1
    while d < ts:
        keep = row >= d
        a_sh = jnp.where(keep, pltpu.roll(a, d, 0), 1.0)
        b_sh = jnp.where(keep, pltpu.roll(b, d, 0), 0.0)
        b = a * b_sh + b
        a = a * a_sh
        d *= 2
    h = a * hcar[...] + b
    y_ref[0] = h * jax.nn.gelu(gate_ref[0])
    hcar[...] = h[ts - 1:ts, :]

    @pl.when(n == pl.num_programs(1) - 1)
    def _():
        hn_ref[0] = h[ts - 1:ts, :]
        convn_ref[0] = x[ts - (CONV_W - 1):ts, :]


def _block_diag(w):
    n, d, _ = w.shape
    eye = jnp.eye(n, dtype=w.dtype)
    return (eye[:, None, :, None] * w[:, :, None, :]).reshape(n * d, n * d)


def _rg_lru(x, gate, h0, conv0, p, ts):
    b, s, w = x.shape
    row = lambda a: a.reshape(1, w)
    tile = pl.BlockSpec((1, ts, w), lambda i, n: (i, n, 0))
    const = lambda shape: pl.BlockSpec(shape, lambda i, n: (0,) * len(shape))
    return pl.pallas_call(
        functools.partial(_rg_lru_body, ts=ts),
        grid=(b, s // ts),
        in_specs=[tile, tile,
                  pl.BlockSpec((1, 1, w), lambda i, n: (i, 0, 0)),
                  pl.BlockSpec((1, CONV_W - 1, w), lambda i, n: (i, 0, 0)),
                  const((CONV_W, w)), const((1, w)), const((w, w)), const((1, w)), const((w, w)), const((1, w)),
                  const((1, w))],
        out_specs=[tile,
                   pl.BlockSpec((1, 1, w), lambda i, n: (i, 0, 0)),
                   pl.BlockSpec((1, CONV_W - 1, w), lambda i, n: (i, 0, 0))],
        out_shape=[jax.ShapeDtypeStruct((b, s, w), F32), jax.ShapeDtypeStruct((b, 1, w), F32),
                   jax.ShapeDtypeStruct((b, CONV_W - 1, w), F32)],
        scratch_shapes=[pltpu.VMEM((_CONV_PAD + ts, w), F32), pltpu.VMEM((1, w), F32)],
        compiler_params=_cparams(("parallel", "arbitrary")),
        name="rg_lru",
    )(x, gate, h0.reshape(b, 1, w), conv0, p['c_conv_w'], row(p['c_conv_b']),
      _block_diag(p['c_gate_a_w']), row(p['c_gate_a_b']), _block_diag(p['c_gate_x_w']), row(p['c_gate_x_b']),
      row(p['c_lambda']))


def _gdn_body(qkv_ref, z_ref, bl_ref, al_ref, s0_ref, conv0_ref, cw_ref, alog_ref, dtb_ref, ng_ref,
              o_ref, sn_ref, convn_ref, buf, st, *, c, per_step, tail):
    n = pl.program_id(1)
    w = D_WIDTH
    hd = D_HEAD_DIM

    @pl.when(n == 0)
    def _():
        buf[_CONV_PAD - (CONV_W - 1):_CONV_PAD, :] = conv0_ref[0]
        st[...] = s0_ref[0]

    rows = per_step * c
    x = qkv_ref[0]
    y = jax.nn.silu(_causal_conv_tile(buf, x, cw_ref, rows))
    q, k, v = y[:, 0:w], y[:, w:2 * w], y[:, 2 * w:3 * w]

    same_head = (lax.broadcasted_iota(jnp.int32, (w, w), 0) // hd) == (lax.broadcasted_iota(jnp.int32, (w, w), 1) // hd)
    head_sum = same_head.astype(BF16)

    def head_sums(x2):
        hi, lo = _split_bf16(x2)
        return _dot(hi, head_sum) + _dot(lo, head_sum)

    def bd(a):
        return jnp.where(same_head, jnp.concatenate([a] * D_HEADS, axis=0), 0.0)

    def dot3_bd(a, b):
        return _dot3(a, bd(b))

    q = q * lax.rsqrt(head_sums(q * q) + EPS) * (hd ** -0.5)
    k = k * lax.rsqrt(head_sums(k * k) + EPS)
    beta = jax.nn.sigmoid(bl_ref[0])
    g = -jnp.exp(alog_ref[...]) * jax.nn.softplus(al_ref[0] + dtb_ref[...])
    if tail < rows:
        real = lax.broadcasted_iota(jnp.int32, (rows, w), 0) < jnp.where(n == pl.num_programs(1) - 1, tail, rows)
        beta = jnp.where(real, beta, 0.0)
        g = jnp.where(real, g, 0.0)

    ti = lax.broadcasted_iota(jnp.int32, (c, w), 0)
    tj = lax.broadcasted_iota(jnp.int32, (c, w), 1) % hd
    incl = ti >= tj
    strict = ti > tj
    diag = ti == tj
    ci = lax.broadcasted_iota(jnp.int32, (c, c), 0)
    cj = lax.broadcasted_iota(jnp.int32, (c, c), 1)
    lower_ones = (ci >= cj).astype(BF16)
    all_ones = jnp.ones((c, c), BF16)

    def chunk_local(lo):
        sl = slice(lo, lo + c)
        qc, kc, vc, bc = q[sl], k[sl], v[sl], beta[sl]
        gcum = _dot_exact_left(lower_ones, g[sl])
        g_row = _dot_exact_left(all_ones, jnp.where(diag, gcum, 0.0))
        decay = jnp.where(incl, jnp.exp(gcum - g_row), 0.0)
        eg = jnp.exp(gcum)
        kb = kc * bc
        g_last = gcum[c - 1:c, :]
        k_t = jnp.where(same_head, _dot_tn(kc.astype(BF16), diag.astype(BF16)), 0.0).astype(BF16)
        gram = _dot(jnp.concatenate([kb, qc], axis=0).astype(BF16), k_t)
        low = jnp.where(strict, gram[0:c] * decay, 0.0)
        inv = diag.astype(F32) - low
        pw = low
        span = 2
        while span < c:
            pw = dot3_bd(pw, pw)
            inv = inv + dot3_bd(inv, pw)
            span *= 2
        return dict(v_w=dot3_bd(inv, vc * bc), k_w=dot3_bd(inv, kb * eg), attn=gram[c:2 * c] * decay,
                    q_g=qc * eg, k_d=kc * jnp.exp(g_last - gcum), g_last=g_last)

    local = [chunk_local(i * c) for i in range(per_step)]
    state = st[...]
    outs = []
    for ch in local:
        through = _dot3(jnp.concatenate([ch['k_w'], ch['q_g']], axis=0), state)
        v_new = ch['v_w'] - through[0:c]
        outs.append(through[c:2 * c] + dot3_bd(ch['attn'], v_new))
        kd_hi, kd_lo = _split_bf16(ch['k_d'])
        vn_hi, vn_lo = _split_bf16(v_new)
        outer = _dot_tn(kd_hi, vn_hi) + (_dot_tn(kd_hi, vn_lo) + _dot_tn(kd_lo, vn_hi))
        state = state * jnp.exp(ch['g_last']) + jnp.where(same_head, outer, 0.0)
    st[...] = state

    o = outs[0] if per_step == 1 else jnp.concatenate(outs, axis=0)
    o = o * lax.rsqrt(head_sums(o * o) * (1.0 / hd) + EPS) * ng_ref[...]
    o_ref[0] = o * jax.nn.silu(z_ref[0])

    @pl.when(n == pl.num_programs(1) - 1)
    def _():
        sn_ref[0] = state
        convn_ref[0] = x[tail - (CONV_W - 1):tail, :]


def _gdn(qkv, z, beta_logit, alpha_logit, s0, conv0, p):
    b, s_true, w3 = qkv.shape
    w = D_WIDTH
    c = D_HEAD_DIM
    per_step = 2 if s_true >= 2 * c else 1
    rows = per_step * c
    s = -(-s_true // rows) * rows
    tail = s_true - (s - rows)
    assert tail >= CONV_W - 1
    if s != s_true:
        pad = lambda a: jnp.pad(a, ((0, 0), (0, s - s_true), (0, 0)))
        qkv, z, beta_logit, alpha_logit = pad(qkv), pad(z), pad(beta_logit), pad(alpha_logit)
    s0_bd = jax.vmap(_block_diag)(s0)
    per_lane = lambda a: jnp.repeat(a, D_HEAD_DIM).reshape(1, w)
    tile = lambda n_: pl.BlockSpec((1, rows, n_), lambda i, n: (i, n, 0))
    const = lambda shape: pl.BlockSpec(shape, lambda i, n: (0,) * len(shape))
    per_b = lambda shape: pl.BlockSpec((1,) + shape, lambda i, n: (i,) + (0,) * len(shape))
    o, sn, convn = pl.pallas_call(
        functools.partial(_gdn_body, c=c, per_step=per_step, tail=tail),
        grid=(b, s // rows),
        in_specs=[tile(w3), tile(w), tile(w), tile(w), per_b((w, w)), per_b((CONV_W - 1, w3)),
                  const((CONV_W, w3)), const((1, w)), const((1, w)), const((1, w))],
        out_specs=[tile(w), per_b((w, w)), per_b((CONV_W - 1, w3))],
        out_shape=[jax.ShapeDtypeStruct((b, s, w), F32), jax.ShapeDtypeStruct((b, w, w), F32),
                   jax.ShapeDtypeStruct((b, CONV_W - 1, w3), F32)],
        scratch_shapes=[pltpu.VMEM((_CONV_PAD + rows, w3), F32), pltpu.VMEM((w, w), F32)],
        compiler_params=_cparams(("parallel", "arbitrary")),
        name="gdn",
    )(qkv, z, beta_logit, alpha_logit, s0_bd, conv0, p['d_conv_w'], per_lane(p['d_a_log']), per_lane(p['d_dt_bias']),
      jnp.tile(p['d_norm_g'], D_HEADS).reshape(1, w))
    hd = D_HEAD_DIM
    sn = jnp.stack([sn[:, h * hd:(h + 1) * hd, h * hd:(h + 1) * hd] for h in range(D_HEADS)], axis=1)
    return o[:, :s_true], sn, convn


_PEER_SLABS = 2 * PEER_HEADS


def _out_proj_body(x_ref, oa_ref, ob_ref, oc_ref, od_ref, wo_ref, g_ref, wq_ref, sk_ref,
                   x2_ref, h2t_ref, st_ref):
    x2 = x_ref[...]
    for grp, o_ref in enumerate((oa_ref, ob_ref)):
        for hd_ in range(_ATTN_HEADS):
            lo = grp * GROUP_WIDTH + hd_ * _ATTN_DIM
            x2 = x2 + _dot(o_ref[hd_].astype(BF16), wo_ref[lo:lo + _ATTN_DIM, :])
    for grp, o_ref in ((2, oc_ref), (3, od_ref)):
        x2 = x2 + _dot(o_ref[...].astype(BF16), wo_ref[grp * GROUP_WIDTH:(grp + 1) * GROUP_WIDTH, :])
    x2_ref[...] = x2
    h2 = x2 * lax.rsqrt(jnp.mean(x2 * x2, axis=-1, keepdims=True) + EPS) * g_ref[...]
    h2t_ref[...] = h2.T.astype(BF16)
    q = _dot(h2.astype(BF16), wq_ref[...]).astype(BF16)
    for slab in range(_PEER_SLABS):
        st_ref[slab] = _dot_nt(sk_ref[slab % 2], q[:, slab * PEER_HALF:(slab + 1) * PEER_HALF])


def _out_proj(x, o_a, o_b, o_c, o_d, w_out, norm2_g, wq, subkeys, layer, tm):
    t = x.shape[0]
    tile = lambda n: pl.BlockSpec((tm, n), lambda i: (i, 0))
    heads = pl.BlockSpec((_ATTN_HEADS, tm, _ATTN_DIM), lambda i: (0, i, 0))
    const = lambda shape: pl.BlockSpec(shape, lambda i: (0,) * len(shape))
    of_layer = lambda shape: pl.BlockSpec((None,) + shape, lambda i: (layer,) + (0,) * len(shape))
    nq = PEER_HEADS * PEER_QUERY_DIM
    return pl.pallas_call(
        _out_proj_body,
        grid=(t // tm,),
        in_specs=[tile(D_MODEL), heads, heads, tile(GROUP_WIDTH), tile(GROUP_WIDTH),
                  of_layer((D_MODEL, D_MODEL)), const((1, D_MODEL)), of_layer((D_MODEL, nq)),
                  of_layer((2, PEER_N_KEYS, PEER_HALF))],
        out_specs=[tile(D_MODEL), pl.BlockSpec((D_MODEL, tm), lambda i: (0, i)),
                   pl.BlockSpec((_PEER_SLABS, PEER_N_KEYS, tm), lambda i: (0, 0, i))],
        out_shape=[jax.ShapeDtypeStruct((t, D_MODEL), F32), jax.ShapeDtypeStruct((D_MODEL, t), BF16),
                   jax.ShapeDtypeStruct((_PEER_SLABS, PEER_N_KEYS, t), F32)],
        compiler_params=_cparams(("parallel",)),
        name="out_proj",
    )(x, o_a, o_b, o_c, o_d, w_out, norm2_g.reshape(1, D_MODEL), wq, subkeys)


_NO_RANK = 64.0


def _top_values(x, count, with_rank=False):
    vals = []
    rank = jnp.full(x.shape, _NO_RANK, F32) if with_rank else None
    for k in range(count):
        m = jnp.max(x, axis=0, keepdims=True)
        vals.append(m)
        hit = x == m
        if with_rank:
            rank = jnp.where(hit, float(k), rank)
        x = jnp.where(hit, -jnp.inf, x)
    return vals, x, rank


def _peer_select_body(s_ref, c_ref, cnt_ref, r2_ref, e2_ref):
    s1 = s_ref[0]
    s2 = s_ref[1]
    n = s1.shape[1]
    v1, rest1, _ = _top_values(s1, PEER_TOPK)
    v2, _, rank2 = _top_values(s2, PEER_TOPK, with_rank=True)
    sel1 = rest1 == -jnp.inf
    v1_all = jnp.concatenate(v1, axis=0)
    v2_all = jnp.concatenate(v2, axis=0)
    row8 = lax.broadcasted_iota(jnp.int32, (8, n), 0)
    cand = [v1[0] + v2_all, v1[1] + v2_all[0:8]]
    for i in range(2, 8):
        cand.append(jnp.where(row8 < PEER_TOPK // (i + 1), v1[i] + v2_all[0:8], -jnp.inf))
    cand.append(v1_all[8:16] + v2[0])
    top, _, _ = _top_values(jnp.concatenate(cand, axis=0), PEER_TOPK)
    tau = top[PEER_TOPK - 1]
    z = jnp.ones_like(tau)
    for t in top[1:]:
        z = z + jnp.exp(t - top[0])
    count_sorted = jnp.zeros((PEER_TOPK, n), F32)
    for j in range(PEER_TOPK):
        count_sorted = count_sorted + jnp.where(v1_all + v2[j] >= tau, 1.0, 0.0)
    count = jnp.zeros(s1.shape, F32)
    for i in range(PEER_TOPK):
        count = jnp.where(s1 == v1[i], count_sorted[i:i + 1, :], count)
    cnt_ref[0] = count
    c_ref[0] = jnp.where(sel1, jnp.exp(s1 - v1[0]), 0.0) / z
    r2_ref[0] = rank2.astype(BF16)
    e2_ref[0] = jnp.where(rank2 < _NO_RANK, jnp.exp(s2 - v2[0]), 0.0).astype(BF16)


def _peer_select(st, tm):
    t = st.shape[2]
    out_spec = pl.BlockSpec((1, PEER_N_KEYS, tm), lambda i, h: (h, 0, i))
    f32 = jax.ShapeDtypeStruct((PEER_HEADS, PEER_N_KEYS, t), F32)
    bf16 = jax.ShapeDtypeStruct((PEER_HEADS, PEER_N_KEYS, t), BF16)
    return pl.pallas_call(
        _peer_select_body,
        grid=(t // tm, PEER_HEADS),
        in_specs=[pl.BlockSpec((2, PEER_N_KEYS, tm), lambda i, h: (h, 0, i))],
        out_specs=[out_spec] * 4,
        out_shape=[f32, f32, bf16, bf16],
        compiler_params=_cparams(("parallel", "parallel")),
        name="peer_select",
    )(st)


_PEER_GROUP = 4


def _peer_dense_body(ht_ref, u_ref, vt_ref, c_ref, cnt_ref, r2_ref, e2_ref, o_ref, act, pbuf, *, na):
    e = pl.program_id(1)
    ht = ht_ref[...]
    tm = ht.shape[1]
    rows_g = _PEER_GROUP * PEER_N_KEYS
    groups = na // _PEER_GROUP
    packed_rows = 16

    def span(g):
        return slice(g * rows_g, (g + 1) * rows_g)

    def activations(g):
        act[span(g), :] = _dot(u_ref[span(g), :], ht)

    def over_keys(row):
        packed = jnp.broadcast_to(row, (packed_rows, tm)).astype(BF16)
        return jnp.concatenate([packed] * (PEER_N_KEYS // packed_rows), axis=0)

    def weights(g):
        for a in range(g * _PEER_GROUP, (g + 1) * _PEER_GROUP):
            rows = slice(a * PEER_N_KEYS, (a + 1) * PEER_N_KEYS)
            w = None
            for h in range(PEER_HEADS):
                chosen = r2_ref[h] < over_keys(cnt_ref[h, a:a + 1, :])
                term = jnp.where(chosen, e2_ref[h], 0.0) * over_keys(c_ref[h, a:a + 1, :])
                w = term if w is None else w + term
            pbuf[rows, :] = w * jax.nn.gelu(act[rows, :].astype(BF16))

    @pl.when(e == 0)
    def _():
        o_ref[...] = jnp.zeros(o_ref.shape, F32)

    activations(0)
    contrib = None
    for g in range(groups):
        if g + 1 < groups:
            activations(g + 1)
        weights(g)
        part = _dot(vt_ref[:, span(g)], pbuf[span(g), :])
        contrib = part if contrib is None else contrib + part
    o_ref[...] += contrib


def _peer_dense(ht, u, vt, c, count, rank2, e2, layer, tm, eb):
    t = ht.shape[1]
    na = eb // PEER_N_KEYS
    key_rows = pl.BlockSpec((PEER_HEADS, na, tm), lambda i, e: (0, e, i))
    all_keys = pl.BlockSpec((PEER_HEADS, PEER_N_KEYS, tm), lambda i, e: (0, 0, i))
    return pl.pallas_call(
        functools.partial(_peer_dense_body, na=na),
        grid=(t // tm, PEER_EXPERTS // eb),
        in_specs=[pl.BlockSpec((D_MODEL, tm), lambda i, e: (0, i)),
                  pl.BlockSpec((None, eb, D_MODEL), lambda i, e: (layer, e, 0)),
                  pl.BlockSpec((None, D_MODEL, eb), lambda i, e: (layer, 0, e)),
                  key_rows, key_rows, all_keys, all_keys],
        out_specs=pl.BlockSpec((D_MODEL, tm), lambda i, e: (0, i)),
        out_shape=jax.ShapeDtypeStruct((D_MODEL, t), F32),
        scratch_shapes=[pltpu.VMEM((eb, tm), F32), pltpu.VMEM((eb, tm), BF16)],
        compiler_params=_cparams(("parallel", "arbitrary")),
        name="peer_dense",
    )(ht, u, vt, c, count, rank2, e2)


def _residual_body(x_ref, pt_ref, g_ref, o_ref, *, final_norm):
    x = x_ref[...] + pt_ref[...].T
    if final_norm:
        x = x * lax.rsqrt(jnp.mean(x * x, axis=-1, keepdims=True) + EPS) * g_ref[...]
    o_ref[...] = x


def _residual(x, pt, g, tm, final_norm):
    t = x.shape[0]
    return pl.pallas_call(
        functools.partial(_residual_body, final_norm=final_norm),
        grid=(t // tm,),
        in_specs=[pl.BlockSpec((tm, D_MODEL), lambda i: (i, 0)),
                  pl.BlockSpec((D_MODEL, tm), lambda i: (0, i)),
                  pl.BlockSpec((1, D_MODEL), lambda i: (0, 0))],
        out_specs=pl.BlockSpec((tm, D_MODEL), lambda i: (i, 0)),
        out_shape=jax.ShapeDtypeStruct((t, D_MODEL), F32),
        compiler_params=_cparams(("parallel",)),
        name="residual",
    )(x, pt, g.reshape(1, D_MODEL))


def _heads_first(a):
    return jnp.transpose(a, (0, 2, 1, 3))


def _prep_weights(w_in, w_out, peer_wq, peer_subkeys, peer_u, peer_v):
    return {
        'w_in': _permute_w_in(w_in),
        'w_out': w_out.astype(BF16),
        'wq': peer_wq.astype(BF16),
        'sk': peer_subkeys.astype(BF16),
        'u': peer_u.astype(BF16),
        'vt': jnp.transpose(peer_v.astype(BF16), (0, 2, 1)),
    }


def _mixers(x2d, bsz, s, past, p, pw, layer, lam_init, cfg, carried):
    pa_k, pa_v, pb_k, pb_v, pb_logf, c_h0, c_conv0, d_s0, d_conv0 = past
    plen = pa_k.shape[1]
    sk = plen + s
    (a_q, a_k, a_k16, a_v, a_v16, b_q, b_k, b_k16, b_v, b_v16,
     c_x, c_g, d_qkv, d_z, d_beta, d_alpha, small) = _in_proj(x2d, p['norm1_g'], pw['w_in'], layer, cfg['tm_in'], carried)
    bq, bk = cfg['bq'], cfg['bk']

    def seq(a, *tail):
        return a.reshape(bsz, s, *tail)

    def batch_heads(a):
        return jnp.transpose(a.reshape(a.shape[0], bsz, s, a.shape[2]), (1, 0, 2, 3))

    def head_major(a):
        return jnp.transpose(a, (1, 0, 2, 3)).reshape(a.shape[1], bsz * s, a.shape[3])

    def with_past(past_k, past_v, k16, v16):
        keys = jnp.concatenate([_heads_first(past_k).astype(BF16), batch_heads(k16)], axis=2)
        vals = jnp.concatenate([_with_ones(_heads_first(past_v)), batch_heads(v16)], axis=2)
        return keys, vals

    k_all, v_all = with_past(pa_k, pa_v, a_k16, a_v16)
    o_a = head_major(_diff_attn(batch_heads(a_q), k_all, v_all,
                                p['rel_bias'], p['a_lambda'], p['a_norm_g'], plen, bq, bk, lam_init))

    rows = -(-sk // LANES)
    rows = -(-rows // 8) * 8
    f_vals = jnp.concatenate([pb_logf, seq(small[:, 0:B_HEADS], B_HEADS)], axis=1)
    f_vals = jnp.pad(f_vals, ((0, 0), (0, rows * LANES - sk), (0, 0)))
    f_vals = jnp.transpose(f_vals, (0, 2, 1)).reshape(bsz, B_HEADS, rows, LANES)
    logf, cum = _logf_cumsum(f_vals, p['b_forget_bias'], plen)
    cum = cum.reshape(bsz, B_HEADS, rows * LANES)[:, :, :sk]
    b_logf = jnp.transpose(logf.reshape(bsz, B_HEADS, rows * LANES)[:, :, plen:sk], (0, 2, 1))
    kb_all, vb_all = with_past(pb_k, pb_v, b_k16, b_v16)
    o_b = head_major(_forget_attn(batch_heads(b_q), kb_all, vb_all, cum, plen, bq, bk))

    o_c, c_h, c_conv = _rg_lru(seq(c_x, C_WIDTH), seq(c_g, C_WIDTH), c_h0, c_conv0, p, cfg['ts'])

    o_d, d_s, d_conv = _gdn(seq(d_qkv, 3 * D_WIDTH), seq(d_z, D_WIDTH), seq(d_beta, D_WIDTH), seq(d_alpha, D_WIDTH),
                            d_s0, d_conv0, p)

    state = (b_logf, c_h.reshape(bsz, C_WIDTH), c_conv, d_s, d_conv)
    return (o_a, o_b, o_c.reshape(bsz * s, C_WIDTH), o_d.reshape(bsz * s, D_WIDTH)), state, (a_k, a_v, b_k, b_v)


def _layer(x2d, bsz, s, past, p, pw, layer, lam_init, cfg, carried):
    mix, state, carried = _mixers(x2d, bsz, s, past, p, pw, layer, lam_init, cfg, carried)
    x2, h2t, st = _out_proj(x2d, *mix, pw['w_out'], p['norm2_g'], pw['wq'], pw['sk'], layer, cfg['tm_out'])
    c, count, rank2, e2 = _peer_select(st, cfg['tm_sel'])
    peer_t = _peer_dense(h2t, pw['u'], pw['vt'], c, count, rank2, e2, layer, cfg['tm_peer'], cfg['eb'])
    return x2, peer_t, state, carried


_PROMPT_CFG = dict(tm_in=512, bq=1024, bk=512, ts=256, tm_out=256, tm_sel=512, tm_peer=512, eb=2048, tm_res=256)


def _sample_cfg(s, sk):
    return dict(tm_in=512, bq=s, bk=sk, ts=s, tm_out=256, tm_sel=512, tm_peer=512, eb=2048, tm_res=256)


def kernel(x_prompt, x_sample, cache_a_k, cache_a_v, cache_b_k, cache_b_v, cache_b_logf, state_c_h, state_c_conv, state_d_s, state_d_conv, norm1_g, norm2_g, final_norm_g, w_in, w_out, rel_bias, a_lambda, a_norm_g, b_forget_bias, c_conv_w, c_conv_b, c_gate_a_w, c_gate_a_b, c_gate_x_w, c_gate_x_b, c_lambda, d_conv_w, d_a_log, d_dt_bias, d_norm_g, peer_wq, peer_subkeys, peer_u, peer_v):
    pb, ps, _ = x_prompt.shape
    sb, ss, _ = x_sample.shape
    dt = x_prompt.dtype
    xp = x_prompt.reshape(pb * ps, D_MODEL)
    xs = x_sample.reshape(sb * ss, D_MODEL)
    scfg = _sample_cfg(ss, cache_a_k.shape[2] + ss)
    prompt_out, sample_out = [], []
    kv_p, kv_s = (), ()
    pw = _prep_weights(w_in, w_out, peer_wq, peer_subkeys, peer_u, peer_v)
    for l in range(DEPTH):
        p = {
            'norm1_g': norm1_g[l], 'norm2_g': norm2_g[l], 'rel_bias': rel_bias, 'a_lambda': a_lambda[l],
            'a_norm_g': a_norm_g[l], 'b_forget_bias': b_forget_bias[l],
            'c_conv_w': c_conv_w[l], 'c_conv_b': c_conv_b[l],
            'c_gate_a_w': c_gate_a_w[l], 'c_gate_a_b': c_gate_a_b[l],
            'c_gate_x_w': c_gate_x_w[l], 'c_gate_x_b': c_gate_x_b[l], 'c_lambda': c_lambda[l],
            'd_conv_w': d_conv_w[l], 'd_a_log': d_a_log[l], 'd_dt_bias': d_dt_bias[l], 'd_norm_g': d_norm_g[l],
        }
        lam_init = 0.8 - 0.6 * math.exp(-0.3 * l)
        empty = (
            jnp.zeros((pb, 0, A_HEADS, 2 * A_QK_DIM), dt), jnp.zeros((pb, 0, A_HEADS, A_V_DIM), dt),
            jnp.zeros((pb, 0, B_HEADS, B_HEAD_DIM), dt), jnp.zeros((pb, 0, B_HEADS, B_HEAD_DIM), dt),
            jnp.zeros((pb, 0, B_HEADS), dt),
            jnp.zeros((pb, C_WIDTH), dt), jnp.zeros((pb, CONV_W - 1, C_WIDTH), dt),
            jnp.zeros((pb, D_HEADS, D_HEAD_DIM, D_HEAD_DIM), dt), jnp.zeros((pb, CONV_W - 1, 3 * D_WIDTH), dt),
        )
        last = l == DEPTH - 1
        xp2, peer_p, st_p, kv_p = _layer(xp, pb, ps, empty, p, pw, l, lam_init, _PROMPT_CFG, kv_p)
        xp = _residual(xp2, peer_p, final_norm_g, _PROMPT_CFG['tm_res'], last)
        prompt_out.append(st_p)
        past = (cache_a_k[l], cache_a_v[l], cache_b_k[l], cache_b_v[l], cache_b_logf[l],
                state_c_h[l], state_c_conv[l], state_d_s[l], state_d_conv[l])
        xs2, peer_s, st_s, kv_s = _layer(xs, sb, ss, past, p, pw, l, lam_init, scfg, kv_s)
        xs = _residual(xs2, peer_s, final_norm_g, scfg['tm_res'], last)
        sample_out.append(st_s)

    y_prompt = xp.reshape(pb, ps, D_MODEL)
    y_sample = xs.reshape(sb, ss, D_MODEL)
    def leaves(kv, small, bsz, s):
        per_head = [a.reshape(DEPTH, bsz, s, _ATTN_HEADS, _ATTN_DIM) for a in kv]
        return per_head + [jnp.stack(z, axis=0) for z in zip(*small)]

    return (y_prompt, y_sample, *leaves(kv_p, prompt_out, pb, ps), *leaves(kv_s, sample_out, sb, ss))
```

```python
import functools
import math

import numpy as np
import jax
import jax.numpy as jnp
from jax import lax
from jax.experimental import pallas as pl
from jax.experimental.pallas import tpu as pltpu

F32 = jnp.float32
BF16 = jnp.bfloat16
HI = lax.Precision.HIGHEST

D_MODEL = 1024
DEPTH = 2
CHUNK = 64
EPS = 1e-6
CONV_W = 4
GROUP_WIDTH = D_MODEL // 4
A_HEADS = 4
A_QK_DIM = GROUP_WIDTH // (2 * A_HEADS)
A_V_DIM = GROUP_WIDTH // A_HEADS
NUM_BUCKETS = 32
REL_MAX_DIST = 256
B_HEADS = 4
B_HEAD_DIM = GROUP_WIDTH // B_HEADS
C_WIDTH = GROUP_WIDTH
C_BLOCKS = 4
C_BLOCK_DIM = C_WIDTH // C_BLOCKS
C_POWER = 8.0
D_HEADS = 4
D_HEAD_DIM = GROUP_WIDTH // D_HEADS
D_WIDTH = GROUP_WIDTH
PEER_HEADS = 8
PEER_N_KEYS = 128
PEER_EXPERTS = PEER_N_KEYS * PEER_N_KEYS
PEER_TOPK = 16
PEER_QUERY_DIM = 256
PEER_HALF = PEER_QUERY_DIM // 2

LANES = 128
NEG_BIG = -1e30
VMEM_LIMIT = 56 * 1024 * 1024


def _cparams(sem):
    return pltpu.CompilerParams(dimension_semantics=sem, vmem_limit_bytes=VMEM_LIMIT)


def _dot(a, b, precision=None):
    return jnp.dot(a, b, preferred_element_type=F32, precision=precision)


def _dot_nt(a, b, precision=None):
    return lax.dot_general(a, b, (((1,), (1,)), ((), ())), preferred_element_type=F32, precision=precision)


def _dot_tn(a, b, precision=None):
    return lax.dot_general(a, b, (((0,), (0,)), ((), ())), preferred_element_type=F32, precision=precision)


def _split_bf16(x):
    hi = x.astype(BF16)
    return hi, (x - hi.astype(F32)).astype(BF16)


def _dot3(a, b):
    ah, al = _split_bf16(a)
    bh, bl = _split_bf16(b)
    return _dot(ah, bh) + (_dot(ah, bl) + _dot(al, bh))


def _dot_exact_left(a, b):
    b1 = b.astype(BF16)
    r1 = b - b1.astype(F32)
    b2 = r1.astype(BF16)
    b3 = (r1 - b2.astype(F32)).astype(BF16)
    return _dot(a, b1) + (_dot(a, b2) + _dot(a, b3))


_IN_WIDTHS = (256,) * 8 + (768, 256, 256, 256, LANES)
_IN_TOTAL = sum(_IN_WIDTHS)


_IN_RAW = 3084
_IN_ATTN = 1536
_IN_REST = 1536


def _permute_w_in_body(w_ref, o_ref):
    w = w_ref[0]
    rows = w.shape[0]
    o_ref[0, :, 0:_IN_ATTN] = w[:, 0:_IN_ATTN].astype(BF16)
    o_ref[0, :, _IN_ATTN:_IN_ATTN + _IN_REST] = w[:, _IN_ATTN + B_HEADS:_IN_ATTN + B_HEADS + _IN_REST].astype(BF16)
    tail = w[:, _IN_RAW - LANES:_IN_RAW].astype(BF16)
    src = lax.broadcasted_iota(jnp.int32, (LANES, 2 * D_WIDTH), 0)
    dst = lax.broadcasted_iota(jnp.int32, (LANES, 2 * D_WIDTH), 1)
    spread = (src == LANES - 2 * D_HEADS + dst // D_HEAD_DIM).astype(BF16)
    o_ref[0, :, _IN_ATTN + _IN_REST:_IN_ATTN + _IN_REST + 2 * D_WIDTH] = _dot(tail, spread).astype(BF16)
    lane = lax.broadcasted_iota(jnp.int32, (rows, LANES), 1)
    o_ref[0, :, _IN_TOTAL - LANES:_IN_TOTAL] = jnp.where(lane < B_HEADS, w[:, _IN_ATTN:_IN_ATTN + LANES], 0.0).astype(BF16)


def _permute_w_in(w_in):
    depth, rows, _ = w_in.shape
    tr = 256
    return pl.pallas_call(
        _permute_w_in_body,
        grid=(depth, rows // tr),
        in_specs=[pl.BlockSpec((1, tr, _IN_RAW), lambda l, i: (l, i, 0))],
        out_specs=pl.BlockSpec((1, tr, _IN_TOTAL), lambda l, i: (l, i, 0)),
        out_shape=jax.ShapeDtypeStruct((depth, rows, _IN_TOTAL), BF16),
        compiler_params=_cparams(("parallel", "parallel")),
        name="permute_w_in",
    )(w_in)


_ATTN_HEADS = 4
_ATTN_DIM = GROUP_WIDTH // _ATTN_HEADS


_STATE_OUTS = (1, 3, 6, 8)


def _in_proj_body(x_ref, g_ref, w_ref, *rest, n_carried):
    outs = rest[n_carried:]
    x = x_ref[...]
    tm = x.shape[0]
    h = x * lax.rsqrt(jnp.mean(x * x, axis=-1, keepdims=True) + EPS) * g_ref[...]
    hb = h.astype(BF16)
    lane = lax.broadcasted_iota(jnp.int32, (tm, _ATTN_DIM), 1)
    ones_col = jnp.where(lane == 0, 1.0, 0.0)

    def per_head(z, hd_):
        return z[:, hd_ * _ATTN_DIM:(hd_ + 1) * _ATTN_DIM]

    off = 0
    for grp in range(2):
        q_hm, k_ref, k_hm, v_ref, v_hm = outs[5 * grp:5 * grp + 5]
        q = _dot(hb, w_ref[:, off:off + GROUP_WIDTH])
        k = _dot(hb, w_ref[:, off + GROUP_WIDTH:off + 2 * GROUP_WIDTH])
        v = _dot(hb, w_ref[:, off + 2 * GROUP_WIDTH:off + 3 * GROUP_WIDTH])
        off += 3 * GROUP_WIDTH
        k_ref[...] = k
        v_ref[...] = v
        for hd_ in range(_ATTN_HEADS):
            q_hm[hd_] = per_head(q, hd_)
            k_hm[hd_] = per_head(k, hd_).astype(BF16)
            v_hm[hd_] = jnp.concatenate([per_head(v, hd_), ones_col], axis=1).astype(BF16)
    for o_ref, n in zip(outs[10:], _IN_WIDTHS[6:]):
        o_ref[...] = _dot(hb, w_ref[:, off:off + n])
        off += n


def _in_proj(x, g, w_perm, layer, tm, carried):
    t = x.shape[0]
    flat = lambda n, dt: (pl.BlockSpec((tm, n), lambda i: (i, 0)), jax.ShapeDtypeStruct((t, n), dt))
    head_major = lambda n, dt: (pl.BlockSpec((_ATTN_HEADS, tm, n), lambda i: (0, i, 0)),
                                jax.ShapeDtypeStruct((_ATTN_HEADS, t, n), dt))
    state = (pl.BlockSpec((None, tm, GROUP_WIDTH), lambda i: (layer, i, 0)),
             jax.ShapeDtypeStruct((DEPTH, t, GROUP_WIDTH), F32))
    group = [head_major(_ATTN_DIM, F32), state, head_major(_ATTN_DIM, BF16), state, head_major(LANES, BF16)]
    outs = group + group + [flat(n, F32) for n in _IN_WIDTHS[6:]]
    n_fixed = 3
    return pl.pallas_call(
        functools.partial(_in_proj_body, n_carried=len(carried)),
        grid=(t // tm,),
        in_specs=[pl.BlockSpec((tm, D_MODEL), lambda i: (i, 0)),
                  pl.BlockSpec((1, D_MODEL), lambda i: (0, 0)),
                  pl.BlockSpec((None, D_MODEL, _IN_TOTAL), lambda i: (layer, 0, 0))]
                 + [pl.BlockSpec(memory_space=pl.ANY)] * len(carried),
        out_specs=[o[0] for o in outs],
        out_shape=[o[1] for o in outs],
        input_output_aliases={n_fixed + n: _STATE_OUTS[n] for n in range(len(carried))},
        compiler_params=_cparams(("parallel",)),
        name="in_proj",
    )(x, g.reshape(1, D_MODEL), w_perm, *carried)


def _logf_cumsum_body(v_ref, b_ref, logf_ref, cum_ref, *, plen, rows):
    vals = v_ref[0, 0]
    pos = lax.broadcasted_iota(jnp.int32, (rows, LANES), 0) * LANES + lax.broadcasted_iota(jnp.int32, (rows, LANES), 1)
    logf = jnp.where(pos >= plen, jax.nn.log_sigmoid(vals + b_ref[0]), vals)
    logf_ref[0, 0] = logf
    kk = lax.broadcasted_iota(jnp.int32, (LANES, LANES), 0)
    jj = lax.broadcasted_iota(jnp.int32, (LANES, LANES), 1)
    in_row = _dot(logf, (kk <= jj).astype(F32), HI)
    tot = jnp.broadcast_to(in_row[:, LANES - 1:LANES], (rows, LANES))
    ri = lax.broadcasted_iota(jnp.int32, (rows, rows), 0)
    rj = lax.broadcasted_iota(jnp.int32, (rows, rows), 1)
    cum_ref[0, 0] = in_row + _dot((rj < ri).astype(F32), tot, HI)


def _logf_cumsum(vals, bias, plen):
    b, h, rows, _ = vals.shape
    spec = pl.BlockSpec((1, 1, rows, LANES), lambda i, j: (i, j, 0, 0))
    return pl.pallas_call(
        functools.partial(_logf_cumsum_body, plen=plen, rows=rows),
        grid=(b, h),
        in_specs=[spec, pl.BlockSpec((1, 1, 1), lambda i, j: (j, 0, 0))],
        out_specs=[spec, spec],
        out_shape=[jax.ShapeDtypeStruct(vals.shape, F32)] * 2,
        compiler_params=_cparams(("parallel", "parallel")),
        name="logf_cumsum",
    )(vals, bias.reshape(h, 1, 1))


LOG2E = math.log2(math.e)
FAR = 'far'


def _sweep_plan(kinds_per_q):
    n_pairs, sig_id, sigs = [], [], []
    for kinds in kinds_per_q:
        n_far = 0
        while n_far < len(kinds) and kinds[n_far] == FAR:
            n_far += 1
        pairs = min(n_far, len(kinds) - 1) // 2
        tail = tuple(kinds[2 * pairs:])
        if tail not in sigs:
            sigs.append(tail)
        n_pairs.append(pairs)
        sig_id.append(sigs.index(tail))
    return np.asarray(n_pairs, np.int32), np.asarray(sig_id, np.int32), sigs


def _kv_sweep(n_pairs, sig, sigs, qk, consume, bufs_a, bufs_b):
    def put(bufs, tiles):
        for buf, tile in zip(bufs, tiles):
            buf[...] = tile

    def get(bufs):
        return tuple(buf[...] for buf in bufs)

    put(bufs_a, qk(0))

    def pair(t, carry):
        j = 2 * t
        put(bufs_b, qk(j + 1))
        consume(get(bufs_a), j, FAR)
        put(bufs_a, qk(j + 2))
        consume(get(bufs_b), j + 1, FAR)
        return carry

    lax.fori_loop(0, n_pairs, pair, 0)
    base = 2 * n_pairs
    for sid, tail in enumerate(sigs):
        @pl.when(sig == sid)
        def _():
            cur, nxt = bufs_a, bufs_b
            for off, kind in enumerate(tail):
                if off + 1 < len(tail):
                    put(nxt, qk(base + off + 1))
                consume(get(cur), base + off, kind)
                cur, nxt = nxt, cur


def _softmax_block(s, v_blk, m_ref, acc_ref):
    m_old = m_ref[...]
    m_new = jnp.maximum(m_old, jnp.max(s, axis=-1, keepdims=True))
    alpha = jnp.exp2(m_old - m_new)
    bk = s.shape[1]
    m_wide = jnp.concatenate([m_new] * (bk // LANES), axis=1) if bk % LANES == 0 else m_new[:, 0:1]
    p = jnp.exp2(s - m_wide)
    acc_ref[...] = alpha * acc_ref[...] + _dot(p.astype(BF16), v_blk)
    m_ref[...] = m_new


def _with_ones(v):
    b, h, s, d = v.shape
    return jnp.concatenate([v, jnp.ones((b, h, s, 1), v.dtype), jnp.zeros((b, h, s, LANES - d - 1), v.dtype)],
                           axis=-1).astype(BF16)


def _t5_bucket(rel):
    half = NUM_BUCKETS // 2
    max_exact = half // 2
    ret = jnp.where(rel > 0, half, 0)
    n = jnp.abs(rel)
    nf = jnp.maximum(n, 1).astype(F32)
    large = max_exact + (jnp.log(nf / max_exact) / math.log(REL_MAX_DIST / max_exact)
                         * (half - max_exact)).astype(jnp.int32)
    large = jnp.minimum(large, half - 1)
    return ret + jnp.where(n < max_exact, n, large)


def _bucket_bias(rel_bias, rel):
    bucket = _t5_bucket(rel)
    table = rel_bias.astype(F32)
    out = jnp.zeros((table.shape[1],) + rel.shape, F32)
    for b in range(NUM_BUCKETS):
        out = jnp.where(bucket[None] == b, table[b].reshape((-1,) + (1,) * rel.ndim), out)
    return out


def _diff_plan(nq, nk, plen, bq, bk):
    far = bk + 2 * REL_MAX_DIST
    kinds_per_q, deltas = [], []
    for i in range(nq):
        q_hi = (plen + i * bq + bq - 1) // CHUNK
        kinds = []
        for j in range(nk):
            if (j * bk) // CHUNK > q_hi:
                break
            d = plen + i * bq - j * bk
            if d >= far:
                kinds.append(FAR)
            else:
                if d not in deltas:
                    deltas.append(d)
                kinds.append(('tile', deltas.index(d)))
        kinds_per_q.append(kinds)
    return kinds_per_q, deltas


def _diff_bias_tiles(rel_bias, deltas, bq, bk):
    r = jnp.arange(bq, dtype=jnp.int32)[:, None]
    c = jnp.arange(bk, dtype=jnp.int32)[None, :]
    far_bias = _bucket_bias(rel_bias, jnp.full((1, 1), -4 * REL_MAX_DIST, jnp.int32))
    tiles = []
    for d in deltas:
        bias = (_bucket_bias(rel_bias, c - r - d) - far_bias) * LOG2E
        vis = (c // CHUNK) <= ((r + d) // CHUNK)
        tiles.append(jnp.where(vis[None], bias, NEG_BIG))
    return jnp.stack(tiles, axis=0)


def _diff_attn_body(pairs_ref, sig_ref, q_ref, k_ref, v_ref, bias_ref, lam_ref, g_ref, o_ref,
                    sa1, sa2, sb1, sb2, m1, acc1, m2, acc2, *, bk, sigs, lam_init):
    i = pl.program_id(2)
    dv = A_V_DIM
    q = q_ref[0, 0] * (A_QK_DIM ** -0.5 * LOG2E)
    lane = lax.broadcasted_iota(jnp.int32, (1, 2 * A_QK_DIM), 1)
    q1 = jnp.where(lane < A_QK_DIM, q, 0.0).astype(BF16)
    q2 = jnp.where(lane >= A_QK_DIM, q, 0.0).astype(BF16)
    for m, acc in ((m1, acc1), (m2, acc2)):
        m[...] = jnp.full(m.shape, -jnp.inf, F32)
        acc[...] = jnp.zeros(acc.shape, F32)

    def rows(j):
        return pl.ds(j * bk if isinstance(j, int) else pl.multiple_of(j * bk, bk), bk)

    def qk(j):
        kb = k_ref[0, 0, rows(j), :]
        return _dot_nt(q1, kb), _dot_nt(q2, kb)

    def consume(tiles, j, kind):
        vb = v_ref[0, 0, rows(j), :]
        for s, m, acc in zip(tiles, (m1, m2), (acc1, acc2)):
            if kind != FAR:
                s = s + bias_ref[kind[1], 0]
            _softmax_block(s, vb, m, acc)

    _kv_sweep(pairs_ref[i], sig_ref[i], sigs, qk, consume, (sa1, sa2), (sb1, sb2))

    lp = lam_ref[...]
    lam = (jnp.exp(jnp.sum(lp[0:1] * lp[1:2], axis=-1, keepdims=True))
           - jnp.exp(jnp.sum(lp[2:3] * lp[3:4], axis=-1, keepdims=True)) + lam_init)
    a1 = acc1[...]
    a2 = acc2[...]
    o = a1[:, 0:dv] / a1[:, dv:dv + 1] - lam * (a2[:, 0:dv] / a2[:, dv:dv + 1])
    o = o * lax.rsqrt(jnp.mean(o * o, axis=-1, keepdims=True) + EPS) * g_ref[...]
    o_ref[0, 0] = o * (1.0 - lam_init)


def _diff_attn(q, k, v_aug, rel_bias, a_lambda, a_norm_g, plen, bq, bk, lam_init):
    b, h, sq, dq = q.shape
    sk = k.shape[2]
    dv = A_V_DIM
    nq, nk = sq // bq, sk // bk
    kinds, deltas = _diff_plan(nq, nk, plen, bq, bk)
    n_pairs, sig_id, sigs = _sweep_plan(kinds)
    tiles = _diff_bias_tiles(rel_bias, deltas, bq, bk)
    nt = tiles.shape[0]
    per_head = lambda shape: pl.BlockSpec((1, 1) + shape, lambda b_, h_, i, *_: (b_, h_, 0, 0))
    grid_spec = pltpu.PrefetchScalarGridSpec(
        num_scalar_prefetch=2,
        grid=(b, h, nq),
        in_specs=[pl.BlockSpec((1, 1, bq, dq), lambda b_, h_, i, *_: (b_, h_, i, 0)),
                  per_head((sk, dq)), per_head((sk, LANES)),
                  pl.BlockSpec((nt, 1, bq, bk), lambda b_, h_, i, *_: (0, h_, 0, 0)),
                  pl.BlockSpec((4, A_QK_DIM), lambda b_, h_, i, *_: (0, 0)),
                  pl.BlockSpec((1, dv), lambda b_, h_, i, *_: (0, 0))],
        out_specs=pl.BlockSpec((1, 1, bq, dv), lambda b_, h_, i, *_: (b_, h_, i, 0)),
        scratch_shapes=[pltpu.VMEM((bq, bk), F32)] * 4 + [pltpu.VMEM((bq, LANES), F32)] * 4,
    )
    return pl.pallas_call(
        functools.partial(_diff_attn_body, bk=bk, sigs=sigs, lam_init=lam_init),
        grid_spec=grid_spec,
        out_shape=jax.ShapeDtypeStruct((b, h, sq, dv), F32),
        compiler_params=_cparams(("parallel", "parallel", "arbitrary")),
        name="diff_attn",
    )(jnp.asarray(n_pairs), jnp.asarray(sig_id), q, k, v_aug, tiles, a_lambda, a_norm_g.reshape(1, dv))


MASK = 'mask'


def _forget_attn_body(pairs_ref, sig_ref, q_ref, k_ref, v_ref, ck_ref, cref_ref, o_ref, sa, sb, m, acc,
                      *, plen, bq, bk, sigs):
    i = pl.program_id(2)
    d = B_HEAD_DIM
    q = (q_ref[0, 0] * (d ** -0.5 * LOG2E)).astype(BF16)
    cref = cref_ref[0, 0, 0]
    m[...] = jnp.full(m.shape, -jnp.inf, F32)
    acc[...] = jnp.zeros(acc.shape, F32)

    def rows(j):
        return pl.ds(j * bk if isinstance(j, int) else pl.multiple_of(j * bk, bk), bk)

    def qk(j):
        e = (ck_ref[0, 0, pl.ds(j, 1), :] - cref) * LOG2E
        return (_dot_nt(q, k_ref[0, 0, rows(j), :]) - e,)

    def consume(tiles, j, kind):
        s = tiles[0]
        if kind == MASK:
            kpos = j * bk + lax.broadcasted_iota(jnp.int32, (bq, bk), 1)
            qpos = plen + i * bq + lax.broadcasted_iota(jnp.int32, (bq, bk), 0)
            s = jnp.where(kpos <= qpos, s, NEG_BIG)
        _softmax_block(s, v_ref[0, 0, rows(j), :], m, acc)

    _kv_sweep(pairs_ref[i], sig_ref[i], sigs, qk, consume, (sa,), (sb,))
    a = acc[...]
    o_ref[0, 0] = a[:, 0:d] / a[:, d:d + 1]


def _forget_attn(q, k, v_aug, cum, plen, bq, bk):
    b, h, sq, d = q.shape
    sk = k.shape[2]
    nq, nk = sq // bq, sk // bk
    kinds = []
    for i in range(nq):
        q_lo, q_hi = plen + i * bq, plen + i * bq + bq - 1
        kinds.append([FAR if j * bk + bk - 1 <= q_lo else MASK for j in range(nk) if j * bk <= q_hi])
    n_pairs, sig_id, sigs = _sweep_plan(kinds)
    cref = cum[:, :, plen:plen + sq:bq].reshape(b, h, nq, 1, 1)
    ck = cum.reshape(b, h, nk, bk)
    per_head = lambda shape: pl.BlockSpec((1, 1) + shape, lambda b_, h_, i, *_: (b_, h_, 0, 0))
    grid_spec = pltpu.PrefetchScalarGridSpec(
        num_scalar_prefetch=2,
        grid=(b, h, nq),
        in_specs=[pl.BlockSpec((1, 1, bq, d), lambda b_, h_, i, *_: (b_, h_, i, 0)),
                  per_head((sk, d)), per_head((sk, LANES)), per_head((nk, bk)),
                  pl.BlockSpec((1, 1, 1, 1, 1), lambda b_, h_, i, *_: (b_, h_, i, 0, 0))],
        out_specs=pl.BlockSpec((1, 1, bq, d), lambda b_, h_, i, *_: (b_, h_, i, 0)),
        scratch_shapes=[pltpu.VMEM((bq, bk), F32)] * 2 + [pltpu.VMEM((bq, LANES), F32)] * 2,
    )
    return pl.pallas_call(
        functools.partial(_forget_attn_body, plen=plen, bq=bq, bk=bk, sigs=sigs),
        grid_spec=grid_spec,
        out_shape=jax.ShapeDtypeStruct((b, h, sq, d), F32),
        compiler_params=_cparams(("parallel", "parallel", "arbitrary")),
        name="forget_attn",
    )(jnp.asarray(n_pairs), jnp.asarray(sig_id), q, k, v_aug, ck, cref)


_CONV_PAD = 8


def _causal_conv_tile(buf, x, cw_ref, rows):
    buf[_CONV_PAD:_CONV_PAD + rows, :] = x
    lo = _CONV_PAD - (CONV_W - 1)
    y = buf[lo:lo + rows, :] * cw_ref[0:1, :]
    for j in range(1, CONV_W):
        y = y + buf[lo + j:lo + j + rows, :] * cw_ref[j:j + 1, :]
    buf[lo:_CONV_PAD, :] = x[rows - (CONV_W - 1):rows, :]
    return y


def _rg_lru_body(x_ref, gate_ref, h0_ref, conv0_ref, cw_ref, cb_ref, wa_ref, ba_ref, wx_ref, bx_ref, lam_ref,
                 y_ref, hn_ref, convn_ref, buf, hcar, *, ts):
    n = pl.program_id(1)

    @pl.when(n == 0)
    def _():
        buf[_CONV_PAD - (CONV_W - 1):_CONV_PAD, :] = conv0_ref[0]
        hcar[...] = h0_ref[0]

    x = x_ref[0]
    xc = _causal_conv_tile(buf, x, cw_ref, ts) + cb_ref[...]
    r = jax.nn.sigmoid(_dot(xc, wa_ref[...], HI) + ba_ref[...])
    i = jax.nn.sigmoid(_dot(xc, wx_ref[...], HI) + bx_ref[...])
    log_a = -C_POWER * r * jax.nn.softplus(-lam_ref[...])
    a = jnp.exp(log_a)
    th = jnp.tanh(log_a)
    b = jnp.sqrt(-2.0 * th / (1.0 - th)) * i * xc
    row = lax.broadcasted_iota(jnp.int32, a.shape, 0)
    d = 1
    while d < ts:
        keep = row >= d
        a_sh = jnp.where(keep, pltpu.roll(a, d, 0), 1.0)
        b_sh = jnp.where(keep, pltpu.roll(b, d, 0), 0.0)
        b = a * b_sh + b
        a = a * a_sh
        d *= 2
    h = a * hcar[...] + b
    y_ref[0] = h * jax.nn.gelu(gate_ref[0])
    hcar[...] = h[ts - 1:ts, :]

    @pl.when(n == pl.num_programs(1) - 1)
    def _():
        hn_ref[0] = h[ts - 1:ts, :]
        convn_ref[0] = x[ts - (CONV_W - 1):ts, :]


def _block_diag(w):
    n, d, _ = w.shape
    eye = jnp.eye(n, dtype=w.dtype)
    return (eye[:, None, :, None] * w[:, :, None, :]).reshape(n * d, n * d)


def _rg_lru(x, gate, h0, conv0, p, ts):
    b, s, w = x.shape
    row = lambda a: a.reshape(1, w)
    tile = pl.BlockSpec((1, ts, w), lambda i, n: (i, n, 0))
    const = lambda shape: pl.BlockSpec(shape, lambda i, n: (0,) * len(shape))
    return pl.pallas_call(
        functools.partial(_rg_lru_body, ts=ts),
        grid=(b, s // ts),
        in_specs=[tile, tile,
                  pl.BlockSpec((1, 1, w), lambda i, n: (i, 0, 0)),
                  pl.BlockSpec((1, CONV_W - 1, w), lambda i, n: (i, 0, 0)),
                  const((CONV_W, w)), const((1, w)), const((w, w)), const((1, w)), const((w, w)), const((1, w)),
                  const((1, w))],
        out_specs=[tile,
                   pl.BlockSpec((1, 1, w), lambda i, n: (i, 0, 0)),
                   pl.BlockSpec((1, CONV_W - 1, w), lambda i, n: (i, 0, 0))],
        out_shape=[jax.ShapeDtypeStruct((b, s, w), F32), jax.ShapeDtypeStruct((b, 1, w), F32),
                   jax.ShapeDtypeStruct((b, CONV_W - 1, w), F32)],
        scratch_shapes=[pltpu.VMEM((_CONV_PAD + ts, w), F32), pltpu.VMEM((1, w), F32)],
        compiler_params=_cparams(("parallel", "arbitrary")),
        name="rg_lru",
    )(x, gate, h0.reshape(b, 1, w), conv0, p['c_conv_w'], row(p['c_conv_b']),
      _block_diag(p['c_gate_a_w']), row(p['c_gate_a_b']), _block_diag(p['c_gate_x_w']), row(p['c_gate_x_b']),
      row(p['c_lambda']))


def _gdn_body(qkv_ref, z_ref, bl_ref, al_ref, s0_ref, conv0_ref, cw_ref, alog_ref, dtb_ref, ng_ref,
              o_ref, sn_ref, convn_ref, buf, st, *, c, per_step, tail):
    n = pl.program_id(1)
    w = D_WIDTH
    hd = D_HEAD_DIM

    @pl.when(n == 0)
    def _():
        buf[_CONV_PAD - (CONV_W - 1):_CONV_PAD, :] = conv0_ref[0]
        st[...] = s0_ref[0]

    rows = per_step * c
    x = qkv_ref[0]
    y = jax.nn.silu(_causal_conv_tile(buf, x, cw_ref, rows))
    q, k, v = y[:, 0:w], y[:, w:2 * w], y[:, 2 * w:3 * w]

    same_head = (lax.broadcasted_iota(jnp.int32, (w, w), 0) // hd) == (lax.broadcasted_iota(jnp.int32, (w, w), 1) // hd)
    head_sum = same_head.astype(BF16)

    def head_sums(x2):
        hi, lo = _split_bf16(x2)
        return _dot(hi, head_sum) + _dot(lo, head_sum)

    def bd(a):
        return jnp.where(same_head, jnp.concatenate([a] * D_HEADS, axis=0), 0.0)

    def dot3_bd(a, b):
        return _dot3(a, bd(b))

    q = q * lax.rsqrt(head_sums(q * q) + EPS) * (hd ** -0.5)
    k = k * lax.rsqrt(head_sums(k * k) + EPS)
    beta = jax.nn.sigmoid(bl_ref[0])
    g = -jnp.exp(alog_ref[...]) * jax.nn.softplus(al_ref[0] + dtb_ref[...])
    if tail < rows:
        real = lax.broadcasted_iota(jnp.int32, (rows, w), 0) < jnp.where(n == pl.num_programs(1) - 1, tail, rows)
        beta = jnp.where(real, beta, 0.0)
        g = jnp.where(real, g, 0.0)

    ti = lax.broadcasted_iota(jnp.int32, (c, w), 0)
    tj = lax.broadcasted_iota(jnp.int32, (c, w), 1) % hd
    incl = ti >= tj
    strict = ti > tj
    diag = ti == tj
    ci = lax.broadcasted_iota(jnp.int32, (c, c), 0)
    cj = lax.broadcasted_iota(jnp.int32, (c, c), 1)
    lower_ones = (ci >= cj).astype(BF16)
    all_ones = jnp.ones((c, c), BF16)

    def chunk_local(lo):
        sl = slice(lo, lo + c)
        qc, kc, vc, bc = q[sl], k[sl], v[sl], beta[sl]
        gcum = _dot_exact_left(lower_ones, g[sl])
        g_row = _dot_exact_left(all_ones, jnp.where(diag, gcum, 0.0))
        decay = jnp.where(incl, jnp.exp(gcum - g_row), 0.0)
        eg = jnp.exp(gcum)
        kb = kc * bc
        g_last = gcum[c - 1:c, :]
        k_t = jnp.where(same_head, _dot_tn(kc.astype(BF16), diag.astype(BF16)), 0.0).astype(BF16)
        gram = _dot(jnp.concatenate([kb, qc], axis=0).astype(BF16), k_t)
        low = jnp.where(strict, gram[0:c] * decay, 0.0)
        inv = diag.astype(F32) - low
        pw = low
        span = 2
        while span < c:
            pw = dot3_bd(pw, pw)
            inv = inv + dot3_bd(inv, pw)
            span *= 2
        return dict(v_w=dot3_bd(inv, vc * bc), k_w=dot3_bd(inv, kb * eg), attn=gram[c:2 * c] * decay,
                    q_g=qc * eg, k_d=kc * jnp.exp(g_last - gcum), g_last=g_last)

    local = [chunk_local(i * c) for i in range(per_step)]
    state = st[...]
    outs = []
    for ch in local:
        through = _dot3(jnp.concatenate([ch['k_w'], ch['q_g']], axis=0), state)
        v_new = ch['v_w'] - through[0:c]
        outs.append(through[c:2 * c] + dot3_bd(ch['attn'], v_new))
        kd_hi, kd_lo = _split_bf16(ch['k_d'])
        vn_hi, vn_lo = _split_bf16(v_new)
        outer = _dot_tn(kd_hi, vn_hi) + (_dot_tn(kd_hi, vn_lo) + _dot_tn(kd_lo, vn_hi))
        state = state * jnp.exp(ch['g_last']) + jnp.where(same_head, outer, 0.0)
    st[...] = state

    o = outs[0] if per_step == 1 else jnp.concatenate(outs, axis=0)
    o = o * lax.rsqrt(head_sums(o * o) * (1.0 / hd) + EPS) * ng_ref[...]
    o_ref[0] = o * jax.nn.silu(z_ref[0])

    @pl.when(n == pl.num_programs(1) - 1)
    def _():
        sn_ref[0] = state
        convn_ref[0] = x[tail - (CONV_W - 1):tail, :]


def _gdn(qkv, z, beta_logit, alpha_logit, s0, conv0, p):
    b, s_true, w3 = qkv.shape
    w = D_WIDTH
    c = D_HEAD_DIM
    per_step = 2 if s_true >= 2 * c else 1
    rows = per_step * c
    s = -(-s_true // rows) * rows
    tail = s_true - (s - rows)
    assert tail >= CONV_W - 1
    if s != s_true:
        pad = lambda a: jnp.pad(a, ((0, 0), (0, s - s_true), (0, 0)))
        qkv, z, beta_logit, alpha_logit = pad(qkv), pad(z), pad(beta_logit), pad(alpha_logit)
    s0_bd = jax.vmap(_block_diag)(s0)
    per_lane = lambda a: jnp.repeat(a, D_HEAD_DIM).reshape(1, w)
    tile = lambda n_: pl.BlockSpec((1, rows, n_), lambda i, n: (i, n, 0))
    const = lambda shape: pl.BlockSpec(shape, lambda i, n: (0,) * len(shape))
    per_b = lambda shape: pl.BlockSpec((1,) + shape, lambda i, n: (i,) + (0,) * len(shape))
    o, sn, convn = pl.pallas_call(
        functools.partial(_gdn_body, c=c, per_step=per_step, tail=tail),
        grid=(b, s // rows),
        in_specs=[tile(w3), tile(w), tile(w), tile(w), per_b((w, w)), per_b((CONV_W - 1, w3)),
                  const((CONV_W, w3)), const((1, w)), const((1, w)), const((1, w))],
        out_specs=[tile(w), per_b((w, w)), per_b((CONV_W - 1, w3))],
        out_shape=[jax.ShapeDtypeStruct((b, s, w), F32), jax.ShapeDtypeStruct((b, w, w), F32),
                   jax.ShapeDtypeStruct((b, CONV_W - 1, w3), F32)],
        scratch_shapes=[pltpu.VMEM((_CONV_PAD + rows, w3), F32), pltpu.VMEM((w, w), F32)],
        compiler_params=_cparams(("parallel", "arbitrary")),
        name="gdn",
    )(qkv, z, beta_logit, alpha_logit, s0_bd, conv0, p['d_conv_w'], per_lane(p['d_a_log']), per_lane(p['d_dt_bias']),
      jnp.tile(p['d_norm_g'], D_HEADS).reshape(1, w))
    hd = D_HEAD_DIM
    sn = jnp.stack([sn[:, h * hd:(h + 1) * hd, h * hd:(h + 1) * hd] for h in range(D_HEADS)], axis=1)
    return o[:, :s_true], sn, convn


_PEER_SLABS = 2 * PEER_HEADS


def _out_proj_body(x_ref, oa_ref, ob_ref, oc_ref, od_ref, wo_ref, g_ref, wq_ref, sk_ref,
                   x2_ref, h2t_ref, st_ref):
    x2 = x_ref[...]
    for grp, o_ref in enumerate((oa_ref, ob_ref)):
        for hd_ in range(_ATTN_HEADS):
            lo = grp * GROUP_WIDTH + hd_ * _ATTN_DIM
            x2 = x2 + _dot(o_ref[hd_].astype(BF16), wo_ref[lo:lo + _ATTN_DIM, :])
    for grp, o_ref in ((2, oc_ref), (3, od_ref)):
        x2 = x2 + _dot(o_ref[...].astype(BF16), wo_ref[grp * GROUP_WIDTH:(grp + 1) * GROUP_WIDTH, :])
    x2_ref[...] = x2
    h2 = x2 * lax.rsqrt(jnp.mean(x2 * x2, axis=-1, keepdims=True) + EPS) * g_ref[...]
    h2t_ref[...] = h2.T.astype(BF16)
    q = _dot(h2.astype(BF16), wq_ref[...]).astype(BF16)
    for slab in range(_PEER_SLABS):
        st_ref[slab] = _dot_nt(sk_ref[slab % 2], q[:, slab * PEER_HALF:(slab + 1) * PEER_HALF])


def _out_proj(x, o_a, o_b, o_c, o_d, w_out, norm2_g, wq, subkeys, layer, tm):
    t = x.shape[0]
    tile = lambda n: pl.BlockSpec((tm, n), lambda i: (i, 0))
    heads = pl.BlockSpec((_ATTN_HEADS, tm, _ATTN_DIM), lambda i: (0, i, 0))
    const = lambda shape: pl.BlockSpec(shape, lambda i: (0,) * len(shape))
    of_layer = lambda shape: pl.BlockSpec((None,) + shape, lambda i: (layer,) + (0,) * len(shape))
    nq = PEER_HEADS * PEER_QUERY_DIM
    return pl.pallas_call(
        _out_proj_body,
        grid=(t // tm,),
        in_specs=[tile(D_MODEL), heads, heads, tile(GROUP_WIDTH), tile(GROUP_WIDTH),
                  of_layer((D_MODEL, D_MODEL)), const((1, D_MODEL)), of_layer((D_MODEL, nq)),
                  of_layer((2, PEER_N_KEYS, PEER_HALF))],
        out_specs=[tile(D_MODEL), pl.BlockSpec((D_MODEL, tm), lambda i: (0, i)),
                   pl.BlockSpec((_PEER_SLABS, PEER_N_KEYS, tm), lambda i: (0, 0, i))],
        out_shape=[jax.ShapeDtypeStruct((t, D_MODEL), F32), jax.ShapeDtypeStruct((D_MODEL, t), BF16),
                   jax.ShapeDtypeStruct((_PEER_SLABS, PEER_N_KEYS, t), F32)],
        compiler_params=_cparams(("parallel",)),
        name="out_proj",
    )(x, o_a, o_b, o_c, o_d, w_out, norm2_g.reshape(1, D_MODEL), wq, subkeys)


_NO_RANK = 64.0


def _top_values(x, count, with_rank=False):
    vals = []
    rank = jnp.full(x.shape, _NO_RANK, F32) if with_rank else None
    for k in range(count):
        m = jnp.max(x, axis=0, keepdims=True)
        vals.append(m)
        hit = x == m
        if with_rank:
            rank = jnp.where(hit, float(k), rank)
        x = jnp.where(hit, -jnp.inf, x)
    return vals, x, rank


def _peer_select_body(s_ref, c_ref, cnt_ref, r2_ref, e2_ref):
    s1 = s_ref[0]
    s2 = s_ref[1]
    n = s1.shape[1]
    v1, rest1, _ = _top_values(s1, PEER_TOPK)
    v2, _, rank2 = _top_values(s2, PEER_TOPK, with_rank=True)
    sel1 = rest1 == -jnp.inf
    v1_all = jnp.concatenate(v1, axis=0)
    v2_all = jnp.concatenate(v2, axis=0)
    row8 = lax.broadcasted_iota(jnp.int32, (8, n), 0)
    cand = [v1[0] + v2_all, v1[1] + v2_all[0:8]]
    for i in range(2, 8):
        cand.append(jnp.where(row8 < PEER_TOPK // (i + 1), v1[i] + v2_all[0:8], -jnp.inf))
    cand.append(v1_all[8:16] + v2[0])
    top, _, _ = _top_values(jnp.concatenate(cand, axis=0), PEER_TOPK)
    tau = top[PEER_TOPK - 1]
    z = jnp.ones_like(tau)
    for t in top[1:]:
        z = z + jnp.exp(t - top[0])
    count_sorted = jnp.zeros((PEER_TOPK, n), F32)
    for j in range(PEER_TOPK):
        count_sorted = count_sorted + jnp.where(v1_all + v2[j] >= tau, 1.0, 0.0)
    count = jnp.zeros(s1.shape, F32)
    for i in range(PEER_TOPK):
        count = jnp.where(s1 == v1[i], count_sorted[i:i + 1, :], count)
    cnt_ref[0] = count
    c_ref[0] = jnp.where(sel1, jnp.exp(s1 - v1[0]), 0.0) / z
    r2_ref[0] = rank2.astype(BF16)
    e2_ref[0] = jnp.where(rank2 < _NO_RANK, jnp.exp(s2 - v2[0]), 0.0).astype(BF16)


def _peer_select(st, tm):
    t = st.shape[2]
    out_spec = pl.BlockSpec((1, PEER_N_KEYS, tm), lambda i, h: (h, 0, i))
    f32 = jax.ShapeDtypeStruct((PEER_HEADS, PEER_N_KEYS, t), F32)
    bf16 = jax.ShapeDtypeStruct((PEER_HEADS, PEER_N_KEYS, t), BF16)
    return pl.pallas_call(
        _peer_select_body,
        grid=(t // tm, PEER_HEADS),
        in_specs=[pl.BlockSpec((2, PEER_N_KEYS, tm), lambda i, h: (h, 0, i))],
        out_specs=[out_spec] * 4,
        out_shape=[f32, f32, bf16, bf16],
        compiler_params=_cparams(("parallel", "parallel")),
        name="peer_select",
    )(st)


_PEER_STAGES = (4, 4, 4, 4)


def _peer_dense_body(ht_ref, u_ref, vt_ref, c_ref, cnt_ref, r2_ref, e2_ref, o_ref, *scratch, na):
    e = pl.program_id(1)
    ht = ht_ref[...]
    tm = ht.shape[1]
    assert sum(_PEER_STAGES) == na
    first_key = [sum(_PEER_STAGES[:g]) for g in range(len(_PEER_STAGES) + 1)]
    groups = len(_PEER_STAGES)
    acts, pbufs = scratch[:groups], scratch[groups:]
    packed_rows = 16

    def span(g):
        return slice(first_key[g] * PEER_N_KEYS, first_key[g + 1] * PEER_N_KEYS)

    def activations(g):
        acts[g][...] = _dot(u_ref[span(g), :], ht)

    def over_keys(row):
        packed = jnp.broadcast_to(row, (packed_rows, tm)).astype(BF16)
        return jnp.concatenate([packed] * (PEER_N_KEYS // packed_rows), axis=0)

    def gate_weights(g):
        for a in range(first_key[g], first_key[g + 1]):
            local = a - first_key[g]
            w = None
            for h in range(PEER_HEADS):
                chosen = r2_ref[h] < over_keys(cnt_ref[h, a:a + 1, :])
                term = jnp.where(chosen, e2_ref[h], 0.0) * over_keys(c_ref[h, a:a + 1, :])
                w = term if w is None else w + term
            pbufs[g][local * PEER_N_KEYS:(local + 1) * PEER_N_KEYS, :] = w

    def apply_activations(g):
        x = acts[g][...].astype(BF16)
        k0 = math.sqrt(2.0 / math.pi)
        half = x * 0.5
        t = jnp.tanh(x * (k0 + (k0 * 0.044715) * (x * x)))
        pbufs[g][...] = pbufs[g][...] * (half + half * t)

    @pl.when(e == 0)
    def _():
        o_ref[...] = jnp.zeros(o_ref.shape, F32)

    gate_weights(0)
    activations(0)
    if groups > 1:
        activations(1)
    contrib = None
    for g in range(groups):
        if g + 2 < groups:
            activations(g + 2)
        apply_activations(g)
        if g + 1 < groups:
            gate_weights(g + 1)
        part = _dot(vt_ref[:, span(g)], pbufs[g][...])
        contrib = part if contrib is None else contrib + part
    o_ref[...] += contrib


def _peer_dense(ht, u, vt, c, count, rank2, e2, layer, tm, eb):
    t = ht.shape[1]
    na = eb // PEER_N_KEYS
    key_rows = pl.BlockSpec((PEER_HEADS, na, tm), lambda i, e: (0, e, i))
    all_keys = pl.BlockSpec((PEER_HEADS, PEER_N_KEYS, tm), lambda i, e: (0, 0, i))
    return pl.pallas_call(
        functools.partial(_peer_dense_body, na=na),
        grid=(t // tm, PEER_EXPERTS // eb),
        in_specs=[pl.BlockSpec((D_MODEL, tm), lambda i, e: (0, i)),
                  pl.BlockSpec((None, eb, D_MODEL), lambda i, e: (layer, e, 0)),
                  pl.BlockSpec((None, D_MODEL, eb), lambda i, e: (layer, 0, e)),
                  key_rows, key_rows, all_keys, all_keys],
        out_specs=pl.BlockSpec((D_MODEL, tm), lambda i, e: (0, i)),
        out_shape=jax.ShapeDtypeStruct((D_MODEL, t), F32),
        scratch_shapes=[pltpu.VMEM((n * PEER_N_KEYS, tm), F32) for n in _PEER_STAGES]
                       + [pltpu.VMEM((n * PEER_N_KEYS, tm), BF16) for n in _PEER_STAGES],
        compiler_params=_cparams(("parallel", "arbitrary")),
        name="peer_dense",
    )(ht, u, vt, c, count, rank2, e2)


def _residual_body(x_ref, pt_ref, g_ref, o_ref, *, final_norm):
    x = x_ref[...] + pt_ref[...].T
    if final_norm:
        x = x * lax.rsqrt(jnp.mean(x * x, axis=-1, keepdims=True) + EPS) * g_ref[...]
    o_ref[...] = x


def _residual(x, pt, g, tm, final_norm):
    t = x.shape[0]
    return pl.pallas_call(
        functools.partial(_residual_body, final_norm=final_norm),
        grid=(t // tm,),
        in_specs=[pl.BlockSpec((tm, D_MODEL), lambda i: (i, 0)),
                  pl.BlockSpec((D_MODEL, tm), lambda i: (0, i)),
                  pl.BlockSpec((1, D_MODEL), lambda i: (0, 0))],
        out_specs=pl.BlockSpec((tm, D_MODEL), lambda i: (i, 0)),
        out_shape=jax.ShapeDtypeStruct((t, D_MODEL), F32),
        compiler_params=_cparams(("parallel",)),
        name="residual",
    )(x, pt, g.reshape(1, D_MODEL))


def _heads_first(a):
    return jnp.transpose(a, (0, 2, 1, 3))


def _prep_weights(w_in, w_out, peer_wq, peer_subkeys, peer_u, peer_v):
    return {
        'w_in': _permute_w_in(w_in),
        'w_out': w_out.astype(BF16),
        'wq': peer_wq.astype(BF16),
        'sk': peer_subkeys.astype(BF16),
        'u': peer_u.astype(BF16),
        'vt': jnp.transpose(peer_v.astype(BF16), (0, 2, 1)),
    }


def _mixers(x2d, bsz, s, past, p, pw, layer, lam_init, cfg, carried):
    pa_k, pa_v, pb_k, pb_v, pb_logf, c_h0, c_conv0, d_s0, d_conv0 = past
    plen = pa_k.shape[1]
    sk = plen + s
    (a_q, a_k, a_k16, a_v, a_v16, b_q, b_k, b_k16, b_v, b_v16,
     c_x, c_g, d_qkv, d_z, d_beta, d_alpha, small) = _in_proj(x2d, p['norm1_g'], pw['w_in'], layer, cfg['tm_in'], carried)
    bq, bk = cfg['bq'], cfg['bk']

    def seq(a, *tail):
        return a.reshape(bsz, s, *tail)

    def batch_heads(a):
        return jnp.transpose(a.reshape(a.shape[0], bsz, s, a.shape[2]), (1, 0, 2, 3))

    def head_major(a):
        return jnp.transpose(a, (1, 0, 2, 3)).reshape(a.shape[1], bsz * s, a.shape[3])

    def with_past(past_k, past_v, k16, v16):
        keys = jnp.concatenate([_heads_first(past_k).astype(BF16), batch_heads(k16)], axis=2)
        vals = jnp.concatenate([_with_ones(_heads_first(past_v)), batch_heads(v16)], axis=2)
        return keys, vals

    k_all, v_all = with_past(pa_k, pa_v, a_k16, a_v16)
    o_a = head_major(_diff_attn(batch_heads(a_q), k_all, v_all,
                                p['rel_bias'], p['a_lambda'], p['a_norm_g'], plen, bq, bk, lam_init))

    rows = -(-sk // LANES)
    rows = -(-rows // 8) * 8
    f_vals = jnp.concatenate([pb_logf, seq(small[:, 0:B_HEADS], B_HEADS)], axis=1)
    f_vals = jnp.pad(f_vals, ((0, 0), (0, rows * LANES - sk), (0, 0)))
    f_vals = jnp.transpose(f_vals, (0, 2, 1)).reshape(bsz, B_HEADS, rows, LANES)
    logf, cum = _logf_cumsum(f_vals, p['b_forget_bias'], plen)
    cum = cum.reshape(bsz, B_HEADS, rows * LANES)[:, :, :sk]
    b_logf = jnp.transpose(logf.reshape(bsz, B_HEADS, rows * LANES)[:, :, plen:sk], (0, 2, 1))
    kb_all, vb_all = with_past(pb_k, pb_v, b_k16, b_v16)
    o_b = head_major(_forget_attn(batch_heads(b_q), kb_all, vb_all, cum, plen, bq, bk))

    o_c, c_h, c_conv = _rg_lru(seq(c_x, C_WIDTH), seq(c_g, C_WIDTH), c_h0, c_conv0, p, cfg['ts'])

    o_d, d_s, d_conv = _gdn(seq(d_qkv, 3 * D_WIDTH), seq(d_z, D_WIDTH), seq(d_beta, D_WIDTH), seq(d_alpha, D_WIDTH),
                            d_s0, d_conv0, p)

    state = (b_logf, c_h.reshape(bsz, C_WIDTH), c_conv, d_s, d_conv)
    return (o_a, o_b, o_c.reshape(bsz * s, C_WIDTH), o_d.reshape(bsz * s, D_WIDTH)), state, (a_k, a_v, b_k, b_v)


def _layer(x2d, bsz, s, past, p, pw, layer, lam_init, cfg, carried):
    mix, state, carried = _mixers(x2d, bsz, s, past, p, pw, layer, lam_init, cfg, carried)
    x2, h2t, st = _out_proj(x2d, *mix, pw['w_out'], p['norm2_g'], pw['wq'], pw['sk'], layer, cfg['tm_out'])
    c, count, rank2, e2 = _peer_select(st, cfg['tm_sel'])
    peer_t = _peer_dense(h2t, pw['u'], pw['vt'], c, count, rank2, e2, layer, cfg['tm_peer'], cfg['eb'])
    return x2, peer_t, state, carried


_PROMPT_CFG = dict(tm_in=512, bq=1024, bk=512, ts=256, tm_out=256, tm_sel=512, tm_peer=512, eb=2048, tm_res=256)


def _sample_cfg(s, sk):
    return dict(tm_in=512, bq=s, bk=sk, ts=s, tm_out=256, tm_sel=512, tm_peer=512, eb=2048, tm_res=256)


def kernel(x_prompt, x_sample, cache_a_k, cache_a_v, cache_b_k, cache_b_v, cache_b_logf, state_c_h, state_c_conv, state_d_s, state_d_conv, norm1_g, norm2_g, final_norm_g, w_in, w_out, rel_bias, a_lambda, a_norm_g, b_forget_bias, c_conv_w, c_conv_b, c_gate_a_w, c_gate_a_b, c_gate_x_w, c_gate_x_b, c_lambda, d_conv_w, d_a_log, d_dt_bias, d_norm_g, peer_wq, peer_subkeys, peer_u, peer_v):
    pb, ps, _ = x_prompt.shape
    sb, ss, _ = x_sample.shape
    dt = x_prompt.dtype
    xp = x_prompt.reshape(pb * ps, D_MODEL)
    xs = x_sample.reshape(sb * ss, D_MODEL)
    scfg = _sample_cfg(ss, cache_a_k.shape[2] + ss)
    prompt_out, sample_out = [], []
    kv_p, kv_s = (), ()
    pw = _prep_weights(w_in, w_out, peer_wq, peer_subkeys, peer_u, peer_v)
    for l in range(DEPTH):
        p = {
            'norm1_g': norm1_g[l], 'norm2_g': norm2_g[l], 'rel_bias': rel_bias, 'a_lambda': a_lambda[l],
            'a_norm_g': a_norm_g[l], 'b_forget_bias': b_forget_bias[l],
            'c_conv_w': c_conv_w[l], 'c_conv_b': c_conv_b[l],
            'c_gate_a_w': c_gate_a_w[l], 'c_gate_a_b': c_gate_a_b[l],
            'c_gate_x_w': c_gate_x_w[l], 'c_gate_x_b': c_gate_x_b[l], 'c_lambda': c_lambda[l],
            'd_conv_w': d_conv_w[l], 'd_a_log': d_a_log[l], 'd_dt_bias': d_dt_bias[l], 'd_norm_g': d_norm_g[l],
        }
        lam_init = 0.8 - 0.6 * math.exp(-0.3 * l)
        empty = (
            jnp.zeros((pb, 0, A_HEADS, 2 * A_QK_DIM), dt), jnp.zeros((pb, 0, A_HEADS, A_V_DIM), dt),
            jnp.zeros((pb, 0, B_HEADS, B_HEAD_DIM), dt), jnp.zeros((pb, 0, B_HEADS, B_HEAD_DIM), dt),
            jnp.zeros((pb, 0, B_HEADS), dt),
            jnp.zeros((pb, C_WIDTH), dt), jnp.zeros((pb, CONV_W - 1, C_WIDTH), dt),
            jnp.zeros((pb, D_HEADS, D_HEAD_DIM, D_HEAD_DIM), dt), jnp.zeros((pb, CONV_W - 1, 3 * D_WIDTH), dt),
        )
        last = l == DEPTH - 1
        xp2, peer_p, st_p, kv_p = _layer(xp, pb, ps, empty, p, pw, l, lam_init, _PROMPT_CFG, kv_p)
        xp = _residual(xp2, peer_p, final_norm_g, _PROMPT_CFG['tm_res'], last)
        prompt_out.append(st_p)
        past = (cache_a_k[l], cache_a_v[l], cache_b_k[l], cache_b_v[l], cache_b_logf[l],
                state_c_h[l], state_c_conv[l], state_d_s[l], state_d_conv[l])
        xs2, peer_s, st_s, kv_s = _layer(xs, sb, ss, past, p, pw, l, lam_init, scfg, kv_s)
        xs = _residual(xs2, peer_s, final_norm_g, scfg['tm_res'], last)
        sample_out.append(st_s)

    y_prompt = xp.reshape(pb, ps, D_MODEL)
    y_sample = xs.reshape(sb, ss, D_MODEL)
    def leaves(kv, small, bsz, s):
        per_head = [a.reshape(DEPTH, bsz, s, _ATTN_HEADS, _ATTN_DIM) for a in kv]
        return per_head + [jnp.stack(z, axis=0) for z in zip(*small)]

    return (y_prompt, y_sample, *leaves(kv_p, prompt_out, pb, ps), *leaves(kv_s, sample_out, sb, ss))
```

```python
import functools
import math

import numpy as np
import jax
import jax.numpy as jnp
from jax import lax
from jax.experimental import pallas as pl
from jax.experimental.pallas import tpu as pltpu

F32 = jnp.float32
BF16 = jnp.bfloat16
HI = lax.Precision.HIGHEST

D_MODEL = 1024
DEPTH = 2
CHUNK = 64
EPS = 1e-6
CONV_W = 4
GROUP_WIDTH = D_MODEL // 4
A_HEADS = 4
A_QK_DIM = GROUP_WIDTH // (2 * A_HEADS)
A_V_DIM = GROUP_WIDTH // A_HEADS
NUM_BUCKETS = 32
REL_MAX_DIST = 256
B_HEADS = 4
B_HEAD_DIM = GROUP_WIDTH // B_HEADS
C_WIDTH = GROUP_WIDTH
C_BLOCKS = 4
C_BLOCK_DIM = C_WIDTH // C_BLOCKS
C_POWER = 8.0
D_HEADS = 4
D_HEAD_DIM = GROUP_WIDTH // D_HEADS
D_WIDTH = GROUP_WIDTH
PEER_HEADS = 8
PEER_N_KEYS = 128
PEER_EXPERTS = PEER_N_KEYS * PEER_N_KEYS
PEER_TOPK = 16
PEER_QUERY_DIM = 256
PEER_HALF = PEER_QUERY_DIM // 2

LANES = 128
NEG_BIG = -1e30
VMEM_LIMIT = 56 * 1024 * 1024


def _cparams(sem):
    return pltpu.CompilerParams(dimension_semantics=sem, vmem_limit_bytes=VMEM_LIMIT)


def _dot(a, b, precision=None):
    return jnp.dot(a, b, preferred_element_type=F32, precision=precision)


def _dot_nt(a, b, precision=None):
    return lax.dot_general(a, b, (((1,), (1,)), ((), ())), preferred_element_type=F32, precision=precision)


def _dot_tn(a, b, precision=None):
    return lax.dot_general(a, b, (((0,), (0,)), ((), ())), preferred_element_type=F32, precision=precision)


def _split_bf16(x):
    hi = x.astype(BF16)
    return hi, (x - hi.astype(F32)).astype(BF16)


def _dot3(a, b):
    ah, al = _split_bf16(a)
    bh, bl = _split_bf16(b)
    return _dot(ah, bh) + (_dot(ah, bl) + _dot(al, bh))


def _dot_exact_left(a, b):
    b1 = b.astype(BF16)
    r1 = b - b1.astype(F32)
    b2 = r1.astype(BF16)
    b3 = (r1 - b2.astype(F32)).astype(BF16)
    return _dot(a, b1) + (_dot(a, b2) + _dot(a, b3))


_IN_WIDTHS = (256,) * 8 + (768, 256, 256, 256, LANES)
_IN_TOTAL = sum(_IN_WIDTHS)


_IN_RAW = 3084
_IN_ATTN = 1536
_IN_REST = 1536


def _permute_w_in_body(w_ref, o_ref):
    w = w_ref[0]
    rows = w.shape[0]
    o_ref[0, :, 0:_IN_ATTN] = w[:, 0:_IN_ATTN].astype(BF16)
    o_ref[0, :, _IN_ATTN:_IN_ATTN + _IN_REST] = w[:, _IN_ATTN + B_HEADS:_IN_ATTN + B_HEADS + _IN_REST].astype(BF16)
    tail = w[:, _IN_RAW - LANES:_IN_RAW].astype(BF16)
    src = lax.broadcasted_iota(jnp.int32, (LANES, 2 * D_WIDTH), 0)
    dst = lax.broadcasted_iota(jnp.int32, (LANES, 2 * D_WIDTH), 1)
    spread = (src == LANES - 2 * D_HEADS + dst // D_HEAD_DIM).astype(BF16)
    o_ref[0, :, _IN_ATTN + _IN_REST:_IN_ATTN + _IN_REST + 2 * D_WIDTH] = _dot(tail, spread).astype(BF16)
    lane = lax.broadcasted_iota(jnp.int32, (rows, LANES), 1)
    o_ref[0, :, _IN_TOTAL - LANES:_IN_TOTAL] = jnp.where(lane < B_HEADS, w[:, _IN_ATTN:_IN_ATTN + LANES], 0.0).astype(BF16)


def _permute_w_in(w_in):
    depth, rows, _ = w_in.shape
    tr = 256
    return pl.pallas_call(
        _permute_w_in_body,
        grid=(depth, rows // tr),
        in_specs=[pl.BlockSpec((1, tr, _IN_RAW), lambda l, i: (l, i, 0))],
        out_specs=pl.BlockSpec((1, tr, _IN_TOTAL), lambda l, i: (l, i, 0)),
        out_shape=jax.ShapeDtypeStruct((depth, rows, _IN_TOTAL), BF16),
        compiler_params=_cparams(("parallel", "parallel")),
        name="permute_w_in",
    )(w_in)


_ATTN_HEADS = 4
_ATTN_DIM = GROUP_WIDTH // _ATTN_HEADS


_STATE_OUTS = (1, 3, 6, 8)


def _in_proj_body(x_ref, g_ref, w_ref, *rest, n_carried):
    outs = rest[n_carried:]
    x = x_ref[...]
    tm = x.shape[0]
    h = x * lax.rsqrt(jnp.mean(x * x, axis=-1, keepdims=True) + EPS) * g_ref[...]
    hb = h.astype(BF16)
    lane = lax.broadcasted_iota(jnp.int32, (tm, _ATTN_DIM), 1)
    ones_col = jnp.where(lane == 0, 1.0, 0.0)

    def per_head(z, hd_):
        return z[:, hd_ * _ATTN_DIM:(hd_ + 1) * _ATTN_DIM]

    off = 0
    for grp in range(2):
        q_hm, k_ref, k_hm, v_ref, v_hm = outs[5 * grp:5 * grp + 5]
        q = _dot(hb, w_ref[:, off:off + GROUP_WIDTH])
        k = _dot(hb, w_ref[:, off + GROUP_WIDTH:off + 2 * GROUP_WIDTH])
        v = _dot(hb, w_ref[:, off + 2 * GROUP_WIDTH:off + 3 * GROUP_WIDTH])
        off += 3 * GROUP_WIDTH
        k_ref[...] = k
        v_ref[...] = v
        for hd_ in range(_ATTN_HEADS):
            q_hm[hd_] = per_head(q, hd_)
            k_hm[hd_] = per_head(k, hd_).astype(BF16)
            v_hm[hd_] = jnp.concatenate([per_head(v, hd_), ones_col], axis=1).astype(BF16)
    for o_ref, n in zip(outs[10:], _IN_WIDTHS[6:]):
        o_ref[...] = _dot(hb, w_ref[:, off:off + n])
        off += n


def _in_proj(x, g, w_perm, layer, tm, carried):
    t = x.shape[0]
    flat = lambda n, dt: (pl.BlockSpec((tm, n), lambda i: (i, 0)), jax.ShapeDtypeStruct((t, n), dt))
    head_major = lambda n, dt: (pl.BlockSpec((_ATTN_HEADS, tm, n), lambda i: (0, i, 0)),
                                jax.ShapeDtypeStruct((_ATTN_HEADS, t, n), dt))
    state = (pl.BlockSpec((None, tm, GROUP_WIDTH), lambda i: (layer, i, 0)),
             jax.ShapeDtypeStruct((DEPTH, t, GROUP_WIDTH), F32))
    group = [head_major(_ATTN_DIM, F32), state, head_major(_ATTN_DIM, BF16), state, head_major(LANES, BF16)]
    outs = group + group + [flat(n, F32) for n in _IN_WIDTHS[6:]]
    n_fixed = 3
    return pl.pallas_call(
        functools.partial(_in_proj_body, n_carried=len(carried)),
        grid=(t // tm,),
        in_specs=[pl.BlockSpec((tm, D_MODEL), lambda i: (i, 0)),
                  pl.BlockSpec((1, D_MODEL), lambda i: (0, 0)),
                  pl.BlockSpec((None, D_MODEL, _IN_TOTAL), lambda i: (layer, 0, 0))]
                 + [pl.BlockSpec(memory_space=pl.ANY)] * len(carried),
        out_specs=[o[0] for o in outs],
        out_shape=[o[1] for o in outs],
        input_output_aliases={n_fixed + n: _STATE_OUTS[n] for n in range(len(carried))},
        compiler_params=_cparams(("parallel",)),
        name="in_proj",
    )(x, g.reshape(1, D_MODEL), w_perm, *carried)


def _logf_cumsum_body(v_ref, b_ref, logf_ref, cum_ref, *, plen, rows):
    vals = v_ref[0, 0]
    pos = lax.broadcasted_iota(jnp.int32, (rows, LANES), 0) * LANES + lax.broadcasted_iota(jnp.int32, (rows, LANES), 1)
    logf = jnp.where(pos >= plen, jax.nn.log_sigmoid(vals + b_ref[0]), vals)
    logf_ref[0, 0] = logf
    kk = lax.broadcasted_iota(jnp.int32, (LANES, LANES), 0)
    jj = lax.broadcasted_iota(jnp.int32, (LANES, LANES), 1)
    in_row = _dot(logf, (kk <= jj).astype(F32), HI)
    tot = jnp.broadcast_to(in_row[:, LANES - 1:LANES], (rows, LANES))
    ri = lax.broadcasted_iota(jnp.int32, (rows, rows), 0)
    rj = lax.broadcasted_iota(jnp.int32, (rows, rows), 1)
    cum_ref[0, 0] = in_row + _dot((rj < ri).astype(F32), tot, HI)


def _logf_cumsum(vals, bias, plen):
    b, h, rows, _ = vals.shape
    spec = pl.BlockSpec((1, 1, rows, LANES), lambda i, j: (i, j, 0, 0))
    return pl.pallas_call(
        functools.partial(_logf_cumsum_body, plen=plen, rows=rows),
        grid=(b, h),
        in_specs=[spec, pl.BlockSpec((1, 1, 1), lambda i, j: (j, 0, 0))],
        out_specs=[spec, spec],
        out_shape=[jax.ShapeDtypeStruct(vals.shape, F32)] * 2,
        compiler_params=_cparams(("parallel", "parallel")),
        name="logf_cumsum",
    )(vals, bias.reshape(h, 1, 1))


LOG2E = math.log2(math.e)
FAR = 'far'


def _sweep_plan(kinds_per_q):
    n_pairs, sig_id, sigs = [], [], []
    for kinds in kinds_per_q:
        n_far = 0
        while n_far < len(kinds) and kinds[n_far] == FAR:
            n_far += 1
        pairs = min(n_far, len(kinds) - 1) // 2
        tail = tuple(kinds[2 * pairs:])
        if tail not in sigs:
            sigs.append(tail)
        n_pairs.append(pairs)
        sig_id.append(sigs.index(tail))
    return np.asarray(n_pairs, np.int32), np.asarray(sig_id, np.int32), sigs


def _kv_sweep(n_pairs, sig, sigs, qk, consume, bufs_a, bufs_b):
    def put(bufs, tiles):
        for buf, tile in zip(bufs, tiles):
            buf[...] = tile

    def get(bufs):
        return tuple(buf[...] for buf in bufs)

    put(bufs_a, qk(0))

    def pair(t, carry):
        j = 2 * t
        put(bufs_b, qk(j + 1))
        consume(get(bufs_a), j, FAR)
        put(bufs_a, qk(j + 2))
        consume(get(bufs_b), j + 1, FAR)
        return carry

    lax.fori_loop(0, n_pairs, pair, 0)
    base = 2 * n_pairs
    for sid, tail in enumerate(sigs):
        @pl.when(sig == sid)
        def _():
            cur, nxt = bufs_a, bufs_b
            for off, kind in enumerate(tail):
                if off + 1 < len(tail):
                    put(nxt, qk(base + off + 1))
                consume(get(cur), base + off, kind)
                cur, nxt = nxt, cur


def _softmax_block(s, v_blk, m_ref, acc_ref):
    m_old = m_ref[...]
    m_new = jnp.maximum(m_old, jnp.max(s, axis=-1, keepdims=True))
    alpha = jnp.exp2(m_old - m_new)
    bk = s.shape[1]
    m_wide = jnp.concatenate([m_new] * (bk // LANES), axis=1) if bk % LANES == 0 else m_new[:, 0:1]
    p = jnp.exp2(s - m_wide)
    acc_ref[...] = alpha * acc_ref[...] + _dot(p.astype(BF16), v_blk)
    m_ref[...] = m_new


def _with_ones(v):
    b, h, s, d = v.shape
    return jnp.concatenate([v, jnp.ones((b, h, s, 1), v.dtype), jnp.zeros((b, h, s, LANES - d - 1), v.dtype)],
                           axis=-1).astype(BF16)


def _t5_bucket(rel):
    half = NUM_BUCKETS // 2
    max_exact = half // 2
    ret = jnp.where(rel > 0, half, 0)
    n = jnp.abs(rel)
    nf = jnp.maximum(n, 1).astype(F32)
    large = max_exact + (jnp.log(nf / max_exact) / math.log(REL_MAX_DIST / max_exact)
                         * (half - max_exact)).astype(jnp.int32)
    large = jnp.minimum(large, half - 1)
    return ret + jnp.where(n < max_exact, n, large)


def _bucket_bias(rel_bias, rel):
    bucket = _t5_bucket(rel)
    table = rel_bias.astype(F32)
    out = jnp.zeros((table.shape[1],) + rel.shape, F32)
    for b in range(NUM_BUCKETS):
        out = jnp.where(bucket[None] == b, table[b].reshape((-1,) + (1,) * rel.ndim), out)
    return out


def _diff_plan(nq, nk, plen, bq, bk):
    far = bk + 2 * REL_MAX_DIST
    kinds_per_q, deltas = [], []
    for i in range(nq):
        q_hi = (plen + i * bq + bq - 1) // CHUNK
        kinds = []
        for j in range(nk):
            if (j * bk) // CHUNK > q_hi:
                break
            d = plen + i * bq - j * bk
            if d >= far:
                kinds.append(FAR)
            else:
                if d not in deltas:
                    deltas.append(d)
                kinds.append(('tile', deltas.index(d)))
        kinds_per_q.append(kinds)
    return kinds_per_q, deltas


def _diff_bias_tiles(rel_bias, deltas, bq, bk):
    r = jnp.arange(bq, dtype=jnp.int32)[:, None]
    c = jnp.arange(bk, dtype=jnp.int32)[None, :]
    far_bias = _bucket_bias(rel_bias, jnp.full((1, 1), -4 * REL_MAX_DIST, jnp.int32))
    tiles = []
    for d in deltas:
        bias = (_bucket_bias(rel_bias, c - r - d) - far_bias) * LOG2E
        vis = (c // CHUNK) <= ((r + d) // CHUNK)
        tiles.append(jnp.where(vis[None], bias, NEG_BIG))
    return jnp.stack(tiles, axis=0)


def _diff_attn_body(pairs_ref, sig_ref, q_ref, k_ref, v_ref, bias_ref, lam_ref, g_ref, o_ref,
                    sa1, sa2, sb1, sb2, m1, acc1, m2, acc2, *, bk, sigs, lam_init):
    i = pl.program_id(2)
    dv = A_V_DIM
    q = q_ref[0, 0] * (A_QK_DIM ** -0.5 * LOG2E)
    lane = lax.broadcasted_iota(jnp.int32, (1, 2 * A_QK_DIM), 1)
    q1 = jnp.where(lane < A_QK_DIM, q, 0.0).astype(BF16)
    q2 = jnp.where(lane >= A_QK_DIM, q, 0.0).astype(BF16)
    for m, acc in ((m1, acc1), (m2, acc2)):
        m[...] = jnp.full(m.shape, -jnp.inf, F32)
        acc[...] = jnp.zeros(acc.shape, F32)

    def rows(j):
        return pl.ds(j * bk if isinstance(j, int) else pl.multiple_of(j * bk, bk), bk)

    def qk(j):
        kb = k_ref[0, 0, rows(j), :]
        return _dot_nt(q1, kb), _dot_nt(q2, kb)

    def consume(tiles, j, kind):
        vb = v_ref[0, 0, rows(j), :]
        for s, m, acc in zip(tiles, (m1, m2), (acc1, acc2)):
            if kind != FAR:
                s = s + bias_ref[kind[1], 0]
            _softmax_block(s, vb, m, acc)

    _kv_sweep(pairs_ref[i], sig_ref[i], sigs, qk, consume, (sa1, sa2), (sb1, sb2))

    lp = lam_ref[...]
    lam = (jnp.exp(jnp.sum(lp[0:1] * lp[1:2], axis=-1, keepdims=True))
           - jnp.exp(jnp.sum(lp[2:3] * lp[3:4], axis=-1, keepdims=True)) + lam_init)
    a1 = acc1[...]
    a2 = acc2[...]
    o = a1[:, 0:dv] / a1[:, dv:dv + 1] - lam * (a2[:, 0:dv] / a2[:, dv:dv + 1])
    o = o * lax.rsqrt(jnp.mean(o * o, axis=-1, keepdims=True) + EPS) * g_ref[...]
    o_ref[0, 0] = o * (1.0 - lam_init)


def _diff_attn(q, k, v_aug, rel_bias, a_lambda, a_norm_g, plen, bq, bk, lam_init):
    b, h, sq, dq = q.shape
    sk = k.shape[2]
    dv = A_V_DIM
    nq, nk = sq // bq, sk // bk
    kinds, deltas = _diff_plan(nq, nk, plen, bq, bk)
    n_pairs, sig_id, sigs = _sweep_plan(kinds)
    tiles = _diff_bias_tiles(rel_bias, deltas, bq, bk)
    nt = tiles.shape[0]
    per_head = lambda shape: pl.BlockSpec((1, 1) + shape, lambda b_, h_, i, *_: (b_, h_, 0, 0))
    grid_spec = pltpu.PrefetchScalarGridSpec(
        num_scalar_prefetch=2,
        grid=(b, h, nq),
        in_specs=[pl.BlockSpec((1, 1, bq, dq), lambda b_, h_, i, *_: (b_, h_, i, 0)),
                  per_head((sk, dq)), per_head((sk, LANES)),
                  pl.BlockSpec((nt, 1, bq, bk), lambda b_, h_, i, *_: (0, h_, 0, 0)),
                  pl.BlockSpec((4, A_QK_DIM), lambda b_, h_, i, *_: (0, 0)),
                  pl.BlockSpec((1, dv), lambda b_, h_, i, *_: (0, 0))],
        out_specs=pl.BlockSpec((1, 1, bq, dv), lambda b_, h_, i, *_: (b_, h_, i, 0)),
        scratch_shapes=[pltpu.VMEM((bq, bk), F32)] * 4 + [pltpu.VMEM((bq, LANES), F32)] * 4,
    )
    return pl.pallas_call(
        functools.partial(_diff_attn_body, bk=bk, sigs=sigs, lam_init=lam_init),
        grid_spec=grid_spec,
        out_shape=jax.ShapeDtypeStruct((b, h, sq, dv), F32),
        compiler_params=_cparams(("parallel", "parallel", "arbitrary")),
        name="diff_attn",
    )(jnp.asarray(n_pairs), jnp.asarray(sig_id), q, k, v_aug, tiles, a_lambda, a_norm_g.reshape(1, dv))


MASK = 'mask'


def _forget_attn_body(pairs_ref, sig_ref, q_ref, k_ref, v_ref, ck_ref, cref_ref, o_ref, sa, sb, m, acc,
                      *, plen, bq, bk, sigs):
    i = pl.program_id(2)
    d = B_HEAD_DIM
    q = (q_ref[0, 0] * (d ** -0.5 * LOG2E)).astype(BF16)
    cref = cref_ref[0, 0, 0]
    m[...] = jnp.full(m.shape, -jnp.inf, F32)
    acc[...] = jnp.zeros(acc.shape, F32)

    def rows(j):
        return pl.ds(j * bk if isinstance(j, int) else pl.multiple_of(j * bk, bk), bk)

    def qk(j):
        e = (ck_ref[0, 0, pl.ds(j, 1), :] - cref) * LOG2E
        return (_dot_nt(q, k_ref[0, 0, rows(j), :]) - e,)

    def consume(tiles, j, kind):
        s = tiles[0]
        if kind == MASK:
            kpos = j * bk + lax.broadcasted_iota(jnp.int32, (bq, bk), 1)
            qpos = plen + i * bq + lax.broadcasted_iota(jnp.int32, (bq, bk), 0)
            s = jnp.where(kpos <= qpos, s, NEG_BIG)
        _softmax_block(s, v_ref[0, 0, rows(j), :], m, acc)

    _kv_sweep(pairs_ref[i], sig_ref[i], sigs, qk, consume, (sa,), (sb,))
    a = acc[...]
    o_ref[0, 0] = a[:, 0:d] / a[:, d:d + 1]


def _forget_attn(q, k, v_aug, cum, plen, bq, bk):
    b, h, sq, d = q.shape
    sk = k.shape[2]
    nq, nk = sq // bq, sk // bk
    kinds = []
    for i in range(nq):
        q_lo, q_hi = plen + i * bq, plen + i * bq + bq - 1
        kinds.append([FAR if j * bk + bk - 1 <= q_lo else MASK for j in range(nk) if j * bk <= q_hi])
    n_pairs, sig_id, sigs = _sweep_plan(kinds)
    cref = cum[:, :, plen:plen + sq:bq].reshape(b, h, nq, 1, 1)
    ck = cum.reshape(b, h, nk, bk)
    per_head = lambda shape: pl.BlockSpec((1, 1) + shape, lambda b_, h_, i, *_: (b_, h_, 0, 0))
    grid_spec = pltpu.PrefetchScalarGridSpec(
        num_scalar_prefetch=2,
        grid=(b, h, nq),
        in_specs=[pl.BlockSpec((1, 1, bq, d), lambda b_, h_, i, *_: (b_, h_, i, 0)),
                  per_head((sk, d)), per_head((sk, LANES)), per_head((nk, bk)),
                  pl.BlockSpec((1, 1, 1, 1, 1), lambda b_, h_, i, *_: (b_, h_, i, 0, 0))],
        out_specs=pl.BlockSpec((1, 1, bq, d), lambda b_, h_, i, *_: (b_, h_, i, 0)),
        scratch_shapes=[pltpu.VMEM((bq, bk), F32)] * 2 + [pltpu.VMEM((bq, LANES), F32)] * 2,
    )
    return pl.pallas_call(
        functools.partial(_forget_attn_body, plen=plen, bq=bq, bk=bk, sigs=sigs),
        grid_spec=grid_spec,
        out_shape=jax.ShapeDtypeStruct((b, h, sq, d), F32),
        compiler_params=_cparams(("parallel", "parallel", "arbitrary")),
        name="forget_attn",
    )(jnp.asarray(n_pairs), jnp.asarray(sig_id), q, k, v_aug, ck, cref)


_CONV_PAD = 8


def _causal_conv_tile(buf, x, cw_ref, rows):
    buf[_CONV_PAD:_CONV_PAD + rows, :] = x
    lo = _CONV_PAD - (CONV_W - 1)
    y = buf[lo:lo + rows, :] * cw_ref[0:1, :]
    for j in range(1, CONV_W):
        y = y + buf[lo + j:lo + j + rows, :] * cw_ref[j:j + 1, :]
    buf[lo:_CONV_PAD, :] = x[rows - (CONV_W - 1):rows, :]
    return y


def _rg_lru_body(x_ref, gate_ref, h0_ref, conv0_ref, cw_ref, cb_ref, wa_ref, ba_ref, wx_ref, bx_ref, lam_ref,
                 y_ref, hn_ref, convn_ref, buf, hcar, *, ts):
    n = pl.program_id(1)

    @pl.when(n == 0)
    def _():
        buf[_CONV_PAD - (CONV_W - 1):_CONV_PAD, :] = conv0_ref[0]
        hcar[...] = h0_ref[0]

    x = x_ref[0]
    xc = _causal_conv_tile(buf, x, cw_ref, ts) + cb_ref[...]
    r = jax.nn.sigmoid(_dot(xc, wa_ref[...], HI) + ba_ref[...])
    i = jax.nn.sigmoid(_dot(xc, wx_ref[...], HI) + bx_ref[...])
    log_a = -C_POWER * r * jax.nn.softplus(-lam_ref[...])
    a = jnp.exp(log_a)
    th = jnp.tanh(log_a)
    b = jnp.sqrt(-2.0 * th / (1.0 - th)) * i * xc
    row = lax.broadcasted_iota(jnp.int32, a.shape, 0)
    d = 1
    while d < ts:
        keep = row >= d
        a_sh = jnp.where(keep, pltpu.roll(a, d, 0), 1.0)
        b_sh = jnp.where(keep, pltpu.roll(b, d, 0), 0.0)
        b = a * b_sh + b
        a = a * a_sh
        d *= 2
    h = a * hcar[...] + b
    y_ref[0] = h * jax.nn.gelu(gate_ref[0])
    hcar[...] = h[ts - 1:ts, :]

    @pl.when(n == pl.num_programs(1) - 1)
    def _():
        hn_ref[0] = h[ts - 1:ts, :]
        convn_ref[0] = x[ts - (CONV_W - 1):ts, :]


def _block_diag(w):
    n, d, _ = w.shape
    eye = jnp.eye(n, dtype=w.dtype)
    return (eye[:, None, :, None] * w[:, :, None, :]).reshape(n * d, n * d)


def _rg_lru(x, gate, h0, conv0, p, ts):
    b, s, w = x.shape
    row = lambda a: a.reshape(1, w)
    tile = pl.BlockSpec((1, ts, w), lambda i, n: (i, n, 0))
    const = lambda shape: pl.BlockSpec(shape, lambda i, n: (0,) * len(shape))
    return pl.pallas_call(
        functools.partial(_rg_lru_body, ts=ts),
        grid=(b, s // ts),
        in_specs=[tile, tile,
                  pl.BlockSpec((1, 1, w), lambda i, n: (i, 0, 0)),
                  pl.BlockSpec((1, CONV_W - 1, w), lambda i, n: (i, 0, 0)),
                  const((CONV_W, w)), const((1, w)), const((w, w)), const((1, w)), const((w, w)), const((1, w)),
                  const((1, w))],
        out_specs=[tile,
                   pl.BlockSpec((1, 1, w), lambda i, n: (i, 0, 0)),
                   pl.BlockSpec((1, CONV_W - 1, w), lambda i, n: (i, 0, 0))],
        out_shape=[jax.ShapeDtypeStruct((b, s, w), F32), jax.ShapeDtypeStruct((b, 1, w), F32),
                   jax.ShapeDtypeStruct((b, CONV_W - 1, w), F32)],
        scratch_shapes=[pltpu.VMEM((_CONV_PAD + ts, w), F32), pltpu.VMEM((1, w), F32)],
        compiler_params=_cparams(("parallel", "arbitrary")),
        name="rg_lru",
    )(x, gate, h0.reshape(b, 1, w), conv0, p['c_conv_w'], row(p['c_conv_b']),
      _block_diag(p['c_gate_a_w']), row(p['c_gate_a_b']), _block_diag(p['c_gate_x_w']), row(p['c_gate_x_b']),
      row(p['c_lambda']))


_GDN_BASE = 8


def _gdn_body(qkv_ref, z_ref, bl_ref, al_ref, s0_ref, conv0_ref, cw_ref, alog_ref, dtb_ref, ng_ref,
              o_ref, sn_ref, convn_ref, buf, st, *, c, per_step, tail):
    n = pl.program_id(1)
    w = D_WIDTH
    hd = D_HEAD_DIM

    @pl.when(n == 0)
    def _():
        buf[_CONV_PAD - (CONV_W - 1):_CONV_PAD, :] = conv0_ref[0]
        st[...] = s0_ref[0]

    rows = per_step * c
    x = qkv_ref[0]
    y = jax.nn.silu(_causal_conv_tile(buf, x, cw_ref, rows))
    q, k, v = y[:, 0:w], y[:, w:2 * w], y[:, 2 * w:3 * w]

    same_head = (lax.broadcasted_iota(jnp.int32, (w, w), 0) // hd) == (lax.broadcasted_iota(jnp.int32, (w, w), 1) // hd)
    head_sum = same_head.astype(BF16)

    def head_sums(x2):
        hi, lo = _split_bf16(x2)
        return _dot(hi, head_sum) + _dot(lo, head_sum)

    def bd(a):
        return jnp.where(same_head, jnp.concatenate([a] * D_HEADS, axis=0), 0.0)

    def dot3_bd(a, b):
        return _dot3(a, bd(b))

    q = q * lax.rsqrt(head_sums(q * q) + EPS) * (hd ** -0.5)
    k = k * lax.rsqrt(head_sums(k * k) + EPS)
    beta = jax.nn.sigmoid(bl_ref[0])
    g = -jnp.exp(alog_ref[...]) * jax.nn.softplus(al_ref[0] + dtb_ref[...])
    if tail < rows:
        real = lax.broadcasted_iota(jnp.int32, (rows, w), 0) < jnp.where(n == pl.num_programs(1) - 1, tail, rows)
        beta = jnp.where(real, beta, 0.0)
        g = jnp.where(real, g, 0.0)

    ti = lax.broadcasted_iota(jnp.int32, (c, w), 0)
    tj = lax.broadcasted_iota(jnp.int32, (c, w), 1) % hd
    incl = ti >= tj
    strict = ti > tj
    diag = ti == tj
    ci = lax.broadcasted_iota(jnp.int32, (c, c), 0)
    cj = lax.broadcasted_iota(jnp.int32, (c, c), 1)
    lower_ones = (ci >= cj).astype(BF16)
    all_ones = jnp.ones((c, c), BF16)

    def chunk_local(lo):
        sl = slice(lo, lo + c)
        qc, kc, vc, bc = q[sl], k[sl], v[sl], beta[sl]
        gcum = _dot_exact_left(lower_ones, g[sl])
        g_row = _dot_exact_left(all_ones, jnp.where(diag, gcum, 0.0))
        decay = jnp.where(incl, jnp.exp(gcum - g_row), 0.0)
        eg = jnp.exp(gcum)
        kb = kc * bc
        g_last = gcum[c - 1:c, :]
        k_t = jnp.where(same_head, _dot_tn(kc.astype(BF16), diag.astype(BF16)), 0.0).astype(BF16)
        gram = _dot(jnp.concatenate([kb, qc], axis=0).astype(BF16), k_t)
        low = jnp.where(strict, gram[0:c] * decay, 0.0)
        same_block = lambda b: (ti // b) == (tj // b)
        base_blocks = jnp.where(same_block(_GDN_BASE), low, 0.0)
        inv = diag.astype(F32) - base_blocks
        pw = base_blocks
        span = 2
        while span < _GDN_BASE:
            pw = dot3_bd(pw, pw)
            inv = inv + dot3_bd(inv, pw)
            span *= 2
        size = _GDN_BASE
        while size < c:
            lower_left = jnp.where(same_block(2 * size) & jnp.logical_not(same_block(size)), low, 0.0)
            inv = inv - dot3_bd(dot3_bd(inv, lower_left), inv)
            size *= 2
        return dict(v_w=dot3_bd(inv, vc * bc), k_w=dot3_bd(inv, kb * eg), attn=gram[c:2 * c] * decay,
                    q_g=qc * eg, k_d=kc * jnp.exp(g_last - gcum), g_last=g_last)

    local = [chunk_local(i * c) for i in range(per_step)]
    state = st[...]
    outs = []
    for ch in local:
        through = _dot3(jnp.concatenate([ch['k_w'], ch['q_g']], axis=0), state)
        v_new = ch['v_w'] - through[0:c]
        outs.append(through[c:2 * c] + dot3_bd(ch['attn'], v_new))
        kd_hi, kd_lo = _split_bf16(ch['k_d'])
        vn_hi, vn_lo = _split_bf16(v_new)
        outer = _dot_tn(kd_hi, vn_hi) + (_dot_tn(kd_hi, vn_lo) + _dot_tn(kd_lo, vn_hi))
        state = state * jnp.exp(ch['g_last']) + jnp.where(same_head, outer, 0.0)
    st[...] = state

    o = outs[0] if per_step == 1 else jnp.concatenate(outs, axis=0)
    o = o * lax.rsqrt(head_sums(o * o) * (1.0 / hd) + EPS) * ng_ref[...]
    o_ref[0] = o * jax.nn.silu(z_ref[0])

    @pl.when(n == pl.num_programs(1) - 1)
    def _():
        sn_ref[0] = state
        convn_ref[0] = x[tail - (CONV_W - 1):tail, :]


def _gdn(qkv, z, beta_logit, alpha_logit, s0, conv0, p):
    b, s_true, w3 = qkv.shape
    w = D_WIDTH
    c = D_HEAD_DIM
    per_step = 2 if s_true >= 2 * c else 1
    rows = per_step * c
    s = -(-s_true // rows) * rows
    tail = s_true - (s - rows)
    assert tail >= CONV_W - 1
    if s != s_true:
        pad = lambda a: jnp.pad(a, ((0, 0), (0, s - s_true), (0, 0)))
        qkv, z, beta_logit, alpha_logit = pad(qkv), pad(z), pad(beta_logit), pad(alpha_logit)
    s0_bd = jax.vmap(_block_diag)(s0)
    per_lane = lambda a: jnp.repeat(a, D_HEAD_DIM).reshape(1, w)
    tile = lambda n_: pl.BlockSpec((1, rows, n_), lambda i, n: (i, n, 0))
    const = lambda shape: pl.BlockSpec(shape, lambda i, n: (0,) * len(shape))
    per_b = lambda shape: pl.BlockSpec((1,) + shape, lambda i, n: (i,) + (0,) * len(shape))
    o, sn, convn = pl.pallas_call(
        functools.partial(_gdn_body, c=c, per_step=per_step, tail=tail),
        grid=(b, s // rows),
        in_specs=[tile(w3), tile(w), tile(w), tile(w), per_b((w, w)), per_b((CONV_W - 1, w3)),
                  const((CONV_W, w3)), const((1, w)), const((1, w)), const((1, w))],
        out_specs=[tile(w), per_b((w, w)), per_b((CONV_W - 1, w3))],
        out_shape=[jax.ShapeDtypeStruct((b, s, w), F32), jax.ShapeDtypeStruct((b, w, w), F32),
                   jax.ShapeDtypeStruct((b, CONV_W - 1, w3), F32)],
        scratch_shapes=[pltpu.VMEM((_CONV_PAD + rows, w3), F32), pltpu.VMEM((w, w), F32)],
        compiler_params=_cparams(("parallel", "arbitrary")),
        name="gdn",
    )(qkv, z, beta_logit, alpha_logit, s0_bd, conv0, p['d_conv_w'], per_lane(p['d_a_log']), per_lane(p['d_dt_bias']),
      jnp.tile(p['d_norm_g'], D_HEADS).reshape(1, w))
    hd = D_HEAD_DIM
    sn = jnp.stack([sn[:, h * hd:(h + 1) * hd, h * hd:(h + 1) * hd] for h in range(D_HEADS)], axis=1)
    return o[:, :s_true], sn, convn


_PEER_SLABS = 2 * PEER_HEADS


def _out_proj_body(x_ref, oa_ref, ob_ref, oc_ref, od_ref, wo_ref, g_ref, wq_ref, sk_ref,
                   x2_ref, h2t_ref, st_ref):
    x2 = x_ref[...]
    for grp, o_ref in enumerate((oa_ref, ob_ref)):
        for hd_ in range(_ATTN_HEADS):
            lo = grp * GROUP_WIDTH + hd_ * _ATTN_DIM
            x2 = x2 + _dot(o_ref[hd_].astype(BF16), wo_ref[lo:lo + _ATTN_DIM, :])
    for grp, o_ref in ((2, oc_ref), (3, od_ref)):
        x2 = x2 + _dot(o_ref[...].astype(BF16), wo_ref[grp * GROUP_WIDTH:(grp + 1) * GROUP_WIDTH, :])
    x2_ref[...] = x2
    h2 = x2 * lax.rsqrt(jnp.mean(x2 * x2, axis=-1, keepdims=True) + EPS) * g_ref[...]
    h2t_ref[...] = h2.T.astype(BF16)
    q = _dot(h2.astype(BF16), wq_ref[...]).astype(BF16)
    for slab in range(_PEER_SLABS):
        st_ref[slab] = _dot_nt(sk_ref[slab % 2], q[:, slab * PEER_HALF:(slab + 1) * PEER_HALF])


def _out_proj(x, o_a, o_b, o_c, o_d, w_out, norm2_g, wq, subkeys, layer, tm):
    t = x.shape[0]
    tile = lambda n: pl.BlockSpec((tm, n), lambda i: (i, 0))
    heads = pl.BlockSpec((_ATTN_HEADS, tm, _ATTN_DIM), lambda i: (0, i, 0))
    const = lambda shape: pl.BlockSpec(shape, lambda i: (0,) * len(shape))
    of_layer = lambda shape: pl.BlockSpec((None,) + shape, lambda i: (layer,) + (0,) * len(shape))
    nq = PEER_HEADS * PEER_QUERY_DIM
    return pl.pallas_call(
        _out_proj_body,
        grid=(t // tm,),
        in_specs=[tile(D_MODEL), heads, heads, tile(GROUP_WIDTH), tile(GROUP_WIDTH),
                  of_layer((D_MODEL, D_MODEL)), const((1, D_MODEL)), of_layer((D_MODEL, nq)),
                  of_layer((2, PEER_N_KEYS, PEER_HALF))],
        out_specs=[tile(D_MODEL), pl.BlockSpec((D_MODEL, tm), lambda i: (0, i)),
                   pl.BlockSpec((_PEER_SLABS, PEER_N_KEYS, tm), lambda i: (0, 0, i))],
        out_shape=[jax.ShapeDtypeStruct((t, D_MODEL), F32), jax.ShapeDtypeStruct((D_MODEL, t), BF16),
                   jax.ShapeDtypeStruct((_PEER_SLABS, PEER_N_KEYS, t), F32)],
        compiler_params=_cparams(("parallel",)),
        name="out_proj",
    )(x, o_a, o_b, o_c, o_d, w_out, norm2_g.reshape(1, D_MODEL), wq, subkeys)


_NO_RANK = 64.0


def _top_values(x, count, with_rank=False):
    vals = []
    rank = jnp.full(x.shape, _NO_RANK, F32) if with_rank else None
    for k in range(count):
        m = jnp.max(x, axis=0, keepdims=True)
        vals.append(m)
        hit = x == m
        if with_rank:
            rank = jnp.where(hit, float(k), rank)
        x = jnp.where(hit, -jnp.inf, x)
    return vals, x, rank


def _peer_select_body(s_ref, c_ref, cnt_ref, r2_ref, e2_ref):
    s1 = s_ref[0]
    s2 = s_ref[1]
    n = s1.shape[1]
    v1, rest1, _ = _top_values(s1, PEER_TOPK)
    v2, _, rank2 = _top_values(s2, PEER_TOPK, with_rank=True)
    sel1 = rest1 == -jnp.inf
    v1_all = jnp.concatenate(v1, axis=0)
    v2_all = jnp.concatenate(v2, axis=0)
    row8 = lax.broadcasted_iota(jnp.int32, (8, n), 0)
    cand = [v1[0] + v2_all, v1[1] + v2_all[0:8]]
    for i in range(2, 8):
        cand.append(jnp.where(row8 < PEER_TOPK // (i + 1), v1[i] + v2_all[0:8], -jnp.inf))
    cand.append(v1_all[8:16] + v2[0])
    top, _, _ = _top_values(jnp.concatenate(cand, axis=0), PEER_TOPK)
    tau = top[PEER_TOPK - 1]
    z = jnp.ones_like(tau)
    for t in top[1:]:
        z = z + jnp.exp(t - top[0])
    count_sorted = jnp.zeros((PEER_TOPK, n), F32)
    for j in range(PEER_TOPK):
        count_sorted = count_sorted + jnp.where(v1_all + v2[j] >= tau, 1.0, 0.0)
    count = jnp.zeros(s1.shape, F32)
    for i in range(PEER_TOPK):
        count = jnp.where(s1 == v1[i], count_sorted[i:i + 1, :], count)
    cnt_ref[0] = count
    c_ref[0] = jnp.where(sel1, jnp.exp(s1 - v1[0]), 0.0) / z
    r2_ref[0] = rank2.astype(BF16)
    e2_ref[0] = jnp.where(rank2 < _NO_RANK, jnp.exp(s2 - v2[0]), 0.0).astype(BF16)


def _peer_select(st, tm):
    t = st.shape[2]
    out_spec = pl.BlockSpec((1, PEER_N_KEYS, tm), lambda i, h: (h, 0, i))
    f32 = jax.ShapeDtypeStruct((PEER_HEADS, PEER_N_KEYS, t), F32)
    bf16 = jax.ShapeDtypeStruct((PEER_HEADS, PEER_N_KEYS, t), BF16)
    return pl.pallas_call(
        _peer_select_body,
        grid=(t // tm, PEER_HEADS),
        in_specs=[pl.BlockSpec((2, PEER_N_KEYS, tm), lambda i, h: (h, 0, i))],
        out_specs=[out_spec] * 4,
        out_shape=[f32, f32, bf16, bf16],
        compiler_params=_cparams(("parallel", "parallel")),
        name="peer_select",
    )(st)


_PEER_STAGES = (4, 4, 4, 4)


def _peer_dense_body(ht_ref, u_ref, vt_ref, c_ref, cnt_ref, r2_ref, e2_ref, o_ref, *scratch, na):
    e = pl.program_id(1)
    ht = ht_ref[...]
    tm = ht.shape[1]
    assert sum(_PEER_STAGES) == na
    first_key = [sum(_PEER_STAGES[:g]) for g in range(len(_PEER_STAGES) + 1)]
    groups = len(_PEER_STAGES)
    acts, pbufs = scratch[:groups], scratch[groups:]
    packed_rows = 16

    def span(g):
        return slice(first_key[g] * PEER_N_KEYS, first_key[g + 1] * PEER_N_KEYS)

    def activations(g):
        acts[g][...] = _dot(u_ref[span(g), :], ht)

    def over_keys(row):
        packed = jnp.broadcast_to(row, (packed_rows, tm)).astype(BF16)
        return jnp.concatenate([packed] * (PEER_N_KEYS // packed_rows), axis=0)

    def gate_weights(g):
        for a in range(first_key[g], first_key[g + 1]):
            local = a - first_key[g]
            w = None
            for h in range(PEER_HEADS):
                chosen = r2_ref[h] < over_keys(cnt_ref[h, a:a + 1, :])
                term = jnp.where(chosen, e2_ref[h], 0.0) * over_keys(c_ref[h, a:a + 1, :])
                w = term if w is None else w + term
            pbufs[g][local * PEER_N_KEYS:(local + 1) * PEER_N_KEYS, :] = w

    def apply_activations(g):
        x = acts[g][...].astype(BF16)
        k0 = math.sqrt(2.0 / math.pi)
        half = x * 0.5
        t = jnp.tanh(x * (k0 + (k0 * 0.044715) * (x * x)))
        pbufs[g][...] = pbufs[g][...] * (half + half * t)

    @pl.when(e == 0)
    def _():
        o_ref[...] = jnp.zeros(o_ref.shape, F32)

    gate_weights(0)
    activations(0)
    if groups > 1:
        activations(1)
    contrib = None
    for g in range(groups):
        if g + 2 < groups:
            activations(g + 2)
        apply_activations(g)
        if g + 1 < groups:
            gate_weights(g + 1)
        part = _dot(vt_ref[:, span(g)], pbufs[g][...])
        contrib = part if contrib is None else contrib + part
    o_ref[...] += contrib


def _peer_dense(ht, u, vt, c, count, rank2, e2, layer, tm, eb):
    t = ht.shape[1]
    na = eb // PEER_N_KEYS
    key_rows = pl.BlockSpec((PEER_HEADS, na, tm), lambda i, e: (0, e, i))
    all_keys = pl.BlockSpec((PEER_HEADS, PEER_N_KEYS, tm), lambda i, e: (0, 0, i))
    return pl.pallas_call(
        functools.partial(_peer_dense_body, na=na),
        grid=(t // tm, PEER_EXPERTS // eb),
        in_specs=[pl.BlockSpec((D_MODEL, tm), lambda i, e: (0, i)),
                  pl.BlockSpec((None, eb, D_MODEL), lambda i, e: (layer, e, 0)),
                  pl.BlockSpec((None, D_MODEL, eb), lambda i, e: (layer, 0, e)),
                  key_rows, key_rows, all_keys, all_keys],
        out_specs=pl.BlockSpec((D_MODEL, tm), lambda i, e: (0, i)),
        out_shape=jax.ShapeDtypeStruct((D_MODEL, t), F32),
        scratch_shapes=[pltpu.VMEM((n * PEER_N_KEYS, tm), F32) for n in _PEER_STAGES]
                       + [pltpu.VMEM((n * PEER_N_KEYS, tm), BF16) for n in _PEER_STAGES],
        compiler_params=_cparams(("parallel", "arbitrary")),
        name="peer_dense",
    )(ht, u, vt, c, count, rank2, e2)


def _residual_body(x_ref, pt_ref, g_ref, o_ref, *, final_norm):
    x = x_ref[...] + pt_ref[...].T
    if final_norm:
        x = x * lax.rsqrt(jnp.mean(x * x, axis=-1, keepdims=True) + EPS) * g_ref[...]
    o_ref[...] = x


def _residual(x, pt, g, tm, final_norm):
    t = x.shape[0]
    return pl.pallas_call(
        functools.partial(_residual_body, final_norm=final_norm),
        grid=(t // tm,),
        in_specs=[pl.BlockSpec((tm, D_MODEL), lambda i: (i, 0)),
                  pl.BlockSpec((D_MODEL, tm), lambda i: (0, i)),
                  pl.BlockSpec((1, D_MODEL), lambda i: (0, 0))],
        out_specs=pl.BlockSpec((tm, D_MODEL), lambda i: (i, 0)),
        out_shape=jax.ShapeDtypeStruct((t, D_MODEL), F32),
        compiler_params=_cparams(("parallel",)),
        name="residual",
    )(x, pt, g.reshape(1, D_MODEL))


def _heads_first(a):
    return jnp.transpose(a, (0, 2, 1, 3))


def _prep_weights(w_in, w_out, peer_wq, peer_subkeys, peer_u, peer_v):
    return {
        'w_in': _permute_w_in(w_in),
        'w_out': w_out.astype(BF16),
        'wq': peer_wq.astype(BF16),
        'sk': peer_subkeys.astype(BF16),
        'u': peer_u.astype(BF16),
        'vt': jnp.transpose(peer_v.astype(BF16), (0, 2, 1)),
    }


def _mixers(x2d, bsz, s, past, p, pw, layer, lam_init, cfg, carried):
    pa_k, pa_v, pb_k, pb_v, pb_logf, c_h0, c_conv0, d_s0, d_conv0 = past
    plen = pa_k.shape[1]
    sk = plen + s
    (a_q, a_k, a_k16, a_v, a_v16, b_q, b_k, b_k16, b_v, b_v16,
     c_x, c_g, d_qkv, d_z, d_beta, d_alpha, small) = _in_proj(x2d, p['norm1_g'], pw['w_in'], layer, cfg['tm_in'], carried)
    bq, bk = cfg['bq'], cfg['bk']

    def seq(a, *tail):
        return a.reshape(bsz, s, *tail)

    def batch_heads(a):
        return jnp.transpose(a.reshape(a.shape[0], bsz, s, a.shape[2]), (1, 0, 2, 3))

    def head_major(a):
        return jnp.transpose(a, (1, 0, 2, 3)).reshape(a.shape[1], bsz * s, a.shape[3])

    def with_past(past_k, past_v, k16, v16):
        keys = jnp.concatenate([_heads_first(past_k).astype(BF16), batch_heads(k16)], axis=2)
        vals = jnp.concatenate([_with_ones(_heads_first(past_v)), batch_heads(v16)], axis=2)
        return keys, vals

    k_all, v_all = with_past(pa_k, pa_v, a_k16, a_v16)
    o_a = head_major(_diff_attn(batch_heads(a_q), k_all, v_all,
                                p['rel_bias'], p['a_lambda'], p['a_norm_g'], plen, bq, bk, lam_init))

    rows = -(-sk // LANES)
    rows = -(-rows // 8) * 8
    f_vals = jnp.concatenate([pb_logf, seq(small[:, 0:B_HEADS], B_HEADS)], axis=1)
    f_vals = jnp.pad(f_vals, ((0, 0), (0, rows * LANES - sk), (0, 0)))
    f_vals = jnp.transpose(f_vals, (0, 2, 1)).reshape(bsz, B_HEADS, rows, LANES)
    logf, cum = _logf_cumsum(f_vals, p['b_forget_bias'], plen)
    cum = cum.reshape(bsz, B_HEADS, rows * LANES)[:, :, :sk]
    b_logf = jnp.transpose(logf.reshape(bsz, B_HEADS, rows * LANES)[:, :, plen:sk], (0, 2, 1))
    kb_all, vb_all = with_past(pb_k, pb_v, b_k16, b_v16)
    o_b = head_major(_forget_attn(batch_heads(b_q), kb_all, vb_all, cum, plen, bq, bk))

    o_c, c_h, c_conv = _rg_lru(seq(c_x, C_WIDTH), seq(c_g, C_WIDTH), c_h0, c_conv0, p, cfg['ts'])

    o_d, d_s, d_conv = _gdn(seq(d_qkv, 3 * D_WIDTH), seq(d_z, D_WIDTH), seq(d_beta, D_WIDTH), seq(d_alpha, D_WIDTH),
                            d_s0, d_conv0, p)

    state = (b_logf, c_h.reshape(bsz, C_WIDTH), c_conv, d_s, d_conv)
    return (o_a, o_b, o_c.reshape(bsz * s, C_WIDTH), o_d.reshape(bsz * s, D_WIDTH)), state, (a_k, a_v, b_k, b_v)


def _layer(x2d, bsz, s, past, p, pw, layer, lam_init, cfg, carried):
    mix, state, carried = _mixers(x2d, bsz, s, past, p, pw, layer, lam_init, cfg, carried)
    x2, h2t, st = _out_proj(x2d, *mix, pw['w_out'], p['norm2_g'], pw['wq'], pw['sk'], layer, cfg['tm_out'])
    c, count, rank2, e2 = _peer_select(st, cfg['tm_sel'])
    peer_t = _peer_dense(h2t, pw['u'], pw['vt'], c, count, rank2, e2, layer, cfg['tm_peer'], cfg['eb'])
    return x2, peer_t, state, carried


_PROMPT_CFG = dict(tm_in=512, bq=1024, bk=512, ts=256, tm_out=256, tm_sel=1024, tm_peer=512, eb=2048, tm_res=256)


def _sample_cfg(s, sk):
    return dict(tm_in=512, bq=s, bk=sk, ts=s, tm_out=256, tm_sel=512, tm_peer=512, eb=2048, tm_res=256)


def kernel(x_prompt, x_sample, cache_a_k, cache_a_v, cache_b_k, cache_b_v, cache_b_logf, state_c_h, state_c_conv, state_d_s, state_d_conv, norm1_g, norm2_g, final_norm_g, w_in, w_out, rel_bias, a_lambda, a_norm_g, b_forget_bias, c_conv_w, c_conv_b, c_gate_a_w, c_gate_a_b, c_gate_x_w, c_gate_x_b, c_lambda, d_conv_w, d_a_log, d_dt_bias, d_norm_g, peer_wq, peer_subkeys, peer_u, peer_v):
    pb, ps, _ = x_prompt.shape
    sb, ss, _ = x_sample.shape
    dt = x_prompt.dtype
    xp = x_prompt.reshape(pb * ps, D_MODEL)
    xs = x_sample.reshape(sb * ss, D_MODEL)
    scfg = _sample_cfg(ss, cache_a_k.shape[2] + ss)
    prompt_out, sample_out = [], []
    kv_p = tuple(jnp.zeros((DEPTH, pb * ps, GROUP_WIDTH), F32) for _ in _STATE_OUTS)
    kv_s = tuple(jnp.zeros((DEPTH, sb * ss, GROUP_WIDTH), F32) for _ in _STATE_OUTS)
    pw = _prep_weights(w_in, w_out, peer_wq, peer_subkeys, peer_u, peer_v)
    for l in range(DEPTH):
        p = {
            'norm1_g': norm1_g[l], 'norm2_g': norm2_g[l], 'rel_bias': rel_bias, 'a_lambda': a_lambda[l],
            'a_norm_g': a_norm_g[l], 'b_forget_bias': b_forget_bias[l],
            'c_conv_w': c_conv_w[l], 'c_conv_b': c_conv_b[l],
            'c_gate_a_w': c_gate_a_w[l], 'c_gate_a_b': c_gate_a_b[l],
            'c_gate_x_w': c_gate_x_w[l], 'c_gate_x_b': c_gate_x_b[l], 'c_lambda': c_lambda[l],
            'd_conv_w': d_conv_w[l], 'd_a_log': d_a_log[l], 'd_dt_bias': d_dt_bias[l], 'd_norm_g': d_norm_g[l],
        }
        lam_init = 0.8 - 0.6 * math.exp(-0.3 * l)
        empty = (
            jnp.zeros((pb, 0, A_HEADS, 2 * A_QK_DIM), dt), jnp.zeros((pb, 0, A_HEADS, A_V_DIM), dt),
            jnp.zeros((pb, 0, B_HEADS, B_HEAD_DIM), dt), jnp.zeros((pb, 0, B_HEADS, B_HEAD_DIM), dt),
            jnp.zeros((pb, 0, B_HEADS), dt),
            jnp.zeros((pb, C_WIDTH), dt), jnp.zeros((pb, CONV_W - 1, C_WIDTH), dt),
            jnp.zeros((pb, D_HEADS, D_HEAD_DIM, D_HEAD_DIM), dt), jnp.zeros((pb, CONV_W - 1, 3 * D_WIDTH), dt),
        )
        last = l == DEPTH - 1
        xp2, peer_p, st_p, kv_p = _layer(xp, pb, ps, empty, p, pw, l, lam_init, _PROMPT_CFG, kv_p)
        xp = _residual(xp2, peer_p, final_norm_g, _PROMPT_CFG['tm_res'], last)
        prompt_out.append(st_p)
        past = (cache_a_k[l], cache_a_v[l], cache_b_k[l], cache_b_v[l], cache_b_logf[l],
                state_c_h[l], state_c_conv[l], state_d_s[l], state_d_conv[l])
        xs2, peer_s, st_s, kv_s = _layer(xs, sb, ss, past, p, pw, l, lam_init, scfg, kv_s)
        xs = _residual(xs2, peer_s, final_norm_g, scfg['tm_res'], last)
        sample_out.append(st_s)

    y_prompt = xp.reshape(pb, ps, D_MODEL)
    y_sample = xs.reshape(sb, ss, D_MODEL)
    def leaves(kv, small, bsz, s):
        per_head = [a.reshape(DEPTH, bsz, s, _ATTN_HEADS, _ATTN_DIM) for a in kv]
        return per_head + [jnp.stack(z, axis=0) for z in zip(*small)]

    return (y_prompt, y_sample, *leaves(kv_p, prompt_out, pb, ps), *leaves(kv_s, sample_out, sb, ss))
```

```python
import functools
import math

import numpy as np
import jax
import jax.numpy as jnp
from jax import lax
from jax.experimental import pallas as pl
from jax.experimental.pallas import tpu as pltpu

F32 = jnp.float32
BF16 = jnp.bfloat16
HI = lax.Precision.HIGHEST

D_MODEL = 1024
DEPTH = 2
CHUNK = 64
EPS = 1e-6
CONV_W = 4
GROUP_WIDTH = D_MODEL // 4
A_HEADS = 4
A_QK_DIM = GROUP_WIDTH // (2 * A_HEADS)
A_V_DIM = GROUP_WIDTH // A_HEADS
NUM_BUCKETS = 32
REL_MAX_DIST = 256
B_HEADS = 4
B_HEAD_DIM = GROUP_WIDTH // B_HEADS
C_WIDTH = GROUP_WIDTH
C_BLOCKS = 4
C_BLOCK_DIM = C_WIDTH // C_BLOCKS
C_POWER = 8.0
D_HEADS = 4
D_HEAD_DIM = GROUP_WIDTH // D_HEADS
D_WIDTH = GROUP_WIDTH
PEER_HEADS = 8
PEER_N_KEYS = 128
PEER_EXPERTS = PEER_N_KEYS * PEER_N_KEYS
PEER_TOPK = 16
PEER_QUERY_DIM = 256
PEER_HALF = PEER_QUERY_DIM // 2

LANES = 128
NEG_BIG = -1e30
VMEM_LIMIT = 56 * 1024 * 1024


def _cparams(sem):
    return pltpu.CompilerParams(dimension_semantics=sem, vmem_limit_bytes=VMEM_LIMIT)


def _dot(a, b, precision=None):
    return jnp.dot(a, b, preferred_element_type=F32, precision=precision)


def _dot_nt(a, b, precision=None):
    return lax.dot_general(a, b, (((1,), (1,)), ((), ())), preferred_element_type=F32, precision=precision)


def _dot_tn(a, b, precision=None):
    return lax.dot_general(a, b, (((0,), (0,)), ((), ())), preferred_element_type=F32, precision=precision)


def _split_bf16(x):
    hi = x.astype(BF16)
    return hi, (x - hi.astype(F32)).astype(BF16)


def _dot3(a, b):
    ah, al = _split_bf16(a)
    bh, bl = _split_bf16(b)
    return _dot(ah, bh) + (_dot(ah, bl) + _dot(al, bh))


def _dot_exact_left(a, b):
    b1 = b.astype(BF16)
    r1 = b - b1.astype(F32)
    b2 = r1.astype(BF16)
    b3 = (r1 - b2.astype(F32)).astype(BF16)
    return _dot(a, b1) + (_dot(a, b2) + _dot(a, b3))


_IN_WIDTHS = (256,) * 8 + (768, 256, 256, 256, LANES)
_IN_TOTAL = sum(_IN_WIDTHS)


_IN_RAW = 3084
_IN_ATTN = 1536
_IN_REST = 1536


def _permute_w_in_body(w_ref, o_ref):
    w = w_ref[0]
    rows = w.shape[0]
    o_ref[0, :, 0:_IN_ATTN] = w[:, 0:_IN_ATTN].astype(BF16)
    o_ref[0, :, _IN_ATTN:_IN_ATTN + _IN_REST] = w[:, _IN_ATTN + B_HEADS:_IN_ATTN + B_HEADS + _IN_REST].astype(BF16)
    tail = w[:, _IN_RAW - LANES:_IN_RAW].astype(BF16)
    src = lax.broadcasted_iota(jnp.int32, (LANES, 2 * D_WIDTH), 0)
    dst = lax.broadcasted_iota(jnp.int32, (LANES, 2 * D_WIDTH), 1)
    spread = (src == LANES - 2 * D_HEADS + dst // D_HEAD_DIM).astype(BF16)
    o_ref[0, :, _IN_ATTN + _IN_REST:_IN_ATTN + _IN_REST + 2 * D_WIDTH] = _dot(tail, spread).astype(BF16)
    lane = lax.broadcasted_iota(jnp.int32, (rows, LANES), 1)
    o_ref[0, :, _IN_TOTAL - LANES:_IN_TOTAL] = jnp.where(lane < B_HEADS, w[:, _IN_ATTN:_IN_ATTN + LANES], 0.0).astype(BF16)


def _permute_w_in(w_in):
    depth, rows, _ = w_in.shape
    tr = 256
    return pl.pallas_call(
        _permute_w_in_body,
        grid=(depth, rows // tr),
        in_specs=[pl.BlockSpec((1, tr, _IN_RAW), lambda l, i: (l, i, 0))],
        out_specs=pl.BlockSpec((1, tr, _IN_TOTAL), lambda l, i: (l, i, 0)),
        out_shape=jax.ShapeDtypeStruct((depth, rows, _IN_TOTAL), BF16),
        compiler_params=_cparams(("parallel", "parallel")),
        name="permute_w_in",
    )(w_in)


_ATTN_HEADS = 4
_ATTN_DIM = GROUP_WIDTH // _ATTN_HEADS


_STATE_OUTS = (1, 3, 6, 8)


def _in_proj_body(x_ref, g_ref, w_ref, *rest, n_carried):
    outs = rest[n_carried:]
    x = x_ref[...]
    tm = x.shape[0]
    h = x * lax.rsqrt(jnp.mean(x * x, axis=-1, keepdims=True) + EPS) * g_ref[...]
    hb = h.astype(BF16)
    lane = lax.broadcasted_iota(jnp.int32, (tm, _ATTN_DIM), 1)
    ones_col = jnp.where(lane == 0, 1.0, 0.0)

    def per_head(z, hd_):
        return z[:, hd_ * _ATTN_DIM:(hd_ + 1) * _ATTN_DIM]

    off = 0
    for grp in range(2):
        q_hm, k_ref, k_hm, v_ref, v_hm = outs[5 * grp:5 * grp + 5]
        q = _dot(hb, w_ref[:, off:off + GROUP_WIDTH])
        k = _dot(hb, w_ref[:, off + GROUP_WIDTH:off + 2 * GROUP_WIDTH])
        v = _dot(hb, w_ref[:, off + 2 * GROUP_WIDTH:off + 3 * GROUP_WIDTH])
        off += 3 * GROUP_WIDTH
        k_ref[...] = k
        v_ref[...] = v
        for hd_ in range(_ATTN_HEADS):
            q_hm[hd_] = per_head(q, hd_)
            k_hm[hd_] = per_head(k, hd_).astype(BF16)
            v_hm[hd_] = jnp.concatenate([per_head(v, hd_), ones_col], axis=1).astype(BF16)
    for o_ref, n in zip(outs[10:], _IN_WIDTHS[6:]):
        o_ref[...] = _dot(hb, w_ref[:, off:off + n])
        off += n


def _in_proj(x, g, w_perm, layer, tm, carried):
    t = x.shape[0]
    flat = lambda n, dt: (pl.BlockSpec((tm, n), lambda i: (i, 0)), jax.ShapeDtypeStruct((t, n), dt))
    head_major = lambda n, dt: (pl.BlockSpec((_ATTN_HEADS, tm, n), lambda i: (0, i, 0)),
                                jax.ShapeDtypeStruct((_ATTN_HEADS, t, n), dt))
    state = (pl.BlockSpec((None, tm, GROUP_WIDTH), lambda i: (layer, i, 0)),
             jax.ShapeDtypeStruct((DEPTH, t, GROUP_WIDTH), F32))
    group = [head_major(_ATTN_DIM, F32), state, head_major(_ATTN_DIM, BF16), state, head_major(LANES, BF16)]
    outs = group + group + [flat(n, F32) for n in _IN_WIDTHS[6:]]
    n_fixed = 3
    return pl.pallas_call(
        functools.partial(_in_proj_body, n_carried=len(carried)),
        grid=(t // tm,),
        in_specs=[pl.BlockSpec((tm, D_MODEL), lambda i: (i, 0)),
                  pl.BlockSpec((1, D_MODEL), lambda i: (0, 0)),
                  pl.BlockSpec((None, D_MODEL, _IN_TOTAL), lambda i: (layer, 0, 0))]
                 + [pl.BlockSpec(memory_space=pl.ANY)] * len(carried),
        out_specs=[o[0] for o in outs],
        out_shape=[o[1] for o in outs],
        input_output_aliases={n_fixed + n: _STATE_OUTS[n] for n in range(len(carried))},
        compiler_params=_cparams(("parallel",)),
        name="in_proj",
    )(x, g.reshape(1, D_MODEL), w_perm, *carried)


def _logf_cumsum_body(v_ref, b_ref, logf_ref, cum_ref, *, plen, rows):
    vals = v_ref[0, 0]
    pos = lax.broadcasted_iota(jnp.int32, (rows, LANES), 0) * LANES + lax.broadcasted_iota(jnp.int32, (rows, LANES), 1)
    logf = jnp.where(pos >= plen, jax.nn.log_sigmoid(vals + b_ref[0]), vals)
    logf_ref[0, 0] = logf
    kk = lax.broadcasted_iota(jnp.int32, (LANES, LANES), 0)
    jj = lax.broadcasted_iota(jnp.int32, (LANES, LANES), 1)
    in_row = _dot(logf, (kk <= jj).astype(F32), HI)
    tot = jnp.broadcast_to(in_row[:, LANES - 1:LANES], (rows, LANES))
    ri = lax.broadcasted_iota(jnp.int32, (rows, rows), 0)
    rj = lax.broadcasted_iota(jnp.int32, (rows, rows), 1)
    cum_ref[0, 0] = in_row + _dot((rj < ri).astype(F32), tot, HI)


def _logf_cumsum(vals, bias, plen):
    b, h, rows, _ = vals.shape
    spec = pl.BlockSpec((1, 1, rows, LANES), lambda i, j: (i, j, 0, 0))
    return pl.pallas_call(
        functools.partial(_logf_cumsum_body, plen=plen, rows=rows),
        grid=(b, h),
        in_specs=[spec, pl.BlockSpec((1, 1, 1), lambda i, j: (j, 0, 0))],
        out_specs=[spec, spec],
        out_shape=[jax.ShapeDtypeStruct(vals.shape, F32)] * 2,
        compiler_params=_cparams(("parallel", "parallel")),
        name="logf_cumsum",
    )(vals, bias.reshape(h, 1, 1))


LOG2E = math.log2(math.e)
FAR = 'far'


def _sweep_plan(kinds_per_q):
    n_pairs, sig_id, sigs = [], [], []
    for kinds in kinds_per_q:
        n_far = 0
        while n_far < len(kinds) and kinds[n_far] == FAR:
            n_far += 1
        pairs = min(n_far, len(kinds) - 1) // 2
        tail = tuple(kinds[2 * pairs:])
        if tail not in sigs:
            sigs.append(tail)
        n_pairs.append(pairs)
        sig_id.append(sigs.index(tail))
    return np.asarray(n_pairs, np.int32), np.asarray(sig_id, np.int32), sigs


def _kv_sweep(n_pairs, sig, sigs, qk, consume, bufs_a, bufs_b):
    def put(bufs, tiles):
        for buf, tile in zip(bufs, tiles):
            buf[...] = tile

    def get(bufs):
        return tuple(buf[...] for buf in bufs)

    put(bufs_a, qk(0))

    def pair(t, carry):
        j = 2 * t
        put(bufs_b, qk(j + 1))
        consume(get(bufs_a), j, FAR)
        put(bufs_a, qk(j + 2))
        consume(get(bufs_b), j + 1, FAR)
        return carry

    lax.fori_loop(0, n_pairs, pair, 0)
    base = 2 * n_pairs
    for sid, tail in enumerate(sigs):
        @pl.when(sig == sid)
        def _():
            cur, nxt = bufs_a, bufs_b
            for off, kind in enumerate(tail):
                if off + 1 < len(tail):
                    put(nxt, qk(base + off + 1))
                consume(get(cur), base + off, kind)
                cur, nxt = nxt, cur


def _softmax_block(s, v_blk, m_ref, acc_ref):
    m_old = m_ref[...]
    m_new = jnp.maximum(m_old, jnp.max(s, axis=-1, keepdims=True))
    alpha = jnp.exp2(m_old - m_new)
    bk = s.shape[1]
    m_wide = jnp.concatenate([m_new] * (bk // LANES), axis=1) if bk % LANES == 0 else m_new[:, 0:1]
    p = jnp.exp2(s - m_wide)
    acc_ref[...] = alpha * acc_ref[...] + _dot(p.astype(BF16), v_blk)
    m_ref[...] = m_new


def _with_ones(v):
    b, h, s, d = v.shape
    return jnp.concatenate([v, jnp.ones((b, h, s, 1), v.dtype), jnp.zeros((b, h, s, LANES - d - 1), v.dtype)],
                           axis=-1).astype(BF16)


def _t5_bucket(rel):
    half = NUM_BUCKETS // 2
    max_exact = half // 2
    ret = jnp.where(rel > 0, half, 0)
    n = jnp.abs(rel)
    nf = jnp.maximum(n, 1).astype(F32)
    large = max_exact + (jnp.log(nf / max_exact) / math.log(REL_MAX_DIST / max_exact)
                         * (half - max_exact)).astype(jnp.int32)
    large = jnp.minimum(large, half - 1)
    return ret + jnp.where(n < max_exact, n, large)


def _bucket_bias(rel_bias, rel):
    bucket = _t5_bucket(rel)
    table = rel_bias.astype(F32)
    out = jnp.zeros((table.shape[1],) + rel.shape, F32)
    for b in range(NUM_BUCKETS):
        out = jnp.where(bucket[None] == b, table[b].reshape((-1,) + (1,) * rel.ndim), out)
    return out


def _diff_plan(nq, nk, plen, bq, bk):
    far = bk + 2 * REL_MAX_DIST
    kinds_per_q, deltas = [], []
    for i in range(nq):
        q_hi = (plen + i * bq + bq - 1) // CHUNK
        kinds = []
        for j in range(nk):
            if (j * bk) // CHUNK > q_hi:
                break
            d = plen + i * bq - j * bk
            if d >= far:
                kinds.append(FAR)
            else:
                if d not in deltas:
                    deltas.append(d)
                kinds.append(('tile', deltas.index(d)))
        kinds_per_q.append(kinds)
    return kinds_per_q, deltas


def _diff_bias_tiles(rel_bias, deltas, bq, bk):
    r = jnp.arange(bq, dtype=jnp.int32)[:, None]
    c = jnp.arange(bk, dtype=jnp.int32)[None, :]
    far_bias = _bucket_bias(rel_bias, jnp.full((1, 1), -4 * REL_MAX_DIST, jnp.int32))
    tiles = []
    for d in deltas:
        bias = (_bucket_bias(rel_bias, c - r - d) - far_bias) * LOG2E
        vis = (c // CHUNK) <= ((r + d) // CHUNK)
        tiles.append(jnp.where(vis[None], bias, NEG_BIG))
    return jnp.stack(tiles, axis=0)


def _diff_attn_body(pairs_ref, sig_ref, q_ref, k_ref, v_ref, bias_ref, lam_ref, g_ref, o_ref,
                    sa1, sa2, sb1, sb2, m1, acc1, m2, acc2, *, bk, sigs, lam_init):
    i = pl.program_id(2)
    dv = A_V_DIM
    q = q_ref[0, 0] * (A_QK_DIM ** -0.5 * LOG2E)
    lane = lax.broadcasted_iota(jnp.int32, (1, 2 * A_QK_DIM), 1)
    q1 = jnp.where(lane < A_QK_DIM, q, 0.0).astype(BF16)
    q2 = jnp.where(lane >= A_QK_DIM, q, 0.0).astype(BF16)
    for m, acc in ((m1, acc1), (m2, acc2)):
        m[...] = jnp.full(m.shape, -jnp.inf, F32)
        acc[...] = jnp.zeros(acc.shape, F32)

    def rows(j):
        return pl.ds(j * bk if isinstance(j, int) else pl.multiple_of(j * bk, bk), bk)

    def qk(j):
        kb = k_ref[0, 0, rows(j), :]
        return _dot_nt(q1, kb), _dot_nt(q2, kb)

    def consume(tiles, j, kind):
        vb = v_ref[0, 0, rows(j), :]
        for s, m, acc in zip(tiles, (m1, m2), (acc1, acc2)):
            if kind != FAR:
                s = s + bias_ref[kind[1], 0]
            _softmax_block(s, vb, m, acc)

    _kv_sweep(pairs_ref[i], sig_ref[i], sigs, qk, consume, (sa1, sa2), (sb1, sb2))

    lp = lam_ref[...]
    lam = (jnp.exp(jnp.sum(lp[0:1] * lp[1:2], axis=-1, keepdims=True))
           - jnp.exp(jnp.sum(lp[2:3] * lp[3:4], axis=-1, keepdims=True)) + lam_init)
    a1 = acc1[...]
    a2 = acc2[...]
    o = a1[:, 0:dv] / a1[:, dv:dv + 1] - lam * (a2[:, 0:dv] / a2[:, dv:dv + 1])
    o = o * lax.rsqrt(jnp.mean(o * o, axis=-1, keepdims=True) + EPS) * g_ref[...]
    o_ref[0, 0] = o * (1.0 - lam_init)


def _diff_attn(q, k, v_aug, rel_bias, a_lambda, a_norm_g, plen, bq, bk, lam_init):
    b, h, sq, dq = q.shape
    sk = k.shape[2]
    dv = A_V_DIM
    nq, nk = sq // bq, sk // bk
    kinds, deltas = _diff_plan(nq, nk, plen, bq, bk)
    n_pairs, sig_id, sigs = _sweep_plan(kinds)
    tiles = _diff_bias_tiles(rel_bias, deltas, bq, bk)
    nt = tiles.shape[0]
    per_head = lambda shape: pl.BlockSpec((1, 1) + shape, lambda b_, h_, i, *_: (b_, h_, 0, 0))
    grid_spec = pltpu.PrefetchScalarGridSpec(
        num_scalar_prefetch=2,
        grid=(b, h, nq),
        in_specs=[pl.BlockSpec((1, 1, bq, dq), lambda b_, h_, i, *_: (b_, h_, i, 0)),
                  per_head((sk, dq)), per_head((sk, LANES)),
                  pl.BlockSpec((nt, 1, bq, bk), lambda b_, h_, i, *_: (0, h_, 0, 0)),
                  pl.BlockSpec((4, A_QK_DIM), lambda b_, h_, i, *_: (0, 0)),
                  pl.BlockSpec((1, dv), lambda b_, h_, i, *_: (0, 0))],
        out_specs=pl.BlockSpec((1, 1, bq, dv), lambda b_, h_, i, *_: (b_, h_, i, 0)),
        scratch_shapes=[pltpu.VMEM((bq, bk), F32)] * 4 + [pltpu.VMEM((bq, LANES), F32)] * 4,
    )
    return pl.pallas_call(
        functools.partial(_diff_attn_body, bk=bk, sigs=sigs, lam_init=lam_init),
        grid_spec=grid_spec,
        out_shape=jax.ShapeDtypeStruct((b, h, sq, dv), F32),
        compiler_params=_cparams(("parallel", "parallel", "arbitrary")),
        name="diff_attn",
    )(jnp.asarray(n_pairs), jnp.asarray(sig_id), q, k, v_aug, tiles, a_lambda, a_norm_g.reshape(1, dv))


MASK = 'mask'


def _forget_attn_body(pairs_ref, sig_ref, q_ref, k_ref, v_ref, ck_ref, cref_ref, o_ref, sa, sb, m, acc,
                      *, plen, bq, bk, sigs):
    i = pl.program_id(2)
    d = B_HEAD_DIM
    q = (q_ref[0, 0] * (d ** -0.5 * LOG2E)).astype(BF16)
    cref = cref_ref[0, 0, 0]
    m[...] = jnp.full(m.shape, -jnp.inf, F32)
    acc[...] = jnp.zeros(acc.shape, F32)

    def rows(j):
        return pl.ds(j * bk if isinstance(j, int) else pl.multiple_of(j * bk, bk), bk)

    def qk(j):
        e = (ck_ref[0, 0, pl.ds(j, 1), :] - cref) * LOG2E
        return (_dot_nt(q, k_ref[0, 0, rows(j), :]) - e,)

    def consume(tiles, j, kind):
        s = tiles[0]
        if kind == MASK:
            kpos = j * bk + lax.broadcasted_iota(jnp.int32, (bq, bk), 1)
            qpos = plen + i * bq + lax.broadcasted_iota(jnp.int32, (bq, bk), 0)
            s = jnp.where(kpos <= qpos, s, NEG_BIG)
        _softmax_block(s, v_ref[0, 0, rows(j), :], m, acc)

    _kv_sweep(pairs_ref[i], sig_ref[i], sigs, qk, consume, (sa,), (sb,))
    a = acc[...]
    o_ref[0, 0] = a[:, 0:d] / a[:, d:d + 1]


def _forget_attn(q, k, v_aug, cum, plen, bq, bk):
    b, h, sq, d = q.shape
    sk = k.shape[2]
    nq, nk = sq // bq, sk // bk
    kinds = []
    for i in range(nq):
        q_lo, q_hi = plen + i * bq, plen + i * bq + bq - 1
        kinds.append([FAR if j * bk + bk - 1 <= q_lo else MASK for j in range(nk) if j * bk <= q_hi])
    n_pairs, sig_id, sigs = _sweep_plan(kinds)
    cref = cum[:, :, plen:plen + sq:bq].reshape(b, h, nq, 1, 1)
    ck = cum.reshape(b, h, nk, bk)
    per_head = lambda shape: pl.BlockSpec((1, 1) + shape, lambda b_, h_, i, *_: (b_, h_, 0, 0))
    grid_spec = pltpu.PrefetchScalarGridSpec(
        num_scalar_prefetch=2,
        grid=(b, h, nq),
        in_specs=[pl.BlockSpec((1, 1, bq, d), lambda b_, h_, i, *_: (b_, h_, i, 0)),
                  per_head((sk, d)), per_head((sk, LANES)), per_head((nk, bk)),
                  pl.BlockSpec((1, 1, 1, 1, 1), lambda b_, h_, i, *_: (b_, h_, i, 0, 0))],
        out_specs=pl.BlockSpec((1, 1, bq, d), lambda b_, h_, i, *_: (b_, h_, i, 0)),
        scratch_shapes=[pltpu.VMEM((bq, bk), F32)] * 2 + [pltpu.VMEM((bq, LANES), F32)] * 2,
    )
    return pl.pallas_call(
        functools.partial(_forget_attn_body, plen=plen, bq=bq, bk=bk, sigs=sigs),
        grid_spec=grid_spec,
        out_shape=jax.ShapeDtypeStruct((b, h, sq, d), F32),
        compiler_params=_cparams(("parallel", "parallel", "arbitrary")),
        name="forget_attn",
    )(jnp.asarray(n_pairs), jnp.asarray(sig_id), q, k, v_aug, ck, cref)


_CONV_PAD = 8


def _causal_conv_tile(buf, x, cw_ref, rows):
    buf[_CONV_PAD:_CONV_PAD + rows, :] = x
    lo = _CONV_PAD - (CONV_W - 1)
    y = buf[lo:lo + rows, :] * cw_ref[0:1, :]
    for j in range(1, CONV_W):
        y = y + buf[lo + j:lo + j + rows, :] * cw_ref[j:j + 1, :]
    buf[lo:_CONV_PAD, :] = x[rows - (CONV_W - 1):rows, :]
    return y


def _rg_lru_body(x_ref, gate_ref, h0_ref, conv0_ref, cw_ref, cb_ref, wa_ref, ba_ref, wx_ref, bx_ref, lam_ref,
                 y_ref, hn_ref, convn_ref, buf, hcar, *, ts):
    n = pl.program_id(1)

    @pl.when(n == 0)
    def _():
        buf[_CONV_PAD - (CONV_W - 1):_CONV_PAD, :] = conv0_ref[0]
        hcar[...] = h0_ref[0]

    x = x_ref[0]
    xc = _causal_conv_tile(buf, x, cw_ref, ts) + cb_ref[...]
    r = jax.nn.sigmoid(_dot(xc, wa_ref[...], HI) + ba_ref[...])
    i = jax.nn.sigmoid(_dot(xc, wx_ref[...], HI) + bx_ref[...])
    log_a = -C_POWER * r * jax.nn.softplus(-lam_ref[...])
    a = jnp.exp(log_a)
    th = jnp.tanh(log_a)
    b = jnp.sqrt(-2.0 * th / (1.0 - th)) * i * xc
    row = lax.broadcasted_iota(jnp.int32, a.shape, 0)
    d = 1
    while d < ts:
        keep = row >= d
        a_sh = jnp.where(keep, pltpu.roll(a, d, 0), 1.0)
        b_sh = jnp.where(keep, pltpu.roll(b, d, 0), 0.0)
        b = a * b_sh + b
        a = a * a_sh
        d *= 2
    h = a * hcar[...] + b
    y_ref[0] = h * jax.nn.gelu(gate_ref[0])
    hcar[...] = h[ts - 1:ts, :]

    @pl.when(n == pl.num_programs(1) - 1)
    def _():
        hn_ref[0] = h[ts - 1:ts, :]
        convn_ref[0] = x[ts - (CONV_W - 1):ts, :]


def _block_diag(w):
    n, d, _ = w.shape
    eye = jnp.eye(n, dtype=w.dtype)
    return (eye[:, None, :, None] * w[:, :, None, :]).reshape(n * d, n * d)


def _rg_lru(x, gate, h0, conv0, p, ts):
    b, s, w = x.shape
    row = lambda a: a.reshape(1, w)
    tile = pl.BlockSpec((1, ts, w), lambda i, n: (i, n, 0))
    const = lambda shape: pl.BlockSpec(shape, lambda i, n: (0,) * len(shape))
    return pl.pallas_call(
        functools.partial(_rg_lru_body, ts=ts),
        grid=(b, s // ts),
        in_specs=[tile, tile,
                  pl.BlockSpec((1, 1, w), lambda i, n: (i, 0, 0)),
                  pl.BlockSpec((1, CONV_W - 1, w), lambda i, n: (i, 0, 0)),
                  const((CONV_W, w)), const((1, w)), const((w, w)), const((1, w)), const((w, w)), const((1, w)),
                  const((1, w))],
        out_specs=[tile,
                   pl.BlockSpec((1, 1, w), lambda i, n: (i, 0, 0)),
                   pl.BlockSpec((1, CONV_W - 1, w), lambda i, n: (i, 0, 0))],
        out_shape=[jax.ShapeDtypeStruct((b, s, w), F32), jax.ShapeDtypeStruct((b, 1, w), F32),
                   jax.ShapeDtypeStruct((b, CONV_W - 1, w), F32)],
        scratch_shapes=[pltpu.VMEM((_CONV_PAD + ts, w), F32), pltpu.VMEM((1, w), F32)],
        compiler_params=_cparams(("parallel", "arbitrary")),
        name="rg_lru",
    )(x, gate, h0.reshape(b, 1, w), conv0, p['c_conv_w'], row(p['c_conv_b']),
      _block_diag(p['c_gate_a_w']), row(p['c_gate_a_b']), _block_diag(p['c_gate_x_w']), row(p['c_gate_x_b']),
      row(p['c_lambda']))


_GDN_BASE = 8


def _gdn_body(qkv_ref, z_ref, bl_ref, al_ref, s0_ref, conv0_ref, cw_ref, alog_ref, dtb_ref, ng_ref,
              o_ref, sn_ref, convn_ref, buf, st, *, c, per_step, tail):
    n = pl.program_id(1)
    w = D_WIDTH
    hd = D_HEAD_DIM

    @pl.when(n == 0)
    def _():
        buf[_CONV_PAD - (CONV_W - 1):_CONV_PAD, :] = conv0_ref[0]
        st[...] = s0_ref[0]

    rows = per_step * c
    x = qkv_ref[0]
    y = jax.nn.silu(_causal_conv_tile(buf, x, cw_ref, rows))
    q, k, v = y[:, 0:w], y[:, w:2 * w], y[:, 2 * w:3 * w]

    same_head = (lax.broadcasted_iota(jnp.int32, (w, w), 0) // hd) == (lax.broadcasted_iota(jnp.int32, (w, w), 1) // hd)
    head_sum = same_head.astype(BF16)

    def head_sums(x2):
        hi, lo = _split_bf16(x2)
        return _dot(hi, head_sum) + _dot(lo, head_sum)

    def bd(a):
        return jnp.where(same_head, jnp.concatenate([a] * D_HEADS, axis=0), 0.0)

    def dot3_bd(a, b):
        return _dot3(a, bd(b))

    q = q * lax.rsqrt(head_sums(q * q) + EPS) * (hd ** -0.5)
    k = k * lax.rsqrt(head_sums(k * k) + EPS)
    beta = jax.nn.sigmoid(bl_ref[0])
    g = -jnp.exp(alog_ref[...]) * jax.nn.softplus(al_ref[0] + dtb_ref[...])
    if tail < rows:
        real = lax.broadcasted_iota(jnp.int32, (rows, w), 0) < jnp.where(n == pl.num_programs(1) - 1, tail, rows)
        beta = jnp.where(real, beta, 0.0)
        g = jnp.where(real, g, 0.0)

    ti = lax.broadcasted_iota(jnp.int32, (c, w), 0)
    tj = lax.broadcasted_iota(jnp.int32, (c, w), 1) % hd
    incl = ti >= tj
    strict = ti > tj
    diag = ti == tj
    ci = lax.broadcasted_iota(jnp.int32, (c, c), 0)
    cj = lax.broadcasted_iota(jnp.int32, (c, c), 1)
    lower_ones = (ci >= cj).astype(BF16)
    all_ones = jnp.ones((c, c), BF16)

    def chunk_local(lo):
        sl = slice(lo, lo + c)
        qc, kc, vc, bc = q[sl], k[sl], v[sl], beta[sl]
        gcum = _dot_exact_left(lower_ones, g[sl])
        g_row = _dot_exact_left(all_ones, jnp.where(diag, gcum, 0.0))
        decay = jnp.where(incl, jnp.exp(gcum - g_row), 0.0)
        eg = jnp.exp(gcum)
        kb = kc * bc
        g_last = gcum[c - 1:c, :]
        k_t = jnp.where(same_head, _dot_tn(kc.astype(BF16), diag.astype(BF16)), 0.0).astype(BF16)
        gram = _dot(jnp.concatenate([kb, qc], axis=0).astype(BF16), k_t)
        low = jnp.where(strict, gram[0:c] * decay, 0.0)
        same_block = lambda b: (ti // b) == (tj // b)
        base_blocks = jnp.where(same_block(_GDN_BASE), low, 0.0)
        inv = diag.astype(F32) - base_blocks
        pw = base_blocks
        span = 2
        while span < _GDN_BASE:
            pw = dot3_bd(pw, pw)
            inv = inv + dot3_bd(inv, pw)
            span *= 2
        size = _GDN_BASE
        while size < c:
            lower_left = jnp.where(same_block(2 * size) & jnp.logical_not(same_block(size)), low, 0.0)
            inv = inv - dot3_bd(dot3_bd(inv, lower_left), inv)
            size *= 2
        return dict(v_w=dot3_bd(inv, vc * bc), k_w=dot3_bd(inv, kb * eg), attn=gram[c:2 * c] * decay,
                    q_g=qc * eg, k_d=kc * jnp.exp(g_last - gcum), g_last=g_last)

    local = [chunk_local(i * c) for i in range(per_step)]
    state = st[...]
    outs = []
    for ch in local:
        through = _dot3(jnp.concatenate([ch['k_w'], ch['q_g']], axis=0), state)
        v_new = ch['v_w'] - through[0:c]
        outs.append(through[c:2 * c] + dot3_bd(ch['attn'], v_new))
        kd_hi, kd_lo = _split_bf16(ch['k_d'])
        vn_hi, vn_lo = _split_bf16(v_new)
        outer = _dot_tn(kd_hi, vn_hi) + (_dot_tn(kd_hi, vn_lo) + _dot_tn(kd_lo, vn_hi))
        state = state * jnp.exp(ch['g_last']) + jnp.where(same_head, outer, 0.0)
    st[...] = state

    o = outs[0] if per_step == 1 else jnp.concatenate(outs, axis=0)
    o = o * lax.rsqrt(head_sums(o * o) * (1.0 / hd) + EPS) * ng_ref[...]
    o_ref[0] = o * jax.nn.silu(z_ref[0])

    @pl.when(n == pl.num_programs(1) - 1)
    def _():
        sn_ref[0] = state
        convn_ref[0] = x[tail - (CONV_W - 1):tail, :]


def _gdn(qkv, z, beta_logit, alpha_logit, s0, conv0, p):
    b, s_true, w3 = qkv.shape
    w = D_WIDTH
    c = D_HEAD_DIM
    per_step = 2 if s_true >= 2 * c else 1
    rows = per_step * c
    s = -(-s_true // rows) * rows
    tail = s_true - (s - rows)
    assert tail >= CONV_W - 1
    if s != s_true:
        pad = lambda a: jnp.pad(a, ((0, 0), (0, s - s_true), (0, 0)))
        qkv, z, beta_logit, alpha_logit = pad(qkv), pad(z), pad(beta_logit), pad(alpha_logit)
    s0_bd = jax.vmap(_block_diag)(s0)
    per_lane = lambda a: jnp.repeat(a, D_HEAD_DIM).reshape(1, w)
    tile = lambda n_: pl.BlockSpec((1, rows, n_), lambda i, n: (i, n, 0))
    const = lambda shape: pl.BlockSpec(shape, lambda i, n: (0,) * len(shape))
    per_b = lambda shape: pl.BlockSpec((1,) + shape, lambda i, n: (i,) + (0,) * len(shape))
    o, sn, convn = pl.pallas_call(
        functools.partial(_gdn_body, c=c, per_step=per_step, tail=tail),
        grid=(b, s // rows),
        in_specs=[tile(w3), tile(w), tile(w), tile(w), per_b((w, w)), per_b((CONV_W - 1, w3)),
                  const((CONV_W, w3)), const((1, w)), const((1, w)), const((1, w))],
        out_specs=[tile(w), per_b((w, w)), per_b((CONV_W - 1, w3))],
        out_shape=[jax.ShapeDtypeStruct((b, s, w), F32), jax.ShapeDtypeStruct((b, w, w), F32),
                   jax.ShapeDtypeStruct((b, CONV_W - 1, w3), F32)],
        scratch_shapes=[pltpu.VMEM((_CONV_PAD + rows, w3), F32), pltpu.VMEM((w, w), F32)],
        compiler_params=_cparams(("parallel", "arbitrary")),
        name="gdn",
    )(qkv, z, beta_logit, alpha_logit, s0_bd, conv0, p['d_conv_w'], per_lane(p['d_a_log']), per_lane(p['d_dt_bias']),
      jnp.tile(p['d_norm_g'], D_HEADS).reshape(1, w))
    hd = D_HEAD_DIM
    sn = jnp.stack([sn[:, h * hd:(h + 1) * hd, h * hd:(h + 1) * hd] for h in range(D_HEADS)], axis=1)
    return o[:, :s_true], sn, convn


_PEER_SLABS = 2 * PEER_HEADS


def _out_proj_body(x_ref, oa_ref, ob_ref, oc_ref, od_ref, wo_ref, g_ref, wq_ref, sk_ref,
                   x2_ref, h2t_ref, st_ref):
    x2 = x_ref[...]
    for grp, o_ref in enumerate((oa_ref, ob_ref)):
        for hd_ in range(_ATTN_HEADS):
            lo = grp * GROUP_WIDTH + hd_ * _ATTN_DIM
            x2 = x2 + _dot(o_ref[hd_].astype(BF16), wo_ref[lo:lo + _ATTN_DIM, :])
    for grp, o_ref in ((2, oc_ref), (3, od_ref)):
        x2 = x2 + _dot(o_ref[...].astype(BF16), wo_ref[grp * GROUP_WIDTH:(grp + 1) * GROUP_WIDTH, :])
    x2_ref[...] = x2
    h2 = x2 * lax.rsqrt(jnp.mean(x2 * x2, axis=-1, keepdims=True) + EPS) * g_ref[...]
    h2t_ref[...] = h2.T.astype(BF16)
    q = _dot(h2.astype(BF16), wq_ref[...]).astype(BF16)
    for slab in range(_PEER_SLABS):
        st_ref[slab] = _dot_nt(sk_ref[slab % 2], q[:, slab * PEER_HALF:(slab + 1) * PEER_HALF])


def _out_proj(x, o_a, o_b, o_c, o_d, w_out, norm2_g, wq, subkeys, layer, tm):
    t = x.shape[0]
    tile = lambda n: pl.BlockSpec((tm, n), lambda i: (i, 0))
    heads = pl.BlockSpec((_ATTN_HEADS, tm, _ATTN_DIM), lambda i: (0, i, 0))
    const = lambda shape: pl.BlockSpec(shape, lambda i: (0,) * len(shape))
    of_layer = lambda shape: pl.BlockSpec((None,) + shape, lambda i: (layer,) + (0,) * len(shape))
    nq = PEER_HEADS * PEER_QUERY_DIM
    return pl.pallas_call(
        _out_proj_body,
        grid=(t // tm,),
        in_specs=[tile(D_MODEL), heads, heads, tile(GROUP_WIDTH), tile(GROUP_WIDTH),
                  of_layer((D_MODEL, D_MODEL)), const((1, D_MODEL)), of_layer((D_MODEL, nq)),
                  of_layer((2, PEER_N_KEYS, PEER_HALF))],
        out_specs=[tile(D_MODEL), pl.BlockSpec((D_MODEL, tm), lambda i: (0, i)),
                   pl.BlockSpec((_PEER_SLABS, PEER_N_KEYS, tm), lambda i: (0, 0, i))],
        out_shape=[jax.ShapeDtypeStruct((t, D_MODEL), F32), jax.ShapeDtypeStruct((D_MODEL, t), BF16),
                   jax.ShapeDtypeStruct((_PEER_SLABS, PEER_N_KEYS, t), F32)],
        compiler_params=_cparams(("parallel",)),
        name="out_proj",
    )(x, o_a, o_b, o_c, o_d, w_out, norm2_g.reshape(1, D_MODEL), wq, subkeys)


_NO_RANK = 64.0


def _top_values(x, count, with_rank=False):
    vals = []
    rank = jnp.full(x.shape, _NO_RANK, F32) if with_rank else None
    for k in range(count):
        m = jnp.max(x, axis=0, keepdims=True)
        vals.append(m)
        hit = x == m
        if with_rank:
            rank = jnp.where(hit, float(k), rank)
        x = jnp.where(hit, -jnp.inf, x)
    return vals, x, rank


def _peer_select_body(s_ref, c_ref, cnt_ref, r2_ref, e2_ref):
    s1 = s_ref[0]
    s2 = s_ref[1]
    n = s1.shape[1]
    v1, rest1, _ = _top_values(s1, PEER_TOPK)
    v2, _, rank2 = _top_values(s2, PEER_TOPK, with_rank=True)
    sel1 = rest1 == -jnp.inf
    v1_all = jnp.concatenate(v1, axis=0)
    v2_all = jnp.concatenate(v2, axis=0)
    row8 = lax.broadcasted_iota(jnp.int32, (8, n), 0)
    cand = [v1[0] + v2_all, v1[1] + v2_all[0:8]]
    for i in range(2, 8):
        cand.append(jnp.where(row8 < PEER_TOPK // (i + 1), v1[i] + v2_all[0:8], -jnp.inf))
    cand.append(v1_all[8:16] + v2[0])
    top, _, _ = _top_values(jnp.concatenate(cand, axis=0), PEER_TOPK)
    tau = top[PEER_TOPK - 1]
    z = jnp.ones_like(tau)
    for t in top[1:]:
        z = z + jnp.exp(t - top[0])
    count_sorted = jnp.zeros((PEER_TOPK, n), F32)
    for j in range(PEER_TOPK):
        count_sorted = count_sorted + jnp.where(v1_all + v2[j] >= tau, 1.0, 0.0)
    count = jnp.zeros(s1.shape, F32)
    for i in range(PEER_TOPK):
        count = jnp.where(s1 == v1[i], count_sorted[i:i + 1, :], count)
    cnt_ref[0] = count
    c_ref[0] = jnp.where(sel1, jnp.exp(s1 - v1[0]), 0.0) / z
    r2_ref[0] = rank2.astype(BF16)
    e2_ref[0] = jnp.where(rank2 < _NO_RANK, jnp.exp(s2 - v2[0]), 0.0).astype(BF16)


def _peer_select(st, tm):
    t = st.shape[2]
    out_spec = pl.BlockSpec((1, PEER_N_KEYS, tm), lambda i, h: (h, 0, i))
    f32 = jax.ShapeDtypeStruct((PEER_HEADS, PEER_N_KEYS, t), F32)
    bf16 = jax.ShapeDtypeStruct((PEER_HEADS, PEER_N_KEYS, t), BF16)
    return pl.pallas_call(
        _peer_select_body,
        grid=(t // tm, PEER_HEADS),
        in_specs=[pl.BlockSpec((2, PEER_N_KEYS, tm), lambda i, h: (h, 0, i))],
        out_specs=[out_spec] * 4,
        out_shape=[f32, f32, bf16, bf16],
        compiler_params=_cparams(("parallel", "parallel")),
        name="peer_select",
    )(st)


_PEER_STAGES = (4, 4, 4, 4)


def _peer_dense_body(ht_ref, u_ref, vt_ref, c_ref, cnt_ref, r2_ref, e2_ref, o_ref, *scratch, na):
    e = pl.program_id(1)
    ht = ht_ref[...]
    tm = ht.shape[1]
    assert sum(_PEER_STAGES) == na
    first_key = [sum(_PEER_STAGES[:g]) for g in range(len(_PEER_STAGES) + 1)]
    groups = len(_PEER_STAGES)
    acts, pbufs = scratch[:groups], scratch[groups:]
    packed_rows = 16

    def span(g):
        return slice(first_key[g] * PEER_N_KEYS, first_key[g + 1] * PEER_N_KEYS)

    def activations(g):
        acts[g][...] = _dot(u_ref[span(g), :], ht)

    def over_keys(row):
        packed = jnp.broadcast_to(row, (packed_rows, tm)).astype(BF16)
        return jnp.concatenate([packed] * (PEER_N_KEYS // packed_rows), axis=0)

    def gate_weights(g):
        for a in range(first_key[g], first_key[g + 1]):
            local = a - first_key[g]
            w = None
            for h in range(PEER_HEADS):
                chosen = r2_ref[h] < over_keys(cnt_ref[h, a:a + 1, :])
                term = jnp.where(chosen, e2_ref[h], 0.0) * over_keys(c_ref[h, a:a + 1, :])
                w = term if w is None else w + term
            pbufs[g][local * PEER_N_KEYS:(local + 1) * PEER_N_KEYS, :] = w

    def apply_activations(g):
        x = acts[g][...].astype(BF16)
        k0 = math.sqrt(2.0 / math.pi)
        half = x * 0.5
        t = jnp.tanh(x * (k0 + (k0 * 0.044715) * (x * x)))
        pbufs[g][...] = pbufs[g][...] * (half + half * t)

    @pl.when(e == 0)
    def _():
        o_ref[...] = jnp.zeros(o_ref.shape, F32)

    gate_weights(0)
    activations(0)
    if groups > 1:
        activations(1)
    contrib = None
    for g in range(groups):
        if g + 2 < groups:
            activations(g + 2)
        apply_activations(g)
        if g + 1 < groups:
            gate_weights(g + 1)
        part = _dot(vt_ref[:, span(g)], pbufs[g][...])
        contrib = part if contrib is None else contrib + part
    o_ref[...] += contrib


def _peer_dense(ht, u, vt, c, count, rank2, e2, layer, tm, eb):
    t = ht.shape[1]
    na = eb // PEER_N_KEYS
    key_rows = pl.BlockSpec((PEER_HEADS, na, tm), lambda i, e: (0, e, i))
    all_keys = pl.BlockSpec((PEER_HEADS, PEER_N_KEYS, tm), lambda i, e: (0, 0, i))
    return pl.pallas_call(
        functools.partial(_peer_dense_body, na=na),
        grid=(t // tm, PEER_EXPERTS // eb),
        in_specs=[pl.BlockSpec((D_MODEL, tm), lambda i, e: (0, i)),
                  pl.BlockSpec((None, eb, D_MODEL), lambda i, e: (layer, e, 0)),
                  pl.BlockSpec((None, D_MODEL, eb), lambda i, e: (layer, 0, e)),
                  key_rows, key_rows, all_keys, all_keys],
        out_specs=pl.BlockSpec((D_MODEL, tm), lambda i, e: (0, i)),
        out_shape=jax.ShapeDtypeStruct((D_MODEL, t), F32),
        scratch_shapes=[pltpu.VMEM((n * PEER_N_KEYS, tm), F32) for n in _PEER_STAGES]
                       + [pltpu.VMEM((n * PEER_N_KEYS, tm), BF16) for n in _PEER_STAGES],
        compiler_params=_cparams(("parallel", "arbitrary")),
        name="peer_dense",
    )(ht, u, vt, c, count, rank2, e2)


def _residual_body(x_ref, pt_ref, g_ref, o_ref, *, final_norm):
    x = x_ref[...] + pt_ref[...].T
    if final_norm:
        x = x * lax.rsqrt(jnp.mean(x * x, axis=-1, keepdims=True) + EPS) * g_ref[...]
    o_ref[...] = x


def _residual(x, pt, g, tm, final_norm):
    t = x.shape[0]
    return pl.pallas_call(
        functools.partial(_residual_body, final_norm=final_norm),
        grid=(t // tm,),
        in_specs=[pl.BlockSpec((tm, D_MODEL), lambda i: (i, 0)),
                  pl.BlockSpec((D_MODEL, tm), lambda i: (0, i)),
                  pl.BlockSpec((1, D_MODEL), lambda i: (0, 0))],
        out_specs=pl.BlockSpec((tm, D_MODEL), lambda i: (i, 0)),
        out_shape=jax.ShapeDtypeStruct((t, D_MODEL), F32),
        compiler_params=_cparams(("parallel",)),
        name="residual",
    )(x, pt, g.reshape(1, D_MODEL))


def _heads_first(a):
    return jnp.transpose(a, (0, 2, 1, 3))


def _prep_weights(w_in, w_out, peer_wq, peer_subkeys, peer_u, peer_v):
    return {
        'w_in': _permute_w_in(w_in),
        'w_out': w_out.astype(BF16),
        'wq': peer_wq.astype(BF16),
        'sk': peer_subkeys.astype(BF16),
        'u': peer_u.astype(BF16),
        'vt': jnp.transpose(peer_v.astype(BF16), (0, 2, 1)),
    }


def _mixers(x2d, bsz, s, past, p, pw, layer, lam_init, cfg, carried):
    pa_k, pa_v, pb_k, pb_v, pb_logf, c_h0, c_conv0, d_s0, d_conv0 = past
    plen = pa_k.shape[1]
    sk = plen + s
    (a_q, a_k, a_k16, a_v, a_v16, b_q, b_k, b_k16, b_v, b_v16,
     c_x, c_g, d_qkv, d_z, d_beta, d_alpha, small) = _in_proj(x2d, p['norm1_g'], pw['w_in'], layer, cfg['tm_in'], carried)
    bq, bk = cfg['bq'], cfg['bk']

    def seq(a, *tail):
        return a.reshape(bsz, s, *tail)

    def batch_heads(a):
        return jnp.transpose(a.reshape(a.shape[0], bsz, s, a.shape[2]), (1, 0, 2, 3))

    def head_major(a):
        return jnp.transpose(a, (1, 0, 2, 3)).reshape(a.shape[1], bsz * s, a.shape[3])

    def with_past(past_k, past_v, k16, v16):
        keys = jnp.concatenate([_heads_first(past_k).astype(BF16), batch_heads(k16)], axis=2)
        vals = jnp.concatenate([_with_ones(_heads_first(past_v)), batch_heads(v16)], axis=2)
        return keys, vals

    k_all, v_all = with_past(pa_k, pa_v, a_k16, a_v16)
    o_a = head_major(_diff_attn(batch_heads(a_q), k_all, v_all,
                                p['rel_bias'], p['a_lambda'], p['a_norm_g'], plen, bq, bk, lam_init))

    rows = -(-sk // LANES)
    rows = -(-rows // 8) * 8
    f_vals = jnp.concatenate([pb_logf, seq(small[:, 0:B_HEADS], B_HEADS)], axis=1)
    f_vals = jnp.pad(f_vals, ((0, 0), (0, rows * LANES - sk), (0, 0)))
    f_vals = jnp.transpose(f_vals, (0, 2, 1)).reshape(bsz, B_HEADS, rows, LANES)
    logf, cum = _logf_cumsum(f_vals, p['b_forget_bias'], plen)
    cum = cum.reshape(bsz, B_HEADS, rows * LANES)[:, :, :sk]
    b_logf = jnp.transpose(logf.reshape(bsz, B_HEADS, rows * LANES)[:, :, plen:sk], (0, 2, 1))
    kb_all, vb_all = with_past(pb_k, pb_v, b_k16, b_v16)
    o_b = head_major(_forget_attn(batch_heads(b_q), kb_all, vb_all, cum, plen, cfg['bq_forget'], cfg['bk_forget']))

    o_c, c_h, c_conv = _rg_lru(seq(c_x, C_WIDTH), seq(c_g, C_WIDTH), c_h0, c_conv0, p, cfg['ts'])

    o_d, d_s, d_conv = _gdn(seq(d_qkv, 3 * D_WIDTH), seq(d_z, D_WIDTH), seq(d_beta, D_WIDTH), seq(d_alpha, D_WIDTH),
                            d_s0, d_conv0, p)

    state = (b_logf, c_h.reshape(bsz, C_WIDTH), c_conv, d_s, d_conv)
    return (o_a, o_b, o_c.reshape(bsz * s, C_WIDTH), o_d.reshape(bsz * s, D_WIDTH)), state, (a_k, a_v, b_k, b_v)


def _layer(x2d, bsz, s, past, p, pw, layer, lam_init, cfg, carried):
    mix, state, carried = _mixers(x2d, bsz, s, past, p, pw, layer, lam_init, cfg, carried)
    x2, h2t, st = _out_proj(x2d, *mix, pw['w_out'], p['norm2_g'], pw['wq'], pw['sk'], layer, cfg['tm_out'])
    c, count, rank2, e2 = _peer_select(st, cfg['tm_sel'])
    peer_t = _peer_dense(h2t, pw['u'], pw['vt'], c, count, rank2, e2, layer, cfg['tm_peer'], cfg['eb'])
    return x2, peer_t, state, carried


_PROMPT_CFG = dict(tm_in=512, bq=1024, bk=512, bq_forget=1024, bk_forget=1024, ts=256, tm_out=512, tm_sel=1024,
                   tm_peer=512, eb=2048, tm_res=256)


def _sample_cfg(s, sk):
    return dict(tm_in=512, bq=s, bk=sk, bq_forget=s, bk_forget=sk, ts=s, tm_out=512, tm_sel=512, tm_peer=512,
                eb=2048, tm_res=256)


def kernel(x_prompt, x_sample, cache_a_k, cache_a_v, cache_b_k, cache_b_v, cache_b_logf, state_c_h, state_c_conv, state_d_s, state_d_conv, norm1_g, norm2_g, final_norm_g, w_in, w_out, rel_bias, a_lambda, a_norm_g, b_forget_bias, c_conv_w, c_conv_b, c_gate_a_w, c_gate_a_b, c_gate_x_w, c_gate_x_b, c_lambda, d_conv_w, d_a_log, d_dt_bias, d_norm_g, peer_wq, peer_subkeys, peer_u, peer_v):
    pb, ps, _ = x_prompt.shape
    sb, ss, _ = x_sample.shape
    dt = x_prompt.dtype
    xp = x_prompt.reshape(pb * ps, D_MODEL)
    xs = x_sample.reshape(sb * ss, D_MODEL)
    scfg = _sample_cfg(ss, cache_a_k.shape[2] + ss)
    prompt_out, sample_out = [], []
    kv_p = tuple(jnp.zeros((DEPTH, pb * ps, GROUP_WIDTH), F32) for _ in _STATE_OUTS)
    kv_s = tuple(jnp.zeros((DEPTH, sb * ss, GROUP_WIDTH), F32) for _ in _STATE_OUTS)
    pw = _prep_weights(w_in, w_out, peer_wq, peer_subkeys, peer_u, peer_v)
    for l in range(DEPTH):
        p = {
            'norm1_g': norm1_g[l], 'norm2_g': norm2_g[l], 'rel_bias': rel_bias, 'a_lambda': a_lambda[l],
            'a_norm_g': a_norm_g[l], 'b_forget_bias': b_forget_bias[l],
            'c_conv_w': c_conv_w[l], 'c_conv_b': c_conv_b[l],
            'c_gate_a_w': c_gate_a_w[l], 'c_gate_a_b': c_gate_a_b[l],
            'c_gate_x_w': c_gate_x_w[l], 'c_gate_x_b': c_gate_x_b[l], 'c_lambda': c_lambda[l],
            'd_conv_w': d_conv_w[l], 'd_a_log': d_a_log[l], 'd_dt_bias': d_dt_bias[l], 'd_norm_g': d_norm_g[l],
        }
        lam_init = 0.8 - 0.6 * math.exp(-0.3 * l)
        empty = (
            jnp.zeros((pb, 0, A_HEADS, 2 * A_QK_DIM), dt), jnp.zeros((pb, 0, A_HEADS, A_V_DIM), dt),
            jnp.zeros((pb, 0, B_HEADS, B_HEAD_DIM), dt), jnp.zeros((pb, 0, B_HEADS, B_HEAD_DIM), dt),
            jnp.zeros((pb, 0, B_HEADS), dt),
            jnp.zeros((pb, C_WIDTH), dt), jnp.zeros((pb, CONV_W - 1, C_WIDTH), dt),
            jnp.zeros((pb, D_HEADS, D_HEAD_DIM, D_HEAD_DIM), dt), jnp.zeros((pb, CONV_W - 1, 3 * D_WIDTH), dt),
        )
        last = l == DEPTH - 1
        xp2, peer_p, st_p, kv_p = _layer(xp, pb, ps, empty, p, pw, l, lam_init, _PROMPT_CFG, kv_p)
        xp = _residual(xp2, peer_p, final_norm_g, _PROMPT_CFG['tm_res'], last)
        prompt_out.append(st_p)
        past = (cache_a_k[l], cache_a_v[l], cache_b_k[l], cache_b_v[l], cache_b_logf[l],
                state_c_h[l], state_c_conv[l], state_d_s[l], state_d_conv[l])
        xs2, peer_s, st_s, kv_s = _layer(xs, sb, ss, past, p, pw, l, lam_init, scfg, kv_s)
        xs = _residual(xs2, peer_s, final_norm_g, scfg['tm_res'], last)
        sample_out.append(st_s)

    y_prompt = xp.reshape(pb, ps, D_MODEL)
    y_sample = xs.reshape(sb, ss, D_MODEL)
    def leaves(kv, small, bsz, s):
        per_head = [a.reshape(DEPTH, bsz, s, _ATTN_HEADS, _ATTN_DIM) for a in kv]
        return per_head + [jnp.stack(z, axis=0) for z in zip(*small)]

    return (y_prompt, y_sample, *leaves(kv_p, prompt_out, pb, ps), *leaves(kv_s, sample_out, sb, ss))
```
